```python
import math
import jax, jax.numpy as jnp
from jax import lax
import numpy as np

D_MODEL = 1024
BATCH = 2
SEQ = 8192
DEPTH = 1

N_MEM = 256
EPS = 1e-6
MLA_HEADS = 4
QK_NOPE = 128
QK_ROPE = 64
QK_HEAD = QK_NOPE + QK_ROPE
V_HEAD = 128
Q_LORA = 384
KV_LORA = 256
ROPE_THETA = 10000.0
Q_BLOCK = 128
HG_HEADS = 4
HG_DK = 128
HG_DV = 128
HG_CHUNK = 64
MEM_HEADS = 4
MEM_HEAD_DIM = 128
MLA_WIDTH = MLA_HEADS * V_HEAD
HG_KWIDTH = HG_HEADS * HG_DK
HG_WIDTH = HG_HEADS * HG_DV
MEM_WIDTH = MEM_HEADS * MEM_HEAD_DIM
MIX_WIDTH = MLA_WIDTH + HG_WIDTH + MEM_WIDTH
D_FF = -(-8 * D_MODEL // (3 * 256)) * 256
IN_SIZES = (Q_LORA, KV_LORA, QK_ROPE, HG_KWIDTH, HG_KWIDTH, HG_WIDTH, HG_WIDTH, MEM_WIDTH)
IN_WIDTH = Q_LORA + KV_LORA + QK_ROPE + 2 * HG_KWIDTH + 2 * HG_WIDTH + MEM_WIDTH

kernel_name = 'hymba_mla_hgrn2_memxattn_swiglu'


def rmsnorm(x, g):
    xf = x.astype(jnp.float32)
    y = xf * lax.rsqrt(jnp.mean(xf * xf, axis=-1, keepdims=True) + EPS) * g.astype(jnp.float32)
    return y.astype(x.dtype)


def apply_rope(x, pos):
    half = QK_ROPE // 2
    inv_freq = jnp.power(ROPE_THETA, -jnp.arange(half, dtype=jnp.float32) / half)
    ang = pos.astype(jnp.float32)[:, :, None, None] * inv_freq
    cos, sin = jnp.cos(ang), jnp.sin(ang)
    xf = x.astype(jnp.float32)
    x1, x2 = xf[..., :half], xf[..., half:]
    return jnp.concatenate([x1 * cos - x2 * sin, x2 * cos + x1 * sin], axis=-1).astype(x.dtype)


def causal_block_attention(q, k, v):
    B, S, H, Dqk = q.shape
    Dv = v.shape[-1]
    nq = S // Q_BLOCK
    scale = Dqk ** -0.5
    qb = q.astype(jnp.float32).reshape(B, nq, Q_BLOCK, H, Dqk).transpose(1, 0, 2, 3, 4)
    starts = jnp.arange(nq, dtype=jnp.int32) * Q_BLOCK
    kf = k.astype(jnp.float32)
    vf = v.astype(jnp.float32)
    kpos = jnp.arange(S, dtype=jnp.int32)

    def one_block(args):
        qblk, start = args
        s = jnp.einsum('bqhd,bkhd->bhqk', qblk, kf) * scale
        qpos = start + jnp.arange(Q_BLOCK, dtype=jnp.int32)
        mask = kpos[None, :] <= qpos[:, None]
        s = jnp.where(mask[None, None], s, -jnp.inf)
        p = jax.nn.softmax(s, axis=-1)
        return jnp.einsum('bhqk,bkhd->bqhd', p, vf)

    out = lax.map(one_block, (qb, starts))
    return out.transpose(1, 0, 2, 3, 4).reshape(B, S, H, Dv).astype(v.dtype)


def mla_group(c_q, c_kv, k_rope, pos, q_a_norm, w_uq, kv_a_norm, w_ukv, q_norm, k_norm):
    B, S, _ = c_q.shape
    q = (rmsnorm(c_q, q_a_norm) @ w_uq).reshape(B, S, MLA_HEADS, QK_HEAD)
    kv = (rmsnorm(c_kv, kv_a_norm) @ w_ukv).reshape(B, S, MLA_HEADS, QK_NOPE + V_HEAD)
    k_nope, v = kv[..., :QK_NOPE], kv[..., QK_NOPE:]
    k_pe = jnp.broadcast_to(k_rope[:, :, None, :], (B, S, MLA_HEADS, QK_ROPE))
    k = jnp.concatenate([k_nope, k_pe], axis=-1)
    q = rmsnorm(q, q_norm)
    k = rmsnorm(k, k_norm)
    q = jnp.concatenate([q[..., :QK_NOPE], apply_rope(q[..., QK_NOPE:], pos)], axis=-1)
    k = jnp.concatenate([k[..., :QK_NOPE], apply_rope(k[..., QK_NOPE:], pos)], axis=-1)
    o = causal_block_attention(q, k, v)
    return o.reshape(B, S, MLA_WIDTH)


def hgrn2_group(q_raw, f_raw, i_raw, g_raw, lb, out_gain):
    B, S, _ = q_raw.shape
    n = S // HG_CHUNK
    f32 = jnp.float32
    q = jax.nn.silu(q_raw.astype(f32)) * (HG_DK ** -0.5)
    lbf = lb.astype(f32)
    f = lbf + (1.0 - lbf) * jax.nn.sigmoid(f_raw.astype(f32))
    k = 1.0 - f
    logf = jnp.log(f)
    v = i_raw.astype(f32)

    def chunks(t, d):
        return t.reshape(B, n, HG_CHUNK, HG_HEADS, d).transpose(1, 0, 3, 2, 4)

    qc, kc, vc = chunks(q, HG_DK), chunks(k, HG_DK), chunks(v, HG_DV)
    bc = jnp.cumsum(chunks(logf, HG_DK), axis=3)
    causal = jnp.tril(jnp.ones((HG_CHUNK, HG_CHUNK), dtype=bool))

    def chunk_step(state, xs):
        qx, kx, vx, bx = xs
        diff = bx[:, :, :, None, :] - bx[:, :, None, :, :]
        decay = jnp.where(causal[None, None, :, :, None], jnp.exp(jnp.minimum(diff, 0.0)), 0.0)
        a = jnp.einsum('bhtd,bhsd,bhtsd->bhts', qx, kx, decay)
        o = (jnp.einsum('bhts,bhse->bhte', a, vx)
             + jnp.einsum('bhtd,bhde->bhte', qx * jnp.exp(bx), state))
        b_last = bx[:, :, -1:, :]
        new_state = (jnp.exp(b_last[:, :, 0, :])[..., None] * state
                     + jnp.einsum('bhsd,bhse->bhde', kx * jnp.exp(b_last - bx), vx))
        return new_state, o

    s0 = jnp.zeros((B, HG_HEADS, HG_DK, HG_DV), f32)
    _, o = lax.scan(chunk_step, s0, (qc, kc, vc, bc))
    o = o.transpose(1, 0, 3, 2, 4).reshape(B, S, HG_HEADS, HG_DV)
    o = rmsnorm(o, out_gain.reshape(HG_HEADS, HG_DV)).reshape(B, S, HG_WIDTH)
    return (o * jax.nn.silu(g_raw.astype(f32))).astype(q_raw.dtype)


def mem_group(q_raw, mem_h, w_mem_kv, q_norm, k_norm):
    B, S, _ = q_raw.shape
    M = mem_h.shape[1]
    q = rmsnorm(q_raw.reshape(B, S, MEM_HEADS, MEM_HEAD_DIM), q_norm)
    kv = (mem_h @ w_mem_kv).reshape(B, M, 2, MEM_HEADS, MEM_HEAD_DIM)
    k = rmsnorm(kv[:, :, 0], k_norm)
    v = kv[:, :, 1]
    s = jnp.einsum('bqhd,bkhd->bhqk', q.astype(jnp.float32), k.astype(jnp.float32)) * (MEM_HEAD_DIM ** -0.5)
    p = jax.nn.softmax(s, axis=-1)
    o = jnp.einsum('bhqk,bkhd->bqhd', p, v.astype(jnp.float32))
    return o.reshape(B, S, MEM_WIDTH).astype(q_raw.dtype)


def setup_inputs(seed: int = 0) -> dict:
    key = jax.random.key(seed)
    ks = jax.random.split(key, 32)
    f32 = jnp.float32

    def dense(k, shape, fan_in):
        return jax.random.normal(k, shape, f32) * (fan_in ** -0.5)

    def gain(k, shape):
        return 1.0 + 0.05 * jax.random.normal(k, shape, f32)

    L = DEPTH
    x = jax.random.normal(ks[0], (BATCH, SEQ, D_MODEL), f32)
    mem = jax.random.normal(ks[1], (BATCH, N_MEM, D_MODEL), f32)
    offset = jax.random.randint(ks[2], (BATCH, 1), 0, 4096, dtype=jnp.int32)
    positions = offset + jnp.arange(SEQ, dtype=jnp.int32)[None, :]
    return {
        'x': x,
        'mem': mem,
        'positions': positions,
        'norm_mix': gain(ks[3], (L, D_MODEL)),
        'norm_mem': gain(ks[4], (L, D_MODEL)),
        'w_in': dense(ks[5], (L, D_MODEL, IN_WIDTH), D_MODEL),
        'q_a_norm': gain(ks[6], (L, Q_LORA)),
        'w_uq': dense(ks[7], (L, Q_LORA, MLA_HEADS * QK_HEAD), Q_LORA),
        'kv_a_norm': gain(ks[8], (L, KV_LORA)),
        'w_ukv': dense(ks[9], (L, KV_LORA, MLA_HEADS * (QK_NOPE + V_HEAD)), KV_LORA),
        'mla_q_norm': gain(ks[10], (L, QK_HEAD)),
        'mla_k_norm': gain(ks[11], (L, QK_HEAD)),
        'hg_lb_logits': 0.1 * jax.random.normal(ks[12], (L + 1, HG_KWIDTH), f32),
        'hg_out_norm': gain(ks[13], (L, HG_WIDTH)),
        'w_mem_kv': dense(ks[14], (L, D_MODEL, 2 * MEM_WIDTH), D_MODEL),
        'mem_q_norm': gain(ks[15], (L, MEM_HEAD_DIM)),
        'mem_k_norm': gain(ks[16], (L, MEM_HEAD_DIM)),
        'mla_out_norm': gain(ks[17], (L, MLA_WIDTH)),
        'mem_out_norm': gain(ks[18], (L, MEM_WIDTH)),
        'w_out': dense(ks[19], (L, MIX_WIDTH, D_MODEL), MIX_WIDTH),
        'norm_ffn': gain(ks[20], (L, D_MODEL)),
        'w_gate': dense(ks[21], (L, D_MODEL, D_FF), D_MODEL),
        'w_up': dense(ks[22], (L, D_MODEL, D_FF), D_MODEL),
        'w_down': dense(ks[23], (L, D_FF, D_MODEL), D_FF),
    }


def reference(x, mem, positions, norm_mix, norm_mem, w_in, q_a_norm, w_uq, kv_a_norm, w_ukv,
              mla_q_norm, mla_k_norm, hg_lb_logits, hg_out_norm, w_mem_kv, mem_q_norm, mem_k_norm,
              mla_out_norm, mem_out_norm, w_out, norm_ffn, w_gate, w_up, w_down):
    lb_all = jnp.cumsum(jax.nn.softmax(hg_lb_logits.astype(jnp.float32), axis=0), axis=0)
    offsets = [0]
    for sz in IN_SIZES:
        offsets.append(offsets[-1] + sz)
    for l in range(DEPTH):
        h = rmsnorm(x, norm_mix[l])
        mem_h = rmsnorm(mem, norm_mem[l])
        proj = h @ w_in[l]
        c_q, c_kv, k_rope, hq, hf, hi, hg, mq = [proj[..., offsets[j]:offsets[j + 1]] for j in range(len(IN_SIZES))]
        y_mla = mla_group(c_q, c_kv, k_rope, positions, q_a_norm[l], w_uq[l], kv_a_norm[l], w_ukv[l],
                          mla_q_norm[l], mla_k_norm[l])
        y_hg = hgrn2_group(hq, hf, hi, hg, lb_all[l], hg_out_norm[l])
        y_mem = mem_group(mq, mem_h, w_mem_kv[l], mem_q_norm[l], mem_k_norm[l])
        mix = jnp.concatenate([rmsnorm(y_mla, mla_out_norm[l]), y_hg, rmsnorm(y_mem, mem_out_norm[l])], axis=-1)
        x = x + (mix @ w_out[l]).astype(x.dtype)
        h2 = rmsnorm(x, norm_ffn[l])
        x = x + ((jax.nn.silu(h2 @ w_gate[l]) * (h2 @ w_up[l])) @ w_down[l]).astype(x.dtype)
    return x
```

```python
import functools

import jax
import jax.numpy as jnp
from jax import lax
from jax.experimental import pallas as pl
from jax.experimental.pallas import tpu as pltpu

F32 = jnp.float32
BF16 = jnp.bfloat16

EPS = 1e-6
N_HEADS = 4
D_NOPE = 128
D_ROPE = 64
D_QK = D_NOPE + D_ROPE
D_QK_PAD = 256
D_V = 128
Q_LORA = 384
KV_LORA = 256
ROPE_THETA = 10000.0
HG_D = 128
MEM_D = 128
WIDTH = N_HEADS * 128

LANE = 128
HG_CHUNK = 128
HG_SUB = 8

VMEM_LIMIT = 56 * 1024 * 1024


def _dot(a, b):
    return jnp.dot(a, b, preferred_element_type=F32)


def _dot_nt(a, b):
    return lax.dot_general(a, b, (((1,), (1,)), ((), ())), preferred_element_type=F32)


def _rms(x, g, width):
    ss = jnp.sum(x * x, axis=-1, keepdims=True)
    return x * lax.rsqrt(ss * (1.0 / width) + EPS) * g


def _sigmoid(x):
    return 1.0 / (1.0 + jnp.exp(-x))


def _mem_kv_kernel(mem_ref, g_ref, w_ref, kn_ref, k_out, v_out):
    m = mem_ref[0].astype(F32)
    mh = _rms(m, g_ref[...], m.shape[-1]).astype(BF16)
    kv = _dot(mh, w_ref[...])
    for h in range(N_HEADS):
        kh = kv[:, h * MEM_D:(h + 1) * MEM_D]
        k_out[0, :, h * MEM_D:(h + 1) * MEM_D] = _rms(kh, kn_ref[...], MEM_D).astype(BF16)
    v_out[0] = kv[:, WIDTH:].astype(BF16)


def _in_proj_kernel(x_ref, pos_ref, invf_ref, gmix_ref,
                    w_cq_ref, w_ckv_ref, w_kr_ref, w_hq_ref, w_hf_ref, w_hi_ref, w_hg_ref, w_mq_ref,
                    gqa_ref, w_uqa_ref, w_uqb_ref, gkva_ref, w_uk_ref, w_uv_ref,
                    gq_a_ref, gq_b_ref, gk_nope_ref, gk_ra_ref, gk_rb_ref,
                    gmq_ref, kmem_ref, vmem_ref, gmo_ref,
                    q_out, k_out, v_out, hq_out, hf_out, hi_out, hg_out, ymem_out):
    x = x_ref[...].astype(F32)
    h = _rms(x, gmix_ref[...], x.shape[-1]).astype(BF16)

    ang = pos_ref[...].astype(F32) * invf_ref[...]
    cos = jnp.cos(ang)
    sin = jnp.sin(ang)

    cq = _dot(h, w_cq_ref[...])
    cqn = _rms(cq, gqa_ref[...], Q_LORA).astype(BF16)
    qa = _dot(cqn, w_uqa_ref[...])
    qb = _dot(cqn, w_uqb_ref[...])
    q_scale = D_QK ** -0.5
    for hd in range(N_HEADS):
        a = qa[:, hd * D_QK_PAD:(hd + 1) * D_QK_PAD]
        ss = jnp.sum(a * a, axis=-1, keepdims=True)
        rinv = lax.rsqrt(ss * (1.0 / D_QK) + EPS) * q_scale
        nope = a[:, :D_NOPE] * gq_a_ref[:, :D_NOPE] * rinv
        rope = (a[:, D_NOPE:] * gq_a_ref[:, D_NOPE:] * cos
                + qb[:, hd * LANE:(hd + 1) * LANE] * gq_b_ref[...] * sin) * rinv
        q_out[:, hd * D_QK_PAD:hd * D_QK_PAD + D_NOPE] = nope.astype(BF16)
        q_out[:, hd * D_QK_PAD + D_NOPE:(hd + 1) * D_QK_PAD] = rope.astype(BF16)

    ckv = _dot(h, w_ckv_ref[...])
    ckvn = _rms(ckv, gkva_ref[...], KV_LORA).astype(BF16)
    kn = _dot(ckvn, w_uk_ref[...])
    v_out[...] = _dot(ckvn, w_uv_ref[...]).astype(BF16)
    kr = _dot(h, w_kr_ref[...])
    kra = kr[:, :LANE]
    ss_r = jnp.sum(kra * kra, axis=-1, keepdims=True)
    rot = kra * gk_ra_ref[...] * cos + kr[:, LANE:] * gk_rb_ref[...] * sin
    for hd in range(N_HEADS):
        a = kn[:, hd * D_NOPE:(hd + 1) * D_NOPE]
        ss = jnp.sum(a * a, axis=-1, keepdims=True) + ss_r
        rinv = lax.rsqrt(ss * (1.0 / D_QK) + EPS)
        k_out[:, hd * D_QK_PAD:hd * D_QK_PAD + D_NOPE] = (a * gk_nope_ref[...] * rinv).astype(BF16)
        k_out[:, hd * D_QK_PAD + D_NOPE:(hd + 1) * D_QK_PAD] = (rot * rinv).astype(BF16)

    hq = _dot(h, w_hq_ref[...])
    hq_out[...] = (hq * _sigmoid(hq) * (HG_D ** -0.5)).astype(BF16)
    hf_out[...] = _dot(h, w_hf_ref[...])
    hi_out[...] = _dot(h, w_hi_ref[...]).astype(BF16)
    hg = _dot(h, w_hg_ref[...])
    hg_out[...] = (hg * _sigmoid(hg)).astype(BF16)

    mq = _dot(h, w_mq_ref[...])
    ys = []
    for hd in range(N_HEADS):
        sl = slice(hd * MEM_D, (hd + 1) * MEM_D)
        qh = (_rms(mq[:, sl], gmq_ref[...], MEM_D) * (MEM_D ** -0.5)).astype(BF16)
        s = _dot_nt(qh, kmem_ref[0, :, sl])
        p = jnp.exp(s - jnp.max(s, axis=-1, keepdims=True))
        l = jnp.sum(p, axis=-1, keepdims=True)
        ys.append(_dot(p.astype(BF16), vmem_ref[0, :, sl]) / l)
    y = jnp.concatenate(ys, axis=-1)
    ymem_out[...] = _rms(y, gmo_ref[...], WIDTH).astype(BF16)


def _attn_kernel(q_ref, k_ref, v_ref, o_ref, m_sc, l_sc, acc_sc, *, blk):
    qi = pl.program_id(2)
    q = q_ref[0]
    m_sc[...] = jnp.full(m_sc.shape, -jnp.inf, F32)
    l_sc[...] = jnp.zeros(l_sc.shape, F32)
    acc_sc[...] = jnp.zeros(acc_sc.shape, F32)

    def step(j, masked):
        r0 = pl.multiple_of(j * blk, blk)
        s = _dot_nt(q, k_ref[0, pl.ds(r0, blk), :])
        if masked:
            row = lax.broadcasted_iota(jnp.int32, (blk, blk), 0)
            col = lax.broadcasted_iota(jnp.int32, (blk, blk), 1)
            s = jnp.where(col <= row, s, -jnp.inf)
        m_prev = m_sc[...]
        m_new = jnp.maximum(m_prev, jnp.max(s, axis=-1, keepdims=True))
        p = jnp.exp(s - m_new)
        alpha = jnp.exp(m_prev - m_new)
        l_sc[...] = alpha * l_sc[...] + jnp.sum(p, axis=-1, keepdims=True)
        acc_sc[...] = alpha * acc_sc[...] + _dot(p.astype(BF16), v_ref[0, pl.ds(r0, blk), :])
        m_sc[...] = m_new

    def body(j, carry):
        step(j, False)
        return carry

    lax.fori_loop(0, qi, body, 0)
    step(qi, True)
    o_ref[0] = (acc_sc[...] / l_sc[...]).astype(o_ref.dtype)


def _pair_reference(b, m):
    c = b.shape[0]
    n2 = c // (2 * m)
    br = b.reshape(n2, 2 * m, b.shape[1])
    last = br[:, m - 1:m, :]
    return jnp.broadcast_to(last, br.shape).reshape(b.shape)


def _hgrn_kernel(hq_ref, hf_ref, hi_ref, hg_ref, lbl_ref, gain_ref, o_ref, st_ref, *, n_chunks, layer):
    C = HG_CHUNK

    @pl.when(pl.program_id(1) == 0)
    def _():
        st_ref[...] = jnp.zeros(st_ref.shape, F32)

    lg = lbl_ref[...].astype(F32)
    e = jnp.exp(lg - jnp.max(lg, axis=0, keepdims=True))
    lb = jnp.sum(e[:layer + 1], axis=0, keepdims=True) / jnp.sum(e, axis=0, keepdims=True)

    row = lax.broadcasted_iota(jnp.int32, (C, C), 0)
    col = lax.broadcasted_iota(jnp.int32, (C, C), 1)
    tri = (col <= row).astype(BF16)
    diag_off = jnp.where(col <= row, col - (row & ~(HG_SUB - 1)), -1)
    levels = []
    m = HG_SUB
    while m < C:
        bad = ((row ^ col) & ~(2 * m - 1)) | ((row & m) ^ m) | (col & m)
        levels.append((m, bad == 0))
        m *= 2

    def chunk(c, carry):
        r0 = pl.multiple_of(c * C, C)
        fr = hf_ref[0, pl.ds(r0, C), :]
        f = lb + (1.0 - lb) * _sigmoid(fr)
        logf = jnp.log(f)
        kk_all = 1.0 - f
        t0 = logf.astype(BF16)
        r1 = logf - t0.astype(F32)
        t1 = r1.astype(BF16)
        t2 = (r1 - t1.astype(F32)).astype(BF16)
        b_all = _dot(tri, t0) + _dot(tri, t1) + _dot(tri, t2)
        q_all = hq_ref[0, pl.ds(r0, C), :].astype(F32)
        v_all = hi_ref[0, pl.ds(r0, C), :]
        g_all = hg_ref[0, pl.ds(r0, C), :].astype(F32)
        for hd in range(N_HEADS):
            sl = slice(hd * HG_D, (hd + 1) * HG_D)
            b = b_all[:, sl]
            q = q_all[:, sl]
            kk = kk_all[:, sl]
            v = v_all[:, sl]
            kb = kk.astype(BF16)

            b3 = b.reshape(C // HG_SUB, HG_SUB, HG_D)
            ms = []
            for s_off in range(HG_SUB):
                bs = jnp.broadcast_to(b3[:, s_off:s_off + 1, :], b3.shape).reshape(C, HG_D)
                ms.append((q * jnp.exp(jnp.minimum(b - bs, 0.0))).astype(BF16))
            ad = _dot_nt(jnp.concatenate(ms, axis=0), kb)
            a = jnp.zeros((C, C), F32)
            for s_off in range(HG_SUB):
                a = jnp.where(diag_off == s_off, ad[s_off * C:(s_off + 1) * C], a)
            for m_blk, mask in levels:
                d = b - _pair_reference(b, m_blk)
                qe = (q * jnp.exp(jnp.minimum(d, 0.0))).astype(BF16)
                ke = (kk * jnp.exp(jnp.minimum(-d, 0.0))).astype(BF16)
                a = jnp.where(mask, _dot_nt(qe, ke), a)

            st = st_ref[hd]
            o = _dot(a.astype(BF16), v) + _dot_nt((q * jnp.exp(b)).astype(BF16), st.astype(BF16))
            b_last = b[C - 1:C, :]
            kd = (kk * jnp.exp(b_last - b)).astype(BF16)
            vt = v.astype(F32).T.astype(BF16)
            st_ref[hd] = st * jnp.exp(b_last) + _dot(vt, kd)

            on = _rms(o, gain_ref[:, sl], HG_D)
            o_ref[0, pl.ds(r0, C), sl] = (on * g_all[:, sl]).astype(o_ref.dtype)
        return carry

    lax.fori_loop(0, n_chunks, chunk, 0)


def _out_ffn_kernel(x_ref, ymla_ref, yhg_ref, ymem_ref, gmla_ref, w_out_ref, gffn_ref,
                    w_gate_ref, w_up_ref, w_down_ref, o_ref):
    x = x_ref[...].astype(F32)
    ymla = _rms(ymla_ref[...].astype(F32), gmla_ref[...], WIDTH).astype(BF16)
    mix = (_dot(ymla, w_out_ref[0:WIDTH, :])
           + _dot(yhg_ref[...], w_out_ref[WIDTH:2 * WIDTH, :])
           + _dot(ymem_ref[...], w_out_ref[2 * WIDTH:3 * WIDTH, :]))
    x1 = x + mix
    h2 = _rms(x1, gffn_ref[...], x1.shape[-1]).astype(BF16)
    g = _dot(h2, w_gate_ref[...])
    u = _dot(h2, w_up_ref[...])
    act = (g * _sigmoid(g) * u).astype(BF16)
    o_ref[...] = (x1 + _dot(act, w_down_ref[...])).astype(o_ref.dtype)


def _full(shape):
    nd = len(shape)
    return pl.BlockSpec(shape, lambda *_: (0,) * nd)


def _params(sem):
    return pltpu.CompilerParams(dimension_semantics=sem, vmem_limit_bytes=VMEM_LIMIT)


def _row(v):
    return v.reshape(1, -1).astype(F32)


def _layer(x, mem, positions, layer, norm_mix, norm_mem, w_in, q_a_norm, w_uq, kv_a_norm, w_ukv,
           mla_q_norm, mla_k_norm, hg_lb_logits, hg_out_norm, w_mem_kv, mem_q_norm, mem_k_norm,
           mla_out_norm, mem_out_norm, w_out, norm_ffn, w_gate, w_up, w_down):
    B, S, D = x.shape
    M = mem.shape[1]
    T = B * S
    half = D_ROPE // 2
    H = N_HEADS

    sizes = (Q_LORA, KV_LORA, D_ROPE, WIDTH, WIDTH, WIDTH, WIDTH, WIDTH)
    offs = [0]
    for sz in sizes:
        offs.append(offs[-1] + sz)
    w_cq, w_ckv, w_kr, w_hq, w_hf, w_hi, w_hg, w_mq = [
        w_in[:, offs[j]:offs[j + 1]] for j in range(len(sizes))]
    zeros = lambda r, c: jnp.zeros((r, c), w_in.dtype)
    w_kr2 = jnp.concatenate([w_kr, zeros(D, LANE - D_ROPE),
                             -w_kr[:, half:], w_kr[:, :half], zeros(D, LANE - D_ROPE)], axis=1)
    uq = w_uq.reshape(Q_LORA, H, D_QK)
    uq_a = jnp.concatenate([uq, jnp.zeros((Q_LORA, H, D_QK_PAD - D_QK), w_uq.dtype)], axis=2)
    uq_b = jnp.concatenate([-uq[:, :, D_NOPE + half:], uq[:, :, D_NOPE:D_NOPE + half],
                            jnp.zeros((Q_LORA, H, LANE - D_ROPE), w_uq.dtype)], axis=2)
    ukv = w_ukv.reshape(KV_LORA, H, D_NOPE + D_V)
    w_uk = ukv[:, :, :D_NOPE].reshape(KV_LORA, H * D_NOPE)
    w_uv = ukv[:, :, D_NOPE:].reshape(KV_LORA, H * D_V)
    bf = lambda w: w.astype(BF16)

    pad = jnp.zeros((LANE - D_ROPE,), F32)
    gq = mla_q_norm.astype(F32)
    gk = mla_k_norm.astype(F32)
    gq_a = jnp.concatenate([gq, pad]).reshape(1, D_QK_PAD)
    gq_b = jnp.concatenate([gq[D_NOPE + half:], gq[D_NOPE:D_NOPE + half], pad]).reshape(1, LANE)
    gk_nope = gk[:D_NOPE].reshape(1, D_NOPE)
    gk_ra = jnp.concatenate([gk[D_NOPE:], pad]).reshape(1, LANE)
    gk_rb = jnp.concatenate([gk[D_NOPE + half:], gk[D_NOPE:D_NOPE + half], pad]).reshape(1, LANE)
    inv_freq = jnp.power(ROPE_THETA, -jnp.arange(half, dtype=F32) / half)
    invf = jnp.concatenate([inv_freq, inv_freq, pad]).reshape(1, LANE)

    kmem, vmem = pl.pallas_call(
        _mem_kv_kernel,
        grid=(B,),
        in_specs=[pl.BlockSpec((1, M, D), lambda b: (b, 0, 0)),
                  _full((1, D)), _full((D, 2 * WIDTH)), _full((1, MEM_D))],
        out_specs=[pl.BlockSpec((1, M, WIDTH), lambda b: (b, 0, 0))] * 2,
        out_shape=[jax.ShapeDtypeStruct((B, M, WIDTH), BF16)] * 2,
        compiler_params=_params(("arbitrary",)),
        name="mem_kv",
    )(mem, _row(norm_mem), bf(w_mem_kv), _row(mem_k_norm))

    tm = min(512, S)
    assert S % tm == 0
    steps_per_batch = S // tm
    x2 = x.reshape(T, D)
    pos2 = positions.reshape(T, 1).astype(jnp.int32)
    tok = lambda w: pl.BlockSpec((tm, w), lambda i: (i, 0))
    weights = [bf(w_cq), bf(w_ckv), bf(w_kr2), bf(w_hq), bf(w_hf), bf(w_hi), bf(w_hg), bf(w_mq)]
    rest = [_row(q_a_norm), bf(uq_a.reshape(Q_LORA, H * D_QK_PAD)), bf(uq_b.reshape(Q_LORA, H * LANE)),
            _row(kv_a_norm), bf(w_uk), bf(w_uv), gq_a, gq_b, gk_nope, gk_ra, gk_rb, _row(mem_q_norm)]
    mem_spec = pl.BlockSpec((1, M, WIDTH), lambda i: (i // steps_per_batch, 0, 0))
    q_all, k_all, v_all, hq, hf, hi, hg, ymem = pl.pallas_call(
        _in_proj_kernel,
        grid=(T // tm,),
        in_specs=([tok(D), tok(1), _full((1, LANE)), _full((1, D))]
                  + [_full(w.shape) for w in weights] + [_full(r.shape) for r in rest]
                  + [mem_spec, mem_spec, _full((1, WIDTH))]),
        out_specs=[tok(H * D_QK_PAD), tok(H * D_QK_PAD)] + [tok(WIDTH)] * 6,
        out_shape=[jax.ShapeDtypeStruct((T, H * D_QK_PAD), BF16)] * 2
        + [jax.ShapeDtypeStruct((T, WIDTH), dt) for dt in (BF16, BF16, F32, BF16, BF16, BF16)],
        compiler_params=_params(("arbitrary",)),
        name="in_proj",
    )(x2, pos2, invf, _row(norm_mix), *weights, *rest, kmem, vmem, _row(mem_out_norm))

    blk = min(512, S)
    assert S % blk == 0
    y_mla = pl.pallas_call(
        functools.partial(_attn_kernel, blk=blk),
        grid=(B, H, S // blk),
        in_specs=[pl.BlockSpec((1, blk, D_QK_PAD), lambda b, h, i: (b, i, h)),
                  pl.BlockSpec((1, S, D_QK_PAD), lambda b, h, i: (b, 0, h)),
                  pl.BlockSpec((1, S, D_V), lambda b, h, i: (b, 0, h))],
        out_specs=pl.BlockSpec((1, blk, D_V), lambda b, h, i: (b, i, h)),
        out_shape=jax.ShapeDtypeStruct((B, S, H * D_V), BF16),
        scratch_shapes=[pltpu.VMEM((blk, 1), F32), pltpu.VMEM((blk, 1), F32), pltpu.VMEM((blk, D_V), F32)],
        compiler_params=_params(("arbitrary", "arbitrary", "arbitrary")),
        name="mla_attn",
    )(q_all.reshape(B, S, H * D_QK_PAD), k_all.reshape(B, S, H * D_QK_PAD), v_all.reshape(B, S, H * D_V))

    ts = min(512, S)
    assert S % ts == 0 and ts % HG_CHUNK == 0
    seq = lambda: pl.BlockSpec((1, ts, WIDTH), lambda b, i: (b, i, 0))
    n_layers = hg_lb_logits.shape[0]
    y_hg = pl.pallas_call(
        functools.partial(_hgrn_kernel, n_chunks=ts // HG_CHUNK, layer=layer),
        grid=(B, S // ts),
        in_specs=[seq(), seq(), seq(), seq(), _full((n_layers, WIDTH)), _full((1, WIDTH))],
        out_specs=seq(),
        out_shape=jax.ShapeDtypeStruct((B, S, WIDTH), BF16),
        scratch_shapes=[pltpu.VMEM((N_HEADS, HG_D, HG_D), F32)],
        compiler_params=_params(("arbitrary", "arbitrary")),
        name="hgrn",
    )(hq.reshape(B, S, WIDTH), hf.reshape(B, S, WIDTH), hi.reshape(B, S, WIDTH), hg.reshape(B, S, WIDTH),
      hg_lb_logits.astype(F32), _row(hg_out_norm))

    d_ff = w_gate.shape[1]
    once = lambda shape: pl.BlockSpec(shape, lambda i: (0, 0), pipeline_mode=pl.Buffered(1))
    out = pl.pallas_call(
        _out_ffn_kernel,
        grid=(T // tm,),
        in_specs=[tok(D), tok(WIDTH), tok(WIDTH), tok(WIDTH), _full((1, WIDTH)), once((3 * WIDTH, D)),
                  _full((1, D)), once((D, d_ff)), once((D, d_ff)), once((d_ff, D))],
        out_specs=tok(D),
        out_shape=jax.ShapeDtypeStruct((T, D), x.dtype),
        compiler_params=_params(("arbitrary",)),
        name="out_ffn",
    )(x2, y_mla.reshape(T, WIDTH), y_hg.reshape(T, WIDTH), ymem, _row(mla_out_norm), bf(w_out),
      _row(norm_ffn), bf(w_gate), bf(w_up), bf(w_down))
    return out.reshape(B, S, D)


def kernel(x, mem, positions, norm_mix, norm_mem, w_in, q_a_norm, w_uq, kv_a_norm, w_ukv, mla_q_norm, mla_k_norm, hg_lb_logits, hg_out_norm, w_mem_kv, mem_q_norm, mem_k_norm, mla_out_norm, mem_out_norm, w_out, norm_ffn, w_gate, w_up, w_down):
    depth = w_in.shape[0]
    for l in range(depth):
        x = _layer(x, mem, positions, l, norm_mix[l], norm_mem[l], w_in[l], q_a_norm[l], w_uq[l],
                   kv_a_norm[l], w_ukv[l], mla_q_norm[l], mla_k_norm[l], hg_lb_logits, hg_out_norm[l],
                   w_mem_kv[l], mem_q_norm[l], mem_k_norm[l], mla_out_norm[l], mem_out_norm[l],
                   w_out[l], norm_ffn[l], w_gate[l], w_up[l], w_down[l])
    return x
```

```python
import functools

import jax
import jax.numpy as jnp
from jax import lax
from jax.experimental import pallas as pl
from jax.experimental.pallas import tpu as pltpu

F32 = jnp.float32
BF16 = jnp.bfloat16

EPS = 1e-6
N_HEADS = 4
D_NOPE = 128
D_ROPE = 64
D_QK = D_NOPE + D_ROPE
D_QK_PAD = 256
D_V = 128
Q_LORA = 384
KV_LORA = 256
ROPE_THETA = 10000.0
LOG2E = 1.4426950408889634
HG_D = 128
MEM_D = 128
WIDTH = N_HEADS * 128

LANE = 128
HG_CHUNK = 128
HG_SUB = 8

VMEM_LIMIT = 56 * 1024 * 1024


def _dot(a, b):
    return jnp.dot(a, b, preferred_element_type=F32)


def _dot_nt(a, b):
    return lax.dot_general(a, b, (((1,), (1,)), ((), ())), preferred_element_type=F32)


def _rms(x, g, width):
    ss = jnp.sum(x * x, axis=-1, keepdims=True)
    return x * lax.rsqrt(ss * (1.0 / width) + EPS) * g


def _sigmoid(x):
    return 1.0 / (1.0 + jnp.exp(-x))


def _mem_kv_kernel(mem_ref, g_ref, w_ref, kn_ref, k_out, v_out):
    m = mem_ref[0].astype(F32)
    mh = _rms(m, g_ref[...], m.shape[-1]).astype(BF16)
    kv = _dot(mh, w_ref[...])
    for h in range(N_HEADS):
        kh = kv[:, h * MEM_D:(h + 1) * MEM_D]
        k_out[0, :, h * MEM_D:(h + 1) * MEM_D] = _rms(kh, kn_ref[...], MEM_D).astype(BF16)
    v_out[0] = kv[:, WIDTH:].astype(BF16)


def _in_proj_kernel(x_ref, pos_ref, invf_ref, gmix_ref,
                    w_cq_ref, w_ckv_ref, w_kr_ref, w_hq_ref, w_hf_ref, w_hi_ref, w_hg_ref, w_mq_ref,
                    gqa_ref, w_uqa_ref, w_uqb_ref, gkva_ref, w_uk_ref, w_uv_ref,
                    gq_a_ref, gq_b_ref, gk_nope_ref, gk_ra_ref, gk_rb_ref,
                    gmq_ref, kmem_ref, vmem_ref, gmo_ref,
                    q_out, k_out, vt_out, hq_out, hf_out, hi_out, hg_out, ymem_out):
    x = x_ref[...].astype(F32)
    h = _rms(x, gmix_ref[...], x.shape[-1]).astype(BF16)

    ang = pos_ref[...].astype(F32) * invf_ref[...]
    cos = jnp.cos(ang)
    sin = jnp.sin(ang)

    cq = _dot(h, w_cq_ref[...])
    cqn = _rms(cq, gqa_ref[...], Q_LORA).astype(BF16)
    qa = _dot(cqn, w_uqa_ref[...])
    qb = _dot(cqn, w_uqb_ref[...])
    q_scale = LOG2E * D_QK ** -0.5
    for hd in range(N_HEADS):
        a = qa[:, hd * D_QK_PAD:(hd + 1) * D_QK_PAD]
        ss = jnp.sum(a * a, axis=-1, keepdims=True)
        rinv = lax.rsqrt(ss * (1.0 / D_QK) + EPS) * q_scale
        nope = a[:, :D_NOPE] * gq_a_ref[:, :D_NOPE] * rinv
        rope = (a[:, D_NOPE:] * gq_a_ref[:, D_NOPE:] * cos
                + qb[:, hd * LANE:(hd + 1) * LANE] * gq_b_ref[...] * sin) * rinv
        q_out[:, hd * D_QK_PAD:hd * D_QK_PAD + D_NOPE] = nope.astype(BF16)
        q_out[:, hd * D_QK_PAD + D_NOPE:(hd + 1) * D_QK_PAD] = rope.astype(BF16)

    ckv = _dot(h, w_ckv_ref[...])
    ckvn = _rms(ckv, gkva_ref[...], KV_LORA).astype(BF16)
    kn = _dot(ckvn, w_uk_ref[...])
    vt = _dot(ckvn, w_uv_ref[...]).T
    vt_out[0, :, 0] = vt.reshape(N_HEADS, D_V, vt.shape[-1]).astype(BF16)
    kr = _dot(h, w_kr_ref[...])
    kra = kr[:, :LANE]
    ss_r = jnp.sum(kra * kra, axis=-1, keepdims=True)
    rot = kra * gk_ra_ref[...] * cos + kr[:, LANE:] * gk_rb_ref[...] * sin
    for hd in range(N_HEADS):
        a = kn[:, hd * D_NOPE:(hd + 1) * D_NOPE]
        ss = jnp.sum(a * a, axis=-1, keepdims=True) + ss_r
        rinv = lax.rsqrt(ss * (1.0 / D_QK) + EPS)
        k_out[:, hd * D_QK_PAD:hd * D_QK_PAD + D_NOPE] = (a * gk_nope_ref[...] * rinv).astype(BF16)
        k_out[:, hd * D_QK_PAD + D_NOPE:(hd + 1) * D_QK_PAD] = (rot * rinv).astype(BF16)

    hq = _dot(h, w_hq_ref[...])
    hq_out[...] = (hq * _sigmoid(hq) * (HG_D ** -0.5)).astype(BF16)
    hf_out[...] = _dot(h, w_hf_ref[...])
    hi_out[...] = _dot(h, w_hi_ref[...]).astype(BF16)
    hg = _dot(h, w_hg_ref[...])
    hg_out[...] = (hg * _sigmoid(hg)).astype(BF16)

    mq = _dot(h, w_mq_ref[...])
    ys = []
    for hd in range(N_HEADS):
        sl = slice(hd * MEM_D, (hd + 1) * MEM_D)
        qh = (_rms(mq[:, sl], gmq_ref[...], MEM_D) * (MEM_D ** -0.5)).astype(BF16)
        s = _dot_nt(qh, kmem_ref[0, :, sl])
        p = jnp.exp(s - jnp.max(s, axis=-1, keepdims=True))
        l = jnp.sum(p, axis=-1, keepdims=True)
        ys.append(_dot(p.astype(BF16), vmem_ref[0, :, sl]) / l)
    y = jnp.concatenate(ys, axis=-1)
    ymem_out[...] = _rms(y, gmo_ref[...], WIDTH).astype(BF16)


def _attn_kernel(q_ref, k_ref, vt_ref, o_ref, m_sc, l_sc, acc_sc, s0_sc, s1_sc, cm0_sc, cm1_sc,
                 *, blk, heads):
    qi = pl.program_id(2)
    s_bufs = (s0_sc, s1_sc)
    cm_bufs = (cm0_sc, cm1_sc)
    m_sc[...] = jnp.full(m_sc.shape, -jnp.inf, F32)
    l_sc[...] = jnp.zeros(l_sc.shape, F32)
    acc_sc[...] = jnp.zeros(acc_sc.shape, F32)

    def scores(t, slot, hd):
        r0 = t * blk if isinstance(t, int) else pl.multiple_of(t * blk, blk)
        q = q_ref[0, :, hd * D_QK_PAD:(hd + 1) * D_QK_PAD]
        k = k_ref[0, pl.ds(r0, blk), hd * D_QK_PAD:(hd + 1) * D_QK_PAD]
        s = _dot_nt(k, q)
        s_bufs[slot][hd] = s
        cm_bufs[slot][hd] = jnp.max(s, axis=0, keepdims=True)

    def accumulate(t, slot, hd, masked):
        s = s_bufs[slot][hd]
        if masked:
            kv = lax.broadcasted_iota(jnp.int32, (blk, blk), 0)
            qq = lax.broadcasted_iota(jnp.int32, (blk, blk), 1)
            s = jnp.where(kv <= qq, s, -jnp.inf)
            cm = jnp.max(s, axis=0, keepdims=True)
        else:
            cm = cm_bufs[slot][hd]
        m_prev = m_sc[hd]
        m_new = jnp.maximum(m_prev, cm)
        p = jnp.exp2(s - m_new)
        alpha = jnp.exp2(m_prev - m_new)
        l_sc[hd] = alpha * l_sc[hd] + jnp.sum(p, axis=0, keepdims=True)
        acc_sc[hd] = alpha * acc_sc[hd] + _dot(vt_ref[0, hd, t], p.astype(BF16))
        m_sc[hd] = m_new

    for hd in range(heads):
        scores(0, 0, hd)

    def body(j, carry):
        for slot in range(2):
            @pl.when((j & 1) == slot)
            def _():
                for hd in range(heads):
                    scores(j + 1, 1 - slot, hd)
                    accumulate(j, slot, hd, False)
        return carry

    lax.fori_loop(0, qi, body, 0)
    for slot in range(2):
        @pl.when((qi & 1) == slot)
        def _():
            for hd in range(heads):
                accumulate(qi, slot, hd, True)
    for hd in range(heads):
        o = (acc_sc[hd] / l_sc[hd]).T
        o_ref[0, :, hd * D_V:(hd + 1) * D_V] = o.astype(o_ref.dtype)


def _pair_reference(b, m):
    c = b.shape[0]
    n2 = c // (2 * m)
    br = b.reshape(n2, 2 * m, b.shape[1])
    last = br[:, m - 1:m, :]
    return jnp.broadcast_to(last, br.shape).reshape(b.shape)


def _hgrn_kernel(hq_ref, hf_ref, hi_ref, hg_ref, lbl_ref, gain_ref, o_ref, st_ref, *, n_chunks, layer):
    C = HG_CHUNK

    @pl.when(pl.program_id(1) == 0)
    def _():
        st_ref[...] = jnp.zeros(st_ref.shape, F32)

    lg = lbl_ref[...].astype(F32)
    e = jnp.exp(lg - jnp.max(lg, axis=0, keepdims=True))
    lb = jnp.sum(e[:layer + 1], axis=0, keepdims=True) / jnp.sum(e, axis=0, keepdims=True)

    row = lax.broadcasted_iota(jnp.int32, (C, C), 0)
    col = lax.broadcasted_iota(jnp.int32, (C, C), 1)
    tri = (col <= row).astype(BF16)
    diag_off = jnp.where(col <= row, col - (row & ~(HG_SUB - 1)), -1)
    levels = []
    m = HG_SUB
    while m < C:
        bad = ((row ^ col) & ~(2 * m - 1)) | ((row & m) ^ m) | (col & m)
        levels.append((m, bad == 0))
        m *= 2

    def chunk(c, carry):
        r0 = pl.multiple_of(c * C, C)
        fr = hf_ref[0, pl.ds(r0, C), :]
        f = lb + (1.0 - lb) * _sigmoid(fr)
        logf = jnp.log(f)
        kk_all = 1.0 - f
        t0 = logf.astype(BF16)
        r1 = logf - t0.astype(F32)
        t1 = r1.astype(BF16)
        t2 = (r1 - t1.astype(F32)).astype(BF16)
        b_all = _dot(tri, t0) + _dot(tri, t1) + _dot(tri, t2)
        q_all = hq_ref[0, pl.ds(r0, C), :].astype(F32)
        v_all = hi_ref[0, pl.ds(r0, C), :]
        g_all = hg_ref[0, pl.ds(r0, C), :].astype(F32)
        for hd in range(N_HEADS):
            sl = slice(hd * HG_D, (hd + 1) * HG_D)
            b = b_all[:, sl]
            q = q_all[:, sl]
            kk = kk_all[:, sl]
            v = v_all[:, sl]
            kb = kk.astype(BF16)

            b3 = b.reshape(C // HG_SUB, HG_SUB, HG_D)
            ms = []
            for s_off in range(HG_SUB):
                bs = jnp.broadcast_to(b3[:, s_off:s_off + 1, :], b3.shape).reshape(C, HG_D)
                ms.append((q * jnp.exp(jnp.minimum(b - bs, 0.0))).astype(BF16))
            ad = _dot_nt(jnp.concatenate(ms, axis=0), kb)
            a = jnp.zeros((C, C), F32)
            for s_off in range(HG_SUB):
                a = jnp.where(diag_off == s_off, ad[s_off * C:(s_off + 1) * C], a)
            for m_blk, mask in levels:
                d = b - _pair_reference(b, m_blk)
                qe = (q * jnp.exp(jnp.minimum(d, 0.0))).astype(BF16)
                ke = (kk * jnp.exp(jnp.minimum(-d, 0.0))).astype(BF16)
                a = jnp.where(mask, _dot_nt(qe, ke), a)

            st = st_ref[hd]
            o = _dot(a.astype(BF16), v) + _dot_nt((q * jnp.exp(b)).astype(BF16), st.astype(BF16))
            b_last = b[C - 1:C, :]
            kd = (kk * jnp.exp(b_last - b)).astype(BF16)
            vt = v.astype(F32).T.astype(BF16)
            st_ref[hd] = st * jnp.exp(b_last) + _dot(vt, kd)

            on = _rms(o, gain_ref[:, sl], HG_D)
            o_ref[0, pl.ds(r0, C), sl] = (on * g_all[:, sl]).astype(o_ref.dtype)
        return carry

    lax.fori_loop(0, n_chunks, chunk, 0)


def _out_ffn_kernel(x_ref, ymla_ref, yhg_ref, ymem_ref, gmla_ref, w_out_ref, gffn_ref,
                    w_gate_ref, w_up_ref, w_down_ref, o_ref):
    x = x_ref[...].astype(F32)
    ymla = _rms(ymla_ref[...].astype(F32), gmla_ref[...], WIDTH).astype(BF16)
    mix = (_dot(ymla, w_out_ref[0:WIDTH, :])
           + _dot(yhg_ref[...], w_out_ref[WIDTH:2 * WIDTH, :])
           + _dot(ymem_ref[...], w_out_ref[2 * WIDTH:3 * WIDTH, :]))
    x1 = x + mix
    h2 = _rms(x1, gffn_ref[...], x1.shape[-1]).astype(BF16)
    g = _dot(h2, w_gate_ref[...])
    u = _dot(h2, w_up_ref[...])
    act = (g * _sigmoid(g) * u).astype(BF16)
    o_ref[...] = (x1 + _dot(act, w_down_ref[...])).astype(o_ref.dtype)


def _full(shape):
    nd = len(shape)
    return pl.BlockSpec(shape, lambda *_: (0,) * nd)


def _params(sem):
    return pltpu.CompilerParams(dimension_semantics=sem, vmem_limit_bytes=VMEM_LIMIT)


def _row(v):
    return v.reshape(1, -1).astype(F32)


def _layer(x, mem, positions, layer, norm_mix, norm_mem, w_in, q_a_norm, w_uq, kv_a_norm, w_ukv,
           mla_q_norm, mla_k_norm, hg_lb_logits, hg_out_norm, w_mem_kv, mem_q_norm, mem_k_norm,
           mla_out_norm, mem_out_norm, w_out, norm_ffn, w_gate, w_up, w_down):
    B, S, D = x.shape
    M = mem.shape[1]
    T = B * S
    half = D_ROPE // 2
    H = N_HEADS

    sizes = (Q_LORA, KV_LORA, D_ROPE, WIDTH, WIDTH, WIDTH, WIDTH, WIDTH)
    offs = [0]
    for sz in sizes:
        offs.append(offs[-1] + sz)
    w_cq, w_ckv, w_kr, w_hq, w_hf, w_hi, w_hg, w_mq = [
        w_in[:, offs[j]:offs[j + 1]] for j in range(len(sizes))]
    zeros = lambda r, c: jnp.zeros((r, c), w_in.dtype)
    w_kr2 = jnp.concatenate([w_kr, zeros(D, LANE - D_ROPE),
                             -w_kr[:, half:], w_kr[:, :half], zeros(D, LANE - D_ROPE)], axis=1)
    uq = w_uq.reshape(Q_LORA, H, D_QK)
    uq_a = jnp.concatenate([uq, jnp.zeros((Q_LORA, H, D_QK_PAD - D_QK), w_uq.dtype)], axis=2)
    uq_b = jnp.concatenate([-uq[:, :, D_NOPE + half:], uq[:, :, D_NOPE:D_NOPE + half],
                            jnp.zeros((Q_LORA, H, LANE - D_ROPE), w_uq.dtype)], axis=2)
    ukv = w_ukv.reshape(KV_LORA, H, D_NOPE + D_V)
    w_uk = ukv[:, :, :D_NOPE].reshape(KV_LORA, H * D_NOPE)
    w_uv = ukv[:, :, D_NOPE:].reshape(KV_LORA, H * D_V)
    bf = lambda w: w.astype(BF16)

    pad = jnp.zeros((LANE - D_ROPE,), F32)
    gq = mla_q_norm.astype(F32)
    gk = mla_k_norm.astype(F32)
    gq_a = jnp.concatenate([gq, pad]).reshape(1, D_QK_PAD)
    gq_b = jnp.concatenate([gq[D_NOPE + half:], gq[D_NOPE:D_NOPE + half], pad]).reshape(1, LANE)
    gk_nope = gk[:D_NOPE].reshape(1, D_NOPE)
    gk_ra = jnp.concatenate([gk[D_NOPE:], pad]).reshape(1, LANE)
    gk_rb = jnp.concatenate([gk[D_NOPE + half:], gk[D_NOPE:D_NOPE + half], pad]).reshape(1, LANE)
    inv_freq = jnp.power(ROPE_THETA, -jnp.arange(half, dtype=F32) / half)
    invf = jnp.concatenate([inv_freq, inv_freq, pad]).reshape(1, LANE)

    kmem, vmem = pl.pallas_call(
        _mem_kv_kernel,
        grid=(B,),
        in_specs=[pl.BlockSpec((1, M, D), lambda b: (b, 0, 0)),
                  _full((1, D)), _full((D, 2 * WIDTH)), _full((1, MEM_D))],
        out_specs=[pl.BlockSpec((1, M, WIDTH), lambda b: (b, 0, 0))] * 2,
        out_shape=[jax.ShapeDtypeStruct((B, M, WIDTH), BF16)] * 2,
        compiler_params=_params(("arbitrary",)),
        name="mem_kv",
    )(mem, _row(norm_mem), bf(w_mem_kv), _row(mem_k_norm))

    tm = min(512, S)
    assert S % tm == 0
    steps_per_batch = S // tm
    x2 = x.reshape(T, D)
    pos2 = positions.reshape(T, 1).astype(jnp.int32)
    tok = lambda w: pl.BlockSpec((tm, w), lambda i: (i, 0))
    weights = [bf(w_cq), bf(w_ckv), bf(w_kr2), bf(w_hq), bf(w_hf), bf(w_hi), bf(w_hg), bf(w_mq)]
    rest = [_row(q_a_norm), bf(uq_a.reshape(Q_LORA, H * D_QK_PAD)), bf(uq_b.reshape(Q_LORA, H * LANE)),
            _row(kv_a_norm), bf(w_uk), bf(w_uv), gq_a, gq_b, gk_nope, gk_ra, gk_rb, _row(mem_q_norm)]
    mem_spec = pl.BlockSpec((1, M, WIDTH), lambda i: (i // steps_per_batch, 0, 0))
    vt_spec = pl.BlockSpec((1, H, 1, D_V, tm),
                           lambda i: (i // steps_per_batch, 0, i % steps_per_batch, 0, 0))
    q_all, k_all, vt_all, hq, hf, hi, hg, ymem = pl.pallas_call(
        _in_proj_kernel,
        grid=(T // tm,),
        in_specs=([tok(D), tok(1), _full((1, LANE)), _full((1, D))]
                  + [_full(w.shape) for w in weights] + [_full(r.shape) for r in rest]
                  + [mem_spec, mem_spec, _full((1, WIDTH))]),
        out_specs=[tok(H * D_QK_PAD), tok(H * D_QK_PAD), vt_spec] + [tok(WIDTH)] * 5,
        out_shape=[jax.ShapeDtypeStruct((T, H * D_QK_PAD), BF16)] * 2
        + [jax.ShapeDtypeStruct((B, H, steps_per_batch, D_V, tm), BF16)]
        + [jax.ShapeDtypeStruct((T, WIDTH), dt) for dt in (BF16, F32, BF16, BF16, BF16)],
        compiler_params=_params(("arbitrary",)),
        name="in_proj",
    )(x2, pos2, invf, _row(norm_mix), *weights, *rest, kmem, vmem, _row(mem_out_norm))

    blk = tm
    hpb = 2
    y_mla = pl.pallas_call(
        functools.partial(_attn_kernel, blk=blk, heads=hpb),
        grid=(B, H // hpb, S // blk),
        in_specs=[pl.BlockSpec((1, blk, hpb * D_QK_PAD), lambda b, h, i: (b, i, h)),
                  pl.BlockSpec((1, S, hpb * D_QK_PAD), lambda b, h, i: (b, 0, h)),
                  pl.BlockSpec((1, hpb, S // blk, D_V, blk), lambda b, h, i: (b, h, 0, 0, 0))],
        out_specs=pl.BlockSpec((1, blk, hpb * D_V), lambda b, h, i: (b, i, h)),
        out_shape=jax.ShapeDtypeStruct((B, S, H * D_V), BF16),
        scratch_shapes=[pltpu.VMEM((hpb, 1, blk), F32), pltpu.VMEM((hpb, 1, blk), F32),
                        pltpu.VMEM((hpb, D_V, blk), F32),
                        pltpu.VMEM((hpb, blk, blk), F32), pltpu.VMEM((hpb, blk, blk), F32),
                        pltpu.VMEM((hpb, 1, blk), F32), pltpu.VMEM((hpb, 1, blk), F32)],
        compiler_params=_params(("arbitrary", "arbitrary", "arbitrary")),
        name="mla_attn",
    )(q_all.reshape(B, S, H * D_QK_PAD), k_all.reshape(B, S, H * D_QK_PAD), vt_all)

    ts = min(512, S)
    assert S % ts == 0 and ts % HG_CHUNK == 0
    seq = lambda: pl.BlockSpec((1, ts, WIDTH), lambda b, i: (b, i, 0))
    n_layers = hg_lb_logits.shape[0]
    y_hg = pl.pallas_call(
        functools.partial(_hgrn_kernel, n_chunks=ts // HG_CHUNK, layer=layer),
        grid=(B, S // ts),
        in_specs=[seq(), seq(), seq(), seq(), _full((n_layers, WIDTH)), _full((1, WIDTH))],
        out_specs=seq(),
        out_shape=jax.ShapeDtypeStruct((B, S, WIDTH), BF16),
        scratch_shapes=[pltpu.VMEM((N_HEADS, HG_D, HG_D), F32)],
        compiler_params=_params(("arbitrary", "arbitrary")),
        name="hgrn",
    )(hq.reshape(B, S, WIDTH), hf.reshape(B, S, WIDTH), hi.reshape(B, S, WIDTH), hg.reshape(B, S, WIDTH),
      hg_lb_logits.astype(F32), _row(hg_out_norm))

    d_ff = w_gate.shape[1]
    once = lambda shape: pl.BlockSpec(shape, lambda i: (0, 0), pipeline_mode=pl.Buffered(1))
    out = pl.pallas_call(
        _out_ffn_kernel,
        grid=(T // tm,),
        in_specs=[tok(D), tok(WIDTH), tok(WIDTH), tok(WIDTH), _full((1, WIDTH)), once((3 * WIDTH, D)),
                  _full((1, D)), once((D, d_ff)), once((D, d_ff)), once((d_ff, D))],
        out_specs=tok(D),
        out_shape=jax.ShapeDtypeStruct((T, D), x.dtype),
        compiler_params=_params(("arbitrary",)),
        name="out_ffn",
    )(x2, y_mla.reshape(T, WIDTH), y_hg.reshape(T, WIDTH), ymem, _row(mla_out_norm), bf(w_out),
      _row(norm_ffn), bf(w_gate), bf(w_up), bf(w_down))
    return out.reshape(B, S, D)


def kernel(x, mem, positions, norm_mix, norm_mem, w_in, q_a_norm, w_uq, kv_a_norm, w_ukv, mla_q_norm, mla_k_norm, hg_lb_logits, hg_out_norm, w_mem_kv, mem_q_norm, mem_k_norm, mla_out_norm, mem_out_norm, w_out, norm_ffn, w_gate, w_up, w_down):
    depth = w_in.shape[0]
    for l in range(depth):
        x = _layer(x, mem, positions, l, norm_mix[l], norm_mem[l], w_in[l], q_a_norm[l], w_uq[l],
                   kv_a_norm[l], w_ukv[l], mla_q_norm[l], mla_k_norm[l], hg_lb_logits, hg_out_norm[l],
                   w_mem_kv[l], mem_q_norm[l], mem_k_norm[l], mla_out_norm[l], mem_out_norm[l],
                   w_out[l], norm_ffn[l], w_gate[l], w_up[l], w_down[l])
    return x
```

```python
import functools

import jax
import jax.numpy as jnp
from jax import lax
from jax.experimental import pallas as pl
from jax.experimental.pallas import tpu as pltpu

F32 = jnp.float32
BF16 = jnp.bfloat16

EPS = 1e-6
N_HEADS = 4
D_NOPE = 128
D_ROPE = 64
D_QK = D_NOPE + D_ROPE
D_QK_PAD = 256
D_V = 128
Q_LORA = 384
KV_LORA = 256
ROPE_THETA = 10000.0
LOG2E = 1.4426950408889634
HG_D = 128
MEM_D = 128
WIDTH = N_HEADS * 128

LANE = 128
HG_CHUNK = 128
HG_SUB = 8

VMEM_LIMIT = 56 * 1024 * 1024


def _dot(a, b):
    return jnp.dot(a, b, preferred_element_type=F32)


def _dot_nt(a, b):
    return lax.dot_general(a, b, (((1,), (1,)), ((), ())), preferred_element_type=F32)


def _rms(x, g, width):
    ss = jnp.sum(x * x, axis=-1, keepdims=True)
    return x * lax.rsqrt(ss * (1.0 / width) + EPS) * g


def _sigmoid(x):
    return 1.0 / (1.0 + jnp.exp(-x))


def _mem_kv_kernel(mem_ref, g_ref, w_ref, kn_ref, k_out, v_out):
    m = mem_ref[0].astype(F32)
    mh = _rms(m, g_ref[...], m.shape[-1]).astype(BF16)
    kv = _dot(mh, w_ref[...])
    for h in range(N_HEADS):
        kh = kv[:, h * MEM_D:(h + 1) * MEM_D]
        k_out[0, :, h * MEM_D:(h + 1) * MEM_D] = _rms(kh, kn_ref[...], MEM_D).astype(BF16)
    v_out[0] = kv[:, WIDTH:].astype(BF16)


def _in_proj_kernel(x_ref, pos_ref, invf_ref, phase_ref, gmix_ref,
                    w_cqkr_ref, w_ckv_ref, w_hq_ref, w_hf_ref, w_hi_ref, w_hg_ref, w_mq_ref,
                    gqa_ref, w_uq_ref, gkva_ref, w_uk_ref, w_uv_ref,
                    gq_nope_ref, gq_r1_ref, gq_r2_ref, gk_nope_ref, gk_r1_ref, gk_r2_ref,
                    gmq_ref, kmem_ref, vmem_ref, gmo_ref,
                    q_out, k_out, vt_out, hq_out, hf_out, hi_out, hg_out, ymem_out):
    x = x_ref[...].astype(F32)
    h = _rms(x, gmix_ref[...], x.shape[-1]).astype(BF16)

    big = _dot(h, w_cqkr_ref[...])
    ckv = _dot(h, w_ckv_ref[...])
    hq = _dot(h, w_hq_ref[...])
    cqn = _rms(big[:, :Q_LORA], gqa_ref[...], Q_LORA).astype(BF16)
    ckvn = _rms(ckv, gkva_ref[...], KV_LORA).astype(BF16)
    qa = _dot(cqn, w_uq_ref[...])
    kn = _dot(ckvn, w_uk_ref[...])
    vt = _dot(ckvn, w_uv_ref[...]).T
    mq = _dot(h, w_mq_ref[...])
    hf_out[...] = _dot(h, w_hf_ref[...])

    cs = jnp.cos(pos_ref[...].astype(F32) * invf_ref[...] + phase_ref[...])
    sc = pltpu.roll(cs, D_ROPE, axis=1)

    def rotary(tile, g1_ref, g2_ref):
        return tile * (g1_ref[...] * cs) + pltpu.roll(tile, D_ROPE, axis=1) * (g2_ref[...] * sc)

    q_scale = LOG2E * D_QK ** -0.5
    for hd in range(N_HEADS):
        nope = qa[:, hd * D_QK_PAD:hd * D_QK_PAD + D_NOPE]
        tile = qa[:, hd * D_QK_PAD + D_NOPE:(hd + 1) * D_QK_PAD]
        ss = jnp.sum(nope * nope, axis=-1, keepdims=True) + 0.5 * jnp.sum(tile * tile, axis=-1, keepdims=True)
        rinv = lax.rsqrt(ss * (1.0 / D_QK) + EPS) * q_scale
        q_out[:, hd * D_QK_PAD:hd * D_QK_PAD + D_NOPE] = (nope * gq_nope_ref[...] * rinv).astype(BF16)
        q_out[:, hd * D_QK_PAD + D_NOPE:(hd + 1) * D_QK_PAD] = (
            rotary(tile, gq_r1_ref, gq_r2_ref) * rinv).astype(BF16)

    vt_out[0, :, 0] = vt.reshape(N_HEADS, D_V, vt.shape[-1]).astype(BF16)
    ktile = big[:, Q_LORA:]
    ss_r = 0.5 * jnp.sum(ktile * ktile, axis=-1, keepdims=True)
    rot = rotary(ktile, gk_r1_ref, gk_r2_ref)
    for hd in range(N_HEADS):
        a = kn[:, hd * D_NOPE:(hd + 1) * D_NOPE]
        ss = jnp.sum(a * a, axis=-1, keepdims=True) + ss_r
        rinv = lax.rsqrt(ss * (1.0 / D_QK) + EPS)
        k_out[:, hd * D_QK_PAD:hd * D_QK_PAD + D_NOPE] = (a * gk_nope_ref[...] * rinv).astype(BF16)
        k_out[:, hd * D_QK_PAD + D_NOPE:(hd + 1) * D_QK_PAD] = (rot * rinv).astype(BF16)

    ss_ = []
    for hd in range(N_HEADS):
        sl = slice(hd * MEM_D, (hd + 1) * MEM_D)
        qh = (_rms(mq[:, sl], gmq_ref[...], MEM_D) * (MEM_D ** -0.5)).astype(BF16)
        ss_.append(_dot_nt(qh, kmem_ref[0, :, sl]))
    hi_out[...] = _dot(h, w_hi_ref[...]).astype(BF16)
    hq_out[...] = (hq * _sigmoid(hq) * (HG_D ** -0.5)).astype(BF16)
    ys = []
    for hd in range(N_HEADS):
        sl = slice(hd * MEM_D, (hd + 1) * MEM_D)
        s = ss_[hd]
        p = jnp.exp(s - jnp.max(s, axis=-1, keepdims=True))
        l = jnp.sum(p, axis=-1, keepdims=True)
        ys.append(_dot(p.astype(BF16), vmem_ref[0, :, sl]) / l)
    hg = _dot(h, w_hg_ref[...])
    hg_out[...] = (hg * _sigmoid(hg)).astype(BF16)
    y = jnp.concatenate(ys, axis=-1)
    ymem_out[...] = _rms(y, gmo_ref[...], WIDTH).astype(BF16)


def _attn_kernel(q_ref, k_ref, vt_ref, o_ref, m_sc, l_sc, acc_sc, s0_sc, s1_sc, cm0_sc, cm1_sc,
                 *, blk, heads):
    qi = pl.program_id(2)
    s_bufs = (s0_sc, s1_sc)
    cm_bufs = (cm0_sc, cm1_sc)
    m_sc[...] = jnp.full(m_sc.shape, -jnp.inf, F32)
    l_sc[...] = jnp.zeros(l_sc.shape, F32)
    acc_sc[...] = jnp.zeros(acc_sc.shape, F32)

    def scores(t, slot, hd):
        r0 = t * blk if isinstance(t, int) else pl.multiple_of(t * blk, blk)
        q = q_ref[0, :, hd * D_QK_PAD:(hd + 1) * D_QK_PAD]
        k = k_ref[0, pl.ds(r0, blk), hd * D_QK_PAD:(hd + 1) * D_QK_PAD]
        s = _dot_nt(k, q)
        s_bufs[slot][hd] = s
        cm_bufs[slot][hd] = jnp.max(s, axis=0, keepdims=True)

    def accumulate(t, slot, hd, masked):
        s = s_bufs[slot][hd]
        if masked:
            kv = lax.broadcasted_iota(jnp.int32, (blk, blk), 0)
            qq = lax.broadcasted_iota(jnp.int32, (blk, blk), 1)
            s = jnp.where(kv <= qq, s, -jnp.inf)
            cm = jnp.max(s, axis=0, keepdims=True)
        else:
            cm = cm_bufs[slot][hd]
        m_prev = m_sc[hd]
        m_new = jnp.maximum(m_prev, cm)
        p = jnp.exp2(s - m_new)
        alpha = jnp.exp2(m_prev - m_new)
        l_sc[hd] = alpha * l_sc[hd] + jnp.sum(p, axis=0, keepdims=True)
        acc_sc[hd] = alpha * acc_sc[hd] + _dot(vt_ref[0, hd, t], p.astype(BF16))
        m_sc[hd] = m_new

    for hd in range(heads):
        scores(0, 0, hd)

    def body(j, carry):
        for slot in range(2):
            @pl.when((j & 1) == slot)
            def _():
                for hd in range(heads):
                    scores(j + 1, 1 - slot, hd)
                    accumulate(j, slot, hd, False)
        return carry

    lax.fori_loop(0, qi, body, 0)
    for slot in range(2):
        @pl.when((qi & 1) == slot)
        def _():
            for hd in range(heads):
                accumulate(qi, slot, hd, True)
    for hd in range(heads):
        o = (acc_sc[hd] / l_sc[hd]).T
        o_ref[0, :, hd * D_V:(hd + 1) * D_V] = o.astype(o_ref.dtype)


def _pair_reference(b, m):
    c = b.shape[0]
    n2 = c // (2 * m)
    br = b.reshape(n2, 2 * m, b.shape[1])
    last = br[:, m - 1:m, :]
    return jnp.broadcast_to(last, br.shape).reshape(b.shape)


def _hgrn_kernel(hq_ref, hf_ref, hi_ref, hg_ref, lbl_ref, gain_ref, o_ref, st_ref, *, n_chunks, layer):
    C = HG_CHUNK

    @pl.when(pl.program_id(1) == 0)
    def _():
        st_ref[...] = jnp.zeros(st_ref.shape, F32)

    lg = lbl_ref[...].astype(F32)
    e = jnp.exp(lg - jnp.max(lg, axis=0, keepdims=True))
    lb = jnp.sum(e[:layer + 1], axis=0, keepdims=True) / jnp.sum(e, axis=0, keepdims=True)

    row = lax.broadcasted_iota(jnp.int32, (C, C), 0)
    col = lax.broadcasted_iota(jnp.int32, (C, C), 1)
    tri = (col <= row).astype(BF16)
    diag_mask = (((row ^ col) & ~(HG_SUB - 1)) | jnp.where(col <= row, 0, 1)) == 0
    sub_keep = [jnp.where((row & (HG_SUB - 1)) == s_off, 1.0, 0.0).astype(BF16) for s_off in range(HG_SUB)]
    levels = []
    m = HG_SUB
    while m < C:
        bad = ((row ^ col) & ~(2 * m - 1)) | ((row & m) ^ m) | (col & m)
        levels.append((m, bad == 0))
        m *= 2

    def chunk(c, carry):
        r0 = pl.multiple_of(c * C, C)
        fr = hf_ref[0, pl.ds(r0, C), :]
        f = lb + (1.0 - lb) * _sigmoid(fr)
        logf = jnp.log(f) * LOG2E
        kk_all = 1.0 - f
        t0 = logf.astype(BF16)
        r1 = logf - t0.astype(F32)
        t1 = r1.astype(BF16)
        t2 = (r1 - t1.astype(F32)).astype(BF16)
        b_all = _dot(tri, t0) + _dot(tri, t1) + _dot(tri, t2)
        q_all = hq_ref[0, pl.ds(r0, C), :].astype(F32)
        v_all = hi_ref[0, pl.ds(r0, C), :]
        g_all = hg_ref[0, pl.ds(r0, C), :].astype(F32)
        for hd in range(N_HEADS):
            sl = slice(hd * HG_D, (hd + 1) * HG_D)
            b = b_all[:, sl]
            q = q_all[:, sl]
            kk = kk_all[:, sl]
            v = v_all[:, sl]
            kb = kk.astype(BF16)

            b3 = b.reshape(C // HG_SUB, HG_SUB, HG_D)
            ms = []
            ks = []
            for s_off in range(HG_SUB):
                bs = jnp.broadcast_to(b3[:, s_off:s_off + 1, :], b3.shape).reshape(C, HG_D)
                ms.append((q * jnp.exp2(jnp.minimum(b - bs, 0.0))).astype(BF16))
                ks.append(kb * sub_keep[s_off])
            ad = _dot_nt(jnp.concatenate(ms, axis=1), jnp.concatenate(ks, axis=1))
            a = jnp.where(diag_mask, ad, 0.0)
            for m_blk, mask in levels:
                d = b - _pair_reference(b, m_blk)
                qe = (q * jnp.exp2(d)).astype(BF16)
                ke = (kk * jnp.exp2(-d)).astype(BF16)
                a = jnp.where(mask, _dot_nt(qe, ke), a)

            st = st_ref[hd]
            o = _dot(a.astype(BF16), v) + _dot_nt((q * jnp.exp2(b)).astype(BF16), st.astype(BF16))
            b_last = b[C - 1:C, :]
            kd = (kk * jnp.exp2(b_last - b)).astype(BF16)
            vt = v.astype(F32).T.astype(BF16)
            st_ref[hd] = st * jnp.exp2(b_last) + _dot(vt, kd)

            on = _rms(o, gain_ref[:, sl], HG_D)
            o_ref[0, pl.ds(r0, C), sl] = (on * g_all[:, sl]).astype(o_ref.dtype)
        return carry

    lax.fori_loop(0, n_chunks, chunk, 0)


def _out_ffn_kernel(x_ref, ymla_ref, yhg_ref, ymem_ref, gmla_ref, w_out_ref, gffn_ref,
                    w_gate_ref, w_up_ref, w_down_ref, o_ref):
    x = x_ref[...].astype(F32)
    ymla = _rms(ymla_ref[...].astype(F32), gmla_ref[...], WIDTH).astype(BF16)
    mix = (_dot(ymla, w_out_ref[0:WIDTH, :])
           + _dot(yhg_ref[...], w_out_ref[WIDTH:2 * WIDTH, :])
           + _dot(ymem_ref[...], w_out_ref[2 * WIDTH:3 * WIDTH, :]))
    x1 = x + mix
    h2 = _rms(x1, gffn_ref[...], x1.shape[-1]).astype(BF16)
    g = _dot(h2, w_gate_ref[...])
    u = _dot(h2, w_up_ref[...])
    act = (g * _sigmoid(g) * u).astype(BF16)
    o_ref[...] = (x1 + _dot(act, w_down_ref[...])).astype(o_ref.dtype)


def _full(shape):
    nd = len(shape)
    return pl.BlockSpec(shape, lambda *_: (0,) * nd)


def _params(sem):
    return pltpu.CompilerParams(dimension_semantics=sem, vmem_limit_bytes=VMEM_LIMIT)


def _row(v):
    return v.reshape(1, -1).astype(F32)


def _layer(x, mem, positions, layer, norm_mix, norm_mem, w_in, q_a_norm, w_uq, kv_a_norm, w_ukv,
           mla_q_norm, mla_k_norm, hg_lb_logits, hg_out_norm, w_mem_kv, mem_q_norm, mem_k_norm,
           mla_out_norm, mem_out_norm, w_out, norm_ffn, w_gate, w_up, w_down):
    B, S, D = x.shape
    M = mem.shape[1]
    T = B * S
    half = D_ROPE // 2
    H = N_HEADS

    sizes = (Q_LORA, KV_LORA, D_ROPE, WIDTH, WIDTH, WIDTH, WIDTH, WIDTH)
    offs = [0]
    for sz in sizes:
        offs.append(offs[-1] + sz)
    w_cq, w_ckv, w_kr, w_hq, w_hf, w_hi, w_hg, w_mq = [
        w_in[:, offs[j]:offs[j + 1]] for j in range(len(sizes))]
    w_cqkr = jnp.concatenate([w_cq, w_kr, -w_kr[:, half:], w_kr[:, :half]], axis=1)
    uq = w_uq.reshape(Q_LORA, H, D_QK)
    uq2 = jnp.concatenate([uq, -uq[:, :, D_NOPE + half:], uq[:, :, D_NOPE:D_NOPE + half]], axis=2)
    ukv = w_ukv.reshape(KV_LORA, H, D_NOPE + D_V)
    w_uk = ukv[:, :, :D_NOPE].reshape(KV_LORA, H * D_NOPE)
    w_uv = ukv[:, :, D_NOPE:].reshape(KV_LORA, H * D_V)
    bf = lambda w: w.astype(BF16)

    pad = jnp.zeros((LANE - D_ROPE,), F32)

    def rotary_gains(g):
        g = g.astype(F32)
        g1 = jnp.concatenate([g[D_NOPE:], pad]).reshape(1, LANE)
        g2 = jnp.concatenate([g[D_NOPE + half:], g[D_NOPE:D_NOPE + half], pad]).reshape(1, LANE)
        return g[:D_NOPE].reshape(1, D_NOPE), g1, g2

    gq_nope, gq_r1, gq_r2 = rotary_gains(mla_q_norm)
    gk_nope, gk_r1, gk_r2 = rotary_gains(mla_k_norm)
    inv_freq = jnp.power(ROPE_THETA, -jnp.arange(half, dtype=F32) / half)
    invf = jnp.tile(inv_freq, LANE // half).reshape(1, LANE)
    phase = jnp.concatenate([jnp.zeros((D_ROPE,), F32), jnp.full((D_ROPE,), -jnp.pi / 2, F32)]).reshape(1, LANE)

    kmem, vmem = pl.pallas_call(
        _mem_kv_kernel,
        grid=(B,),
        in_specs=[pl.BlockSpec((1, M, D), lambda b: (b, 0, 0)),
                  _full((1, D)), _full((D, 2 * WIDTH)), _full((1, MEM_D))],
        out_specs=[pl.BlockSpec((1, M, WIDTH), lambda b: (b, 0, 0))] * 2,
        out_shape=[jax.ShapeDtypeStruct((B, M, WIDTH), BF16)] * 2,
        compiler_params=_params(("arbitrary",)),
        name="mem_kv",
    )(mem, _row(norm_mem), bf(w_mem_kv), _row(mem_k_norm))

    tm = min(512, S)
    assert S % tm == 0
    steps_per_batch = S // tm
    x2 = x.reshape(T, D)
    pos2 = positions.reshape(T, 1).astype(jnp.int32)
    tok = lambda w: pl.BlockSpec((tm, w), lambda i: (i, 0))
    weights = [bf(w_cqkr), bf(w_ckv), bf(w_hq), bf(w_hf), bf(w_hi), bf(w_hg), bf(w_mq)]
    rest = [_row(q_a_norm), bf(uq2.reshape(Q_LORA, H * D_QK_PAD)), _row(kv_a_norm), bf(w_uk), bf(w_uv),
            gq_nope, gq_r1, gq_r2, gk_nope, gk_r1, gk_r2, _row(mem_q_norm)]
    mem_spec = pl.BlockSpec((1, M, WIDTH), lambda i: (i // steps_per_batch, 0, 0))
    vt_spec = pl.BlockSpec((1, H, 1, D_V, tm),
                           lambda i: (i // steps_per_batch, 0, i % steps_per_batch, 0, 0))
    q_all, k_all, vt_all, hq, hf, hi, hg, ymem = pl.pallas_call(
        _in_proj_kernel,
        grid=(T // tm,),
        in_specs=([tok(D), tok(1), _full((1, LANE)), _full((1, LANE)), _full((1, D))]
                  + [_full(w.shape) for w in weights] + [_full(r.shape) for r in rest]
                  + [mem_spec, mem_spec, _full((1, WIDTH))]),
        out_specs=[tok(H * D_QK_PAD), tok(H * D_QK_PAD), vt_spec] + [tok(WIDTH)] * 5,
        out_shape=[jax.ShapeDtypeStruct((T, H * D_QK_PAD), BF16)] * 2
        + [jax.ShapeDtypeStruct((B, H, steps_per_batch, D_V, tm), BF16)]
        + [jax.ShapeDtypeStruct((T, WIDTH), dt) for dt in (BF16, F32, BF16, BF16, BF16)],
        compiler_params=_params(("arbitrary",)),
        name="in_proj",
    )(x2, pos2, invf, phase, _row(norm_mix), *weights, *rest, kmem, vmem, _row(mem_out_norm))

    blk = tm
    hpb = 2
    y_mla = pl.pallas_call(
        functools.partial(_attn_kernel, blk=blk, heads=hpb),
        grid=(B, H // hpb, S // blk),
        in_specs=[pl.BlockSpec((1, blk, hpb * D_QK_PAD), lambda b, h, i: (b, i, h)),
                  pl.BlockSpec((1, S, hpb * D_QK_PAD), lambda b, h, i: (b, 0, h)),
                  pl.BlockSpec((1, hpb, S // blk, D_V, blk), lambda b, h, i: (b, h, 0, 0, 0))],
        out_specs=pl.BlockSpec((1, blk, hpb * D_V), lambda b, h, i: (b, i, h)),
        out_shape=jax.ShapeDtypeStruct((B, S, H * D_V), BF16),
        scratch_shapes=[pltpu.VMEM((hpb, 1, blk), F32), pltpu.VMEM((hpb, 1, blk), F32),
                        pltpu.VMEM((hpb, D_V, blk), F32),
                        pltpu.VMEM((hpb, blk, blk), F32), pltpu.VMEM((hpb, blk, blk), F32),
                        pltpu.VMEM((hpb, 1, blk), F32), pltpu.VMEM((hpb, 1, blk), F32)],
        compiler_params=_params(("arbitrary", "arbitrary", "arbitrary")),
        name="mla_attn",
    )(q_all.reshape(B, S, H * D_QK_PAD), k_all.reshape(B, S, H * D_QK_PAD), vt_all)

    ts = min(512, S)
    assert S % ts == 0 and ts % HG_CHUNK == 0
    seq = lambda: pl.BlockSpec((1, ts, WIDTH), lambda b, i: (b, i, 0))
    n_layers = hg_lb_logits.shape[0]
    y_hg = pl.pallas_call(
        functools.partial(_hgrn_kernel, n_chunks=ts // HG_CHUNK, layer=layer),
        grid=(B, S // ts),
        in_specs=[seq(), seq(), seq(), seq(), _full((n_layers, WIDTH)), _full((1, WIDTH))],
        out_specs=seq(),
        out_shape=jax.ShapeDtypeStruct((B, S, WIDTH), BF16),
        scratch_shapes=[pltpu.VMEM((N_HEADS, HG_D, HG_D), F32)],
        compiler_params=_params(("arbitrary", "arbitrary")),
        name="hgrn",
    )(hq.reshape(B, S, WIDTH), hf.reshape(B, S, WIDTH), hi.reshape(B, S, WIDTH), hg.reshape(B, S, WIDTH),
      hg_lb_logits.astype(F32), _row(hg_out_norm))

    d_ff = w_gate.shape[1]
    once = lambda shape: pl.BlockSpec(shape, lambda i: (0, 0), pipeline_mode=pl.Buffered(1))
    out = pl.pallas_call(
        _out_ffn_kernel,
        grid=(T // tm,),
        in_specs=[tok(D), tok(WIDTH), tok(WIDTH), tok(WIDTH), _full((1, WIDTH)), once((3 * WIDTH, D)),
                  _full((1, D)), once((D, d_ff)), once((D, d_ff)), once((d_ff, D))],
        out_specs=tok(D),
        out_shape=jax.ShapeDtypeStruct((T, D), x.dtype),
        compiler_params=_params(("arbitrary",)),
        name="out_ffn",
    )(x2, y_mla.reshape(T, WIDTH), y_hg.reshape(T, WIDTH), ymem, _row(mla_out_norm), bf(w_out),
      _row(norm_ffn), bf(w_gate), bf(w_up), bf(w_down))
    return out.reshape(B, S, D)


def kernel(x, mem, positions, norm_mix, norm_mem, w_in, q_a_norm, w_uq, kv_a_norm, w_ukv, mla_q_norm, mla_k_norm, hg_lb_logits, hg_out_norm, w_mem_kv, mem_q_norm, mem_k_norm, mla_out_norm, mem_out_norm, w_out, norm_ffn, w_gate, w_up, w_down):
    depth = w_in.shape[0]
    for l in range(depth):
        x = _layer(x, mem, positions, l, norm_mix[l], norm_mem[l], w_in[l], q_a_norm[l], w_uq[l],
                   kv_a_norm[l], w_ukv[l], mla_q_norm[l], mla_k_norm[l], hg_lb_logits, hg_out_norm[l],
                   w_mem_kv[l], mem_q_norm[l], mem_k_norm[l], mla_out_norm[l], mem_out_norm[l],
                   w_out[l], norm_ffn[l], w_gate[l], w_up[l], w_down[l])
    return x
```

```python
import functools

import jax
import jax.numpy as jnp
from jax import lax
from jax.experimental import pallas as pl
from jax.experimental.pallas import tpu as pltpu

F32 = jnp.float32
BF16 = jnp.bfloat16

EPS = 1e-6
N_HEADS = 4
D_NOPE = 128
D_ROPE = 64
D_QK = D_NOPE + D_ROPE
D_QK_PAD = 256
D_V = 128
Q_LORA = 384
KV_LORA = 256
ROPE_THETA = 10000.0
LOG2E = 1.4426950408889634
HG_D = 128
MEM_D = 128
WIDTH = N_HEADS * 128

LANE = 128
HG_CHUNK = 128
HG_SUB = 8

VMEM_LIMIT = 56 * 1024 * 1024


def _dot(a, b):
    return jnp.dot(a, b, preferred_element_type=F32)


def _dot_nt(a, b):
    return lax.dot_general(a, b, (((1,), (1,)), ((), ())), preferred_element_type=F32)


def _rms(x, g, width):
    ss = jnp.sum(x * x, axis=-1, keepdims=True)
    return x * lax.rsqrt(ss * (1.0 / width) + EPS) * g


def _sigmoid(x):
    return 1.0 / (1.0 + jnp.exp(-x))


def _mem_kv_kernel(mem_ref, g_ref, w_ref, kn_ref, k_out, v_out):
    m = mem_ref[0].astype(F32)
    mh = _rms(m, g_ref[...], m.shape[-1]).astype(BF16)
    kv = _dot(mh, w_ref[...])
    for h in range(N_HEADS):
        kh = kv[:, h * MEM_D:(h + 1) * MEM_D]
        k_out[0, :, h * MEM_D:(h + 1) * MEM_D] = _rms(kh, kn_ref[...], MEM_D).astype(BF16)
    v_out[0] = kv[:, WIDTH:].astype(BF16)


def _in_proj_kernel(x_ref, pos_ref, invf_ref, phase_ref, gmix_ref,
                    w_cqkr_ref, w_ckv_ref, w_hq_ref, w_hf_ref, w_hi_ref, w_hg_ref, w_mq_ref,
                    gqa_ref, w_uq_ref, gkva_ref, w_uk_ref, w_uv_ref,
                    gq_nope_ref, gq_r1_ref, gq_r2_ref, gk_nope_ref, gk_r1_ref, gk_r2_ref,
                    gmq_ref, kmem_ref, vmem_ref, gmo_ref,
                    q_out, k_out, vt_out, hq_out, hf_out, hi_out, hg_out, ymem_out):
    x = x_ref[...].astype(F32)
    h = _rms(x, gmix_ref[...], x.shape[-1]).astype(BF16)

    big = _dot(h, w_cqkr_ref[...])
    ckv = _dot(h, w_ckv_ref[...])
    hq = _dot(h, w_hq_ref[...])
    cqn = _rms(big[:, :Q_LORA], gqa_ref[...], Q_LORA).astype(BF16)
    ckvn = _rms(ckv, gkva_ref[...], KV_LORA).astype(BF16)
    qa = _dot(cqn, w_uq_ref[...])
    kn = _dot(ckvn, w_uk_ref[...])
    vt = _dot(ckvn, w_uv_ref[...]).T
    mq = _dot(h, w_mq_ref[...])
    hf_out[...] = _dot(h, w_hf_ref[...])

    posf = pos_ref[0].astype(F32)
    pos = jnp.concatenate([jnp.broadcast_to(posf[r:r + 1, :], (LANE, LANE)).T
                           for r in range(posf.shape[0])], axis=0)
    cs = jnp.cos(pos * invf_ref[...] + phase_ref[...])
    sc = pltpu.roll(cs, D_ROPE, axis=1)

    def rotary(tile, g1_ref, g2_ref):
        return tile * (g1_ref[...] * cs) + pltpu.roll(tile, D_ROPE, axis=1) * (g2_ref[...] * sc)

    q_scale = LOG2E * D_QK ** -0.5
    for hd in range(N_HEADS):
        nope = qa[:, hd * D_QK_PAD:hd * D_QK_PAD + D_NOPE]
        tile = qa[:, hd * D_QK_PAD + D_NOPE:(hd + 1) * D_QK_PAD]
        ss = jnp.sum(nope * nope, axis=-1, keepdims=True) + 0.5 * jnp.sum(tile * tile, axis=-1, keepdims=True)
        rinv = lax.rsqrt(ss * (1.0 / D_QK) + EPS) * q_scale
        q_out[:, hd * D_QK_PAD:hd * D_QK_PAD + D_NOPE] = (nope * gq_nope_ref[...] * rinv).astype(BF16)
        q_out[:, hd * D_QK_PAD + D_NOPE:(hd + 1) * D_QK_PAD] = (
            rotary(tile, gq_r1_ref, gq_r2_ref) * rinv).astype(BF16)

    vt_out[0, :, 0] = vt.reshape(N_HEADS, D_V, vt.shape[-1]).astype(BF16)
    ktile = big[:, Q_LORA:]
    ss_r = 0.5 * jnp.sum(ktile * ktile, axis=-1, keepdims=True)
    rot = rotary(ktile, gk_r1_ref, gk_r2_ref)
    for hd in range(N_HEADS):
        a = kn[:, hd * D_NOPE:(hd + 1) * D_NOPE]
        ss = jnp.sum(a * a, axis=-1, keepdims=True) + ss_r
        rinv = lax.rsqrt(ss * (1.0 / D_QK) + EPS)
        k_out[:, hd * D_QK_PAD:hd * D_QK_PAD + D_NOPE] = (a * gk_nope_ref[...] * rinv).astype(BF16)
        k_out[:, hd * D_QK_PAD + D_NOPE:(hd + 1) * D_QK_PAD] = (rot * rinv).astype(BF16)

    ss_ = []
    for hd in range(N_HEADS):
        sl = slice(hd * MEM_D, (hd + 1) * MEM_D)
        qh = (_rms(mq[:, sl], gmq_ref[...], MEM_D) * (MEM_D ** -0.5)).astype(BF16)
        ss_.append(_dot_nt(qh, kmem_ref[0, :, sl]))
    hi_out[...] = _dot(h, w_hi_ref[...]).astype(BF16)
    hq_out[...] = (hq * _sigmoid(hq) * (HG_D ** -0.5)).astype(BF16)
    ys = []
    for hd in range(N_HEADS):
        sl = slice(hd * MEM_D, (hd + 1) * MEM_D)
        s = ss_[hd]
        p = jnp.exp(s - jnp.max(s, axis=-1, keepdims=True))
        l = jnp.sum(p, axis=-1, keepdims=True)
        ys.append(_dot(p.astype(BF16), vmem_ref[0, :, sl]) / l)
    hg = _dot(h, w_hg_ref[...])
    hg_out[...] = (hg * _sigmoid(hg)).astype(BF16)
    y = jnp.concatenate(ys, axis=-1)
    ymem_out[...] = _rms(y, gmo_ref[...], WIDTH).astype(BF16)


def _attn_kernel(q_ref, k_ref, vt_ref, o_ref, m_sc, l_sc, acc_sc, s0_sc, s1_sc, cm0_sc, cm1_sc,
                 *, blk, heads):
    qi = pl.program_id(2)
    s_bufs = (s0_sc, s1_sc)
    cm_bufs = (cm0_sc, cm1_sc)
    m_sc[...] = jnp.full(m_sc.shape, -jnp.inf, F32)
    l_sc[...] = jnp.zeros(l_sc.shape, F32)
    acc_sc[...] = jnp.zeros(acc_sc.shape, F32)

    def scores(t, slot, hd):
        r0 = t * blk if isinstance(t, int) else pl.multiple_of(t * blk, blk)
        q = q_ref[0, :, hd * D_QK_PAD:(hd + 1) * D_QK_PAD]
        k = k_ref[0, pl.ds(r0, blk), hd * D_QK_PAD:(hd + 1) * D_QK_PAD]
        s = _dot_nt(k, q)
        s_bufs[slot][hd] = s
        cm_bufs[slot][hd] = jnp.max(s, axis=0, keepdims=True)

    def accumulate(t, slot, hd, masked):
        s = s_bufs[slot][hd]
        if masked:
            kv = lax.broadcasted_iota(jnp.int32, (blk, blk), 0)
            qq = lax.broadcasted_iota(jnp.int32, (blk, blk), 1)
            s = jnp.where(kv <= qq, s, -jnp.inf)
            cm = jnp.max(s, axis=0, keepdims=True)
        else:
            cm = cm_bufs[slot][hd]
        m_prev = m_sc[hd]
        m_new = jnp.maximum(m_prev, cm)
        p = jnp.exp2(s - m_new)
        alpha = jnp.exp2(m_prev - m_new)
        l_sc[hd] = alpha * l_sc[hd] + jnp.sum(p, axis=0, keepdims=True)
        acc_sc[hd] = alpha * acc_sc[hd] + _dot(vt_ref[0, hd, t], p.astype(BF16))
        m_sc[hd] = m_new

    for hd in range(heads):
        scores(0, 0, hd)

    def advance(t, slot):
        for hd in range(heads):
            scores(t + 1, 1 - slot, hd)
            accumulate(t, slot, hd, False)

    def body(jj, carry):
        advance(2 * jj, 0)
        advance(2 * jj + 1, 1)
        return carry

    lax.fori_loop(0, jnp.right_shift(qi, 1), body, 0)

    @pl.when((qi & 1) == 1)
    def _():
        advance(qi - 1, 0)
        for hd in range(heads):
            accumulate(qi, 1, hd, True)

    @pl.when((qi & 1) == 0)
    def _():
        for hd in range(heads):
            accumulate(qi, 0, hd, True)

    for hd in range(heads):
        o = (acc_sc[hd] / l_sc[hd]).T
        o_ref[0, :, hd * D_V:(hd + 1) * D_V] = o.astype(o_ref.dtype)


def _pair_reference(b, m):
    c = b.shape[0]
    n2 = c // (2 * m)
    br = b.reshape(n2, 2 * m, b.shape[1])
    last = br[:, m - 1:m, :]
    return jnp.broadcast_to(last, br.shape).reshape(b.shape)


def _hgrn_kernel(hq_ref, hf_ref, hi_ref, hg_ref, lbl_ref, gain_ref, o_ref, st_ref, *, n_chunks, layer):
    C = HG_CHUNK

    @pl.when(pl.program_id(1) == 0)
    def _():
        st_ref[...] = jnp.zeros(st_ref.shape, F32)

    lg = lbl_ref[...].astype(F32)
    e = jnp.exp(lg - jnp.max(lg, axis=0, keepdims=True))
    lb = jnp.sum(e[:layer + 1], axis=0, keepdims=True) / jnp.sum(e, axis=0, keepdims=True)

    row = lax.broadcasted_iota(jnp.int32, (C, C), 0)
    col = lax.broadcasted_iota(jnp.int32, (C, C), 1)
    tri = (col <= row).astype(BF16)
    diag_mask = (((row ^ col) & ~(HG_SUB - 1)) | jnp.where(col <= row, 0, 1)) == 0
    sub_keep = [jnp.where((row & (HG_SUB - 1)) == s_off, 1.0, 0.0).astype(BF16) for s_off in range(HG_SUB)]
    levels = []
    m = HG_SUB
    while m < C:
        bad = ((row ^ col) & ~(2 * m - 1)) | ((row & m) ^ m) | (col & m)
        levels.append((m, bad == 0))
        m *= 2

    def chunk(c, carry):
        r0 = pl.multiple_of(c * C, C)
        fr = hf_ref[0, pl.ds(r0, C), :]
        f = lb + (1.0 - lb) * _sigmoid(fr)
        logf = jnp.log(f) * LOG2E
        kk_all = 1.0 - f
        t0 = logf.astype(BF16)
        r1 = logf - t0.astype(F32)
        t1 = r1.astype(BF16)
        t2 = (r1 - t1.astype(F32)).astype(BF16)
        b_all = _dot(tri, t0) + _dot(tri, t1) + _dot(tri, t2)
        q_all = hq_ref[0, pl.ds(r0, C), :].astype(F32)
        v_all = hi_ref[0, pl.ds(r0, C), :]
        g_all = hg_ref[0, pl.ds(r0, C), :].astype(F32)
        for hd in range(N_HEADS):
            sl = slice(hd * HG_D, (hd + 1) * HG_D)
            b = b_all[:, sl]
            q = q_all[:, sl]
            kk = kk_all[:, sl]
            v = v_all[:, sl]
            kb = kk.astype(BF16)

            b3 = b.reshape(C // HG_SUB, HG_SUB, HG_D)
            ms = []
            ks = []
            for s_off in range(HG_SUB):
                bs = jnp.broadcast_to(b3[:, s_off:s_off + 1, :], b3.shape).reshape(C, HG_D)
                ms.append((q * jnp.exp2(jnp.minimum(b - bs, 0.0))).astype(BF16))
                ks.append(kb * sub_keep[s_off])
            ad = _dot_nt(jnp.concatenate(ms, axis=1), jnp.concatenate(ks, axis=1))
            a = jnp.where(diag_mask, ad, 0.0)
            for m_blk, mask in levels:
                d = b - _pair_reference(b, m_blk)
                qe = (q * jnp.exp2(d)).astype(BF16)
                ke = (kk * jnp.exp2(-d)).astype(BF16)
                a = jnp.where(mask, _dot_nt(qe, ke), a)

            st = st_ref[hd]
            o = _dot(a.astype(BF16), v) + _dot_nt((q * jnp.exp2(b)).astype(BF16), st.astype(BF16))
            b_last = b[C - 1:C, :]
            kd = (kk * jnp.exp2(b_last - b)).astype(BF16)
            vt = v.astype(F32).T.astype(BF16)
            st_ref[hd] = st * jnp.exp2(b_last) + _dot(vt, kd)

            on = _rms(o, gain_ref[:, sl], HG_D)
            o_ref[0, pl.ds(r0, C), sl] = (on * g_all[:, sl]).astype(o_ref.dtype)
        return carry

    lax.fori_loop(0, n_chunks, chunk, 0)


def _out_ffn_kernel(x_ref, ymla_ref, yhg_ref, ymem_ref, gmla_ref, w_out_ref, gffn_ref,
                    w_gate_ref, w_up_ref, w_down_ref, o_ref):
    x = x_ref[...].astype(F32)
    ymla = _rms(ymla_ref[...].astype(F32), gmla_ref[...], WIDTH).astype(BF16)
    mix = (_dot(ymla, w_out_ref[0:WIDTH, :])
           + _dot(yhg_ref[...], w_out_ref[WIDTH:2 * WIDTH, :])
           + _dot(ymem_ref[...], w_out_ref[2 * WIDTH:3 * WIDTH, :]))
    x1 = x + mix
    h2 = _rms(x1, gffn_ref[...], x1.shape[-1]).astype(BF16)
    g = _dot(h2, w_gate_ref[...])
    u = _dot(h2, w_up_ref[...])
    act = (g * _sigmoid(g) * u).astype(BF16)
    o_ref[...] = (x1 + _dot(act, w_down_ref[...])).astype(o_ref.dtype)


def _full(shape):
    nd = len(shape)
    return pl.BlockSpec(shape, lambda *_: (0,) * nd)


def _params(sem):
    return pltpu.CompilerParams(dimension_semantics=sem, vmem_limit_bytes=VMEM_LIMIT)


def _row(v):
    return v.reshape(1, -1).astype(F32)


def _layer(x, mem, positions, layer, norm_mix, norm_mem, w_in, q_a_norm, w_uq, kv_a_norm, w_ukv,
           mla_q_norm, mla_k_norm, hg_lb_logits, hg_out_norm, w_mem_kv, mem_q_norm, mem_k_norm,
           mla_out_norm, mem_out_norm, w_out, norm_ffn, w_gate, w_up, w_down):
    B, S, D = x.shape
    M = mem.shape[1]
    T = B * S
    half = D_ROPE // 2
    H = N_HEADS

    sizes = (Q_LORA, KV_LORA, D_ROPE, WIDTH, WIDTH, WIDTH, WIDTH, WIDTH)
    offs = [0]
    for sz in sizes:
        offs.append(offs[-1] + sz)
    w_cq, w_ckv, w_kr, w_hq, w_hf, w_hi, w_hg, w_mq = [
        w_in[:, offs[j]:offs[j + 1]] for j in range(len(sizes))]
    w_cqkr = jnp.concatenate([w_cq, w_kr, -w_kr[:, half:], w_kr[:, :half]], axis=1)
    uq = w_uq.reshape(Q_LORA, H, D_QK)
    uq2 = jnp.concatenate([uq, -uq[:, :, D_NOPE + half:], uq[:, :, D_NOPE:D_NOPE + half]], axis=2)
    ukv = w_ukv.reshape(KV_LORA, H, D_NOPE + D_V)
    w_uk = ukv[:, :, :D_NOPE].reshape(KV_LORA, H * D_NOPE)
    w_uv = ukv[:, :, D_NOPE:].reshape(KV_LORA, H * D_V)
    bf = lambda w: w.astype(BF16)

    pad = jnp.zeros((LANE - D_ROPE,), F32)

    def rotary_gains(g):
        g = g.astype(F32)
        g1 = jnp.concatenate([g[D_NOPE:], pad]).reshape(1, LANE)
        g2 = jnp.concatenate([g[D_NOPE + half:], g[D_NOPE:D_NOPE + half], pad]).reshape(1, LANE)
        return g[:D_NOPE].reshape(1, D_NOPE), g1, g2

    gq_nope, gq_r1, gq_r2 = rotary_gains(mla_q_norm)
    gk_nope, gk_r1, gk_r2 = rotary_gains(mla_k_norm)
    inv_freq = jnp.power(ROPE_THETA, -jnp.arange(half, dtype=F32) / half)
    invf = jnp.tile(inv_freq, LANE // half).reshape(1, LANE)
    phase = jnp.concatenate([jnp.zeros((D_ROPE,), F32), jnp.full((D_ROPE,), -jnp.pi / 2, F32)]).reshape(1, LANE)

    kmem, vmem = pl.pallas_call(
        _mem_kv_kernel,
        grid=(B,),
        in_specs=[pl.BlockSpec((1, M, D), lambda b: (b, 0, 0)),
                  _full((1, D)), _full((D, 2 * WIDTH)), _full((1, MEM_D))],
        out_specs=[pl.BlockSpec((1, M, WIDTH), lambda b: (b, 0, 0))] * 2,
        out_shape=[jax.ShapeDtypeStruct((B, M, WIDTH), BF16)] * 2,
        compiler_params=_params(("arbitrary",)),
        name="mem_kv",
    )(mem, _row(norm_mem), bf(w_mem_kv), _row(mem_k_norm))

    tm = min(512, S)
    assert S % tm == 0
    steps_per_batch = S // tm
    x2 = x.reshape(T, D)
    assert tm % LANE == 0
    pos2 = positions.reshape(T // tm, tm // LANE, LANE).astype(jnp.int32)
    pos_spec = pl.BlockSpec((1, tm // LANE, LANE), lambda i: (i, 0, 0))
    tok = lambda w: pl.BlockSpec((tm, w), lambda i: (i, 0))
    weights = [bf(w_cqkr), bf(w_ckv), bf(w_hq), bf(w_hf), bf(w_hi), bf(w_hg), bf(w_mq)]
    rest = [_row(q_a_norm), bf(uq2.reshape(Q_LORA, H * D_QK_PAD)), _row(kv_a_norm), bf(w_uk), bf(w_uv),
            gq_nope, gq_r1, gq_r2, gk_nope, gk_r1, gk_r2, _row(mem_q_norm)]
    mem_spec = pl.BlockSpec((1, M, WIDTH), lambda i: (i // steps_per_batch, 0, 0))
    vt_spec = pl.BlockSpec((1, H, 1, D_V, tm),
                           lambda i: (i // steps_per_batch, 0, i % steps_per_batch, 0, 0))
    q_all, k_all, vt_all, hq, hf, hi, hg, ymem = pl.pallas_call(
        _in_proj_kernel,
        grid=(T // tm,),
        in_specs=([tok(D), pos_spec, _full((1, LANE)), _full((1, LANE)), _full((1, D))]
                  + [_full(w.shape) for w in weights] + [_full(r.shape) for r in rest]
                  + [mem_spec, mem_spec, _full((1, WIDTH))]),
        out_specs=[tok(H * D_QK_PAD), tok(H * D_QK_PAD), vt_spec] + [tok(WIDTH)] * 5,
        out_shape=[jax.ShapeDtypeStruct((T, H * D_QK_PAD), BF16)] * 2
        + [jax.ShapeDtypeStruct((B, H, steps_per_batch, D_V, tm), BF16)]
        + [jax.ShapeDtypeStruct((T, WIDTH), dt) for dt in (BF16, F32, BF16, BF16, BF16)],
        compiler_params=_params(("arbitrary",)),
        name="in_proj",
    )(x2, pos2, invf, phase, _row(norm_mix), *weights, *rest, kmem, vmem, _row(mem_out_norm))

    blk = tm
    hpb = 2
    y_mla = pl.pallas_call(
        functools.partial(_attn_kernel, blk=blk, heads=hpb),
        grid=(B, H // hpb, S // blk),
        in_specs=[pl.BlockSpec((1, blk, hpb * D_QK_PAD), lambda b, h, i: (b, i, h)),
                  pl.BlockSpec((1, S, hpb * D_QK_PAD), lambda b, h, i: (b, 0, h)),
                  pl.BlockSpec((1, hpb, S // blk, D_V, blk), lambda b, h, i: (b, h, 0, 0, 0))],
        out_specs=pl.BlockSpec((1, blk, hpb * D_V), lambda b, h, i: (b, i, h)),
        out_shape=jax.ShapeDtypeStruct((B, S, H * D_V), BF16),
        scratch_shapes=[pltpu.VMEM((hpb, 1, blk), F32), pltpu.VMEM((hpb, 1, blk), F32),
                        pltpu.VMEM((hpb, D_V, blk), F32),
                        pltpu.VMEM((hpb, blk, blk), F32), pltpu.VMEM((hpb, blk, blk), F32),
                        pltpu.VMEM((hpb, 1, blk), F32), pltpu.VMEM((hpb, 1, blk), F32)],
        compiler_params=_params(("arbitrary", "arbitrary", "arbitrary")),
        name="mla_attn",
    )(q_all.reshape(B, S, H * D_QK_PAD), k_all.reshape(B, S, H * D_QK_PAD), vt_all)

    ts = min(512, S)
    assert S % ts == 0 and ts % HG_CHUNK == 0
    seq = lambda: pl.BlockSpec((1, ts, WIDTH), lambda b, i: (b, i, 0))
    n_layers = hg_lb_logits.shape[0]
    y_hg = pl.pallas_call(
        functools.partial(_hgrn_kernel, n_chunks=ts // HG_CHUNK, layer=layer),
        grid=(B, S // ts),
        in_specs=[seq(), seq(), seq(), seq(), _full((n_layers, WIDTH)), _full((1, WIDTH))],
        out_specs=seq(),
        out_shape=jax.ShapeDtypeStruct((B, S, WIDTH), BF16),
        scratch_shapes=[pltpu.VMEM((N_HEADS, HG_D, HG_D), F32)],
        compiler_params=_params(("arbitrary", "arbitrary")),
        name="hgrn",
    )(hq.reshape(B, S, WIDTH), hf.reshape(B, S, WIDTH), hi.reshape(B, S, WIDTH), hg.reshape(B, S, WIDTH),
      hg_lb_logits.astype(F32), _row(hg_out_norm))

    d_ff = w_gate.shape[1]
    once = lambda shape: pl.BlockSpec(shape, lambda i: (0, 0), pipeline_mode=pl.Buffered(1))
    out = pl.pallas_call(
        _out_ffn_kernel,
        grid=(T // tm,),
        in_specs=[tok(D), tok(WIDTH), tok(WIDTH), tok(WIDTH), _full((1, WIDTH)), once((3 * WIDTH, D)),
                  _full((1, D)), once((D, d_ff)), once((D, d_ff)), once((d_ff, D))],
        out_specs=tok(D),
        out_shape=jax.ShapeDtypeStruct((T, D), x.dtype),
        compiler_params=_params(("arbitrary",)),
        name="out_ffn",
    )(x2, y_mla.reshape(T, WIDTH), y_hg.reshape(T, WIDTH), ymem, _row(mla_out_norm), bf(w_out),
      _row(norm_ffn), bf(w_gate), bf(w_up), bf(w_down))
    return out.reshape(B, S, D)


def kernel(x, mem, positions, norm_mix, norm_mem, w_in, q_a_norm, w_uq, kv_a_norm, w_ukv, mla_q_norm, mla_k_norm, hg_lb_logits, hg_out_norm, w_mem_kv, mem_q_norm, mem_k_norm, mla_out_norm, mem_out_norm, w_out, norm_ffn, w_gate, w_up, w_down):
    depth = w_in.shape[0]
    for l in range(depth):
        x = _layer(x, mem, positions, l, norm_mix[l], norm_mem[l], w_in[l], q_a_norm[l], w_uq[l],
                   kv_a_norm[l], w_ukv[l], mla_q_norm[l], mla_k_norm[l], hg_lb_logits, hg_out_norm[l],
                   w_mem_kv[l], mem_q_norm[l], mem_k_norm[l], mla_out_norm[l], mem_out_norm[l],
                   w_out[l], norm_ffn[l], w_gate[l], w_up[l], w_down[l])
    return x
```

```python
import functools

import jax
import jax.numpy as jnp
from jax import lax
from jax.experimental import pallas as pl
from jax.experimental.pallas import tpu as pltpu

F32 = jnp.float32
BF16 = jnp.bfloat16

EPS = 1e-6
N_HEADS = 4
D_NOPE = 128
D_ROPE = 64
D_QK = D_NOPE + D_ROPE
D_QK_PAD = 256
D_V = 128
Q_LORA = 384
KV_LORA = 256
ROPE_THETA = 10000.0
LOG2E = 1.4426950408889634
HG_D = 128
MEM_D = 128
WIDTH = N_HEADS * 128

LANE = 128
HG_CHUNK = 128
HG_SUB = 8

VMEM_LIMIT = 56 * 1024 * 1024


def _dot(a, b):
    return jnp.dot(a, b, preferred_element_type=F32)


def _dot_nt(a, b):
    return lax.dot_general(a, b, (((1,), (1,)), ((), ())), preferred_element_type=F32)


def _rms(x, g, width):
    ss = jnp.sum(x * x, axis=-1, keepdims=True)
    return x * lax.rsqrt(ss * (1.0 / width) + EPS) * g


def _sigmoid(x):
    return 1.0 / (1.0 + jnp.exp(-x))


def _mem_kv_kernel(mem_ref, g_ref, w_ref, kn_ref, k_out, v_out):
    m = mem_ref[0].astype(F32)
    mh = _rms(m, g_ref[...], m.shape[-1]).astype(BF16)
    kv = _dot(mh, w_ref[...])
    for h in range(N_HEADS):
        kh = kv[:, h * MEM_D:(h + 1) * MEM_D]
        k_out[0, :, h * MEM_D:(h + 1) * MEM_D] = _rms(kh, kn_ref[...], MEM_D).astype(BF16)
    v_out[0] = kv[:, WIDTH:].astype(BF16)


def _in_proj_kernel(x_ref, pos_ref, invf_ref, phase_ref, gmix_ref,
                    w_cqkr_ref, w_ckv_ref, w_hq_ref, w_hf_ref, w_hi_ref, w_hg_ref, w_mq_ref,
                    gqa_ref, w_uq_ref, gkva_ref, w_uk_ref, w_uv_ref,
                    gq_nope_ref, gq_r1_ref, gq_r2_ref, gk_nope_ref, gk_r1_ref, gk_r2_ref,
                    gmq_ref, kmem_ref, vmem_ref, gmo_ref,
                    q_out, k_out, vt_out, hq_out, hf_out, hi_out, hg_out, ymem_out):
    x = x_ref[...].astype(F32)
    h = _rms(x, gmix_ref[...], x.shape[-1]).astype(BF16)

    big = _dot(h, w_cqkr_ref[...])
    ckv = _dot(h, w_ckv_ref[...])
    hq = _dot(h, w_hq_ref[...])
    cqn = _rms(big[:, :Q_LORA], gqa_ref[...], Q_LORA).astype(BF16)
    ckvn = _rms(ckv, gkva_ref[...], KV_LORA).astype(BF16)
    qa = _dot(cqn, w_uq_ref[...])
    kn = _dot(ckvn, w_uk_ref[...])
    vt = _dot(ckvn, w_uv_ref[...]).T
    mq = _dot(h, w_mq_ref[...])
    hf_out[...] = _dot(h, w_hf_ref[...])

    posf = pos_ref[0].astype(F32)
    pos = jnp.concatenate([jnp.broadcast_to(posf[r:r + 1, :], (LANE, LANE)).T
                           for r in range(posf.shape[0])], axis=0)
    cs = jnp.cos(pos * invf_ref[...] + phase_ref[...])
    sc = pltpu.roll(cs, D_ROPE, axis=1)

    def rotary(tile, g1_ref, g2_ref):
        return tile * (g1_ref[...] * cs) + pltpu.roll(tile, D_ROPE, axis=1) * (g2_ref[...] * sc)

    q_scale = LOG2E * D_QK ** -0.5
    for hd in range(N_HEADS):
        nope = qa[:, hd * D_QK_PAD:hd * D_QK_PAD + D_NOPE]
        tile = qa[:, hd * D_QK_PAD + D_NOPE:(hd + 1) * D_QK_PAD]
        ss = jnp.sum(nope * nope, axis=-1, keepdims=True) + 0.5 * jnp.sum(tile * tile, axis=-1, keepdims=True)
        rinv = lax.rsqrt(ss * (1.0 / D_QK) + EPS) * q_scale
        q_out[:, hd * D_QK_PAD:hd * D_QK_PAD + D_NOPE] = (nope * gq_nope_ref[...] * rinv).astype(BF16)
        q_out[:, hd * D_QK_PAD + D_NOPE:(hd + 1) * D_QK_PAD] = (
            rotary(tile, gq_r1_ref, gq_r2_ref) * rinv).astype(BF16)

    vt_out[0, :, 0] = vt.reshape(N_HEADS, D_V, vt.shape[-1]).astype(BF16)
    ktile = big[:, Q_LORA:]
    ss_r = 0.5 * jnp.sum(ktile * ktile, axis=-1, keepdims=True)
    rot = rotary(ktile, gk_r1_ref, gk_r2_ref)
    for hd in range(N_HEADS):
        a = kn[:, hd * D_NOPE:(hd + 1) * D_NOPE]
        ss = jnp.sum(a * a, axis=-1, keepdims=True) + ss_r
        rinv = lax.rsqrt(ss * (1.0 / D_QK) + EPS)
        k_out[:, hd * D_QK_PAD:hd * D_QK_PAD + D_NOPE] = (a * gk_nope_ref[...] * rinv).astype(BF16)
        k_out[:, hd * D_QK_PAD + D_NOPE:(hd + 1) * D_QK_PAD] = (rot * rinv).astype(BF16)

    ss_ = []
    for hd in range(N_HEADS):
        sl = slice(hd * MEM_D, (hd + 1) * MEM_D)
        qh = (_rms(mq[:, sl], gmq_ref[...], MEM_D) * (MEM_D ** -0.5)).astype(BF16)
        ss_.append(_dot_nt(qh, kmem_ref[0, :, sl]))
    hi_out[...] = _dot(h, w_hi_ref[...]).astype(BF16)
    hq_out[...] = (hq * _sigmoid(hq) * (HG_D ** -0.5)).astype(BF16)
    ys = []
    for hd in range(N_HEADS):
        sl = slice(hd * MEM_D, (hd + 1) * MEM_D)
        s = ss_[hd]
        p = jnp.exp(s - jnp.max(s, axis=-1, keepdims=True))
        l = jnp.sum(p, axis=-1, keepdims=True)
        ys.append(_dot(p.astype(BF16), vmem_ref[0, :, sl]) / l)
    hg = _dot(h, w_hg_ref[...])
    hg_out[...] = (hg * _sigmoid(hg)).astype(BF16)
    y = jnp.concatenate(ys, axis=-1)
    ymem_out[...] = _rms(y, gmo_ref[...], WIDTH).astype(BF16)


def _attn_kernel(q_ref, k_ref, vt_ref, o_ref, m_sc, l_sc, acc_sc, s0_sc, s1_sc, cm0_sc, cm1_sc,
                 *, bq, bk, heads):
    qi = pl.program_id(2)
    s_bufs = (s0_sc, s1_sc)
    cm_bufs = (cm0_sc, cm1_sc)
    m_sc[...] = jnp.full(m_sc.shape, -jnp.inf, F32)
    l_sc[...] = jnp.zeros(l_sc.shape, F32)
    acc_sc[...] = jnp.zeros(acc_sc.shape, F32)

    def scores(t, slot, hd):
        r0 = t * bk if isinstance(t, int) else pl.multiple_of(t * bk, bk)
        q = q_ref[0, :, hd * D_QK_PAD:(hd + 1) * D_QK_PAD]
        k = k_ref[0, pl.ds(r0, bk), hd * D_QK_PAD:(hd + 1) * D_QK_PAD]
        s = _dot_nt(k, q)
        s_bufs[slot][hd] = s
        cm_bufs[slot][hd] = jnp.max(s, axis=0, keepdims=True)

    def accumulate(t, slot, hd, key_offset=None):
        s = s_bufs[slot][hd]
        if key_offset is not None:
            kv = lax.broadcasted_iota(jnp.int32, (bk, bq), 0) + key_offset
            qq = lax.broadcasted_iota(jnp.int32, (bk, bq), 1)
            s = jnp.where(kv <= qq, s, -jnp.inf)
            cm = jnp.max(s, axis=0, keepdims=True)
        else:
            cm = cm_bufs[slot][hd]
        m_prev = m_sc[hd]
        m_new = jnp.maximum(m_prev, cm)
        p = jnp.exp2(s - m_new)
        alpha = jnp.exp2(m_prev - m_new)
        l_sc[hd] = alpha * l_sc[hd] + jnp.sum(p, axis=0, keepdims=True)
        acc_sc[hd] = alpha * acc_sc[hd] + _dot(vt_ref[0, hd, t], p.astype(BF16))
        m_sc[hd] = m_new

    for hd in range(heads):
        scores(0, 0, hd)

    def advance(t, slot, key_offset=None):
        for hd in range(heads):
            scores(t + 1, 1 - slot, hd)
            accumulate(t, slot, hd, key_offset)

    def body(jj, carry):
        advance(2 * jj, 0)
        advance(2 * jj + 1, 1)
        return carry

    lax.fori_loop(0, qi, body, 0)
    advance(2 * qi, 0, key_offset=0)
    for hd in range(heads):
        accumulate(2 * qi + 1, 1, hd, key_offset=bk)

    for hd in range(heads):
        o = (acc_sc[hd] / l_sc[hd]).T
        o_ref[0, :, hd * D_V:(hd + 1) * D_V] = o.astype(o_ref.dtype)


def _pair_reference(b, m):
    c = b.shape[0]
    n2 = c // (2 * m)
    br = b.reshape(n2, 2 * m, b.shape[1])
    last = br[:, m - 1:m, :]
    return jnp.broadcast_to(last, br.shape).reshape(b.shape)


def _hgrn_kernel(hq_ref, hf_ref, hi_ref, hg_ref, lbl_ref, gain_ref, o_ref, st_ref, *, n_chunks, layer):
    C = HG_CHUNK

    @pl.when(pl.program_id(1) == 0)
    def _():
        st_ref[...] = jnp.zeros(st_ref.shape, F32)

    lg = lbl_ref[...].astype(F32)
    e = jnp.exp(lg - jnp.max(lg, axis=0, keepdims=True))
    lb = jnp.sum(e[:layer + 1], axis=0, keepdims=True) / jnp.sum(e, axis=0, keepdims=True)

    row = lax.broadcasted_iota(jnp.int32, (C, C), 0)
    col = lax.broadcasted_iota(jnp.int32, (C, C), 1)
    tri = (col <= row).astype(BF16)
    diag_mask = (((row ^ col) & ~(HG_SUB - 1)) | jnp.where(col <= row, 0, 1)) == 0
    sub_keep = [jnp.where((row & (HG_SUB - 1)) == s_off, 1.0, 0.0).astype(BF16) for s_off in range(HG_SUB)]
    levels = []
    m = HG_SUB
    while m < C:
        bad = ((row ^ col) & ~(2 * m - 1)) | ((row & m) ^ m) | (col & m)
        levels.append((m, bad == 0))
        m *= 2

    def chunk(c, carry):
        r0 = pl.multiple_of(c * C, C)
        fr = hf_ref[0, pl.ds(r0, C), :]
        f = lb + (1.0 - lb) * _sigmoid(fr)
        logf = jnp.log(f) * LOG2E
        kk_all = 1.0 - f
        t0 = logf.astype(BF16)
        r1 = logf - t0.astype(F32)
        t1 = r1.astype(BF16)
        t2 = (r1 - t1.astype(F32)).astype(BF16)
        b_all = _dot(tri, t0) + _dot(tri, t1) + _dot(tri, t2)
        q_all = hq_ref[0, pl.ds(r0, C), :].astype(F32)
        v_all = hi_ref[0, pl.ds(r0, C), :]
        g_all = hg_ref[0, pl.ds(r0, C), :].astype(F32)
        for hd in range(N_HEADS):
            sl = slice(hd * HG_D, (hd + 1) * HG_D)
            b = b_all[:, sl]
            q = q_all[:, sl]
            kk = kk_all[:, sl]
            v = v_all[:, sl]
            kb = kk.astype(BF16)

            b3 = b.reshape(C // HG_SUB, HG_SUB, HG_D)
            ms = []
            ks = []
            for s_off in range(HG_SUB):
                bs = jnp.broadcast_to(b3[:, s_off:s_off + 1, :], b3.shape).reshape(C, HG_D)
                ms.append((q * jnp.exp2(jnp.minimum(b - bs, 0.0))).astype(BF16))
                ks.append(kb * sub_keep[s_off])
            ad = _dot_nt(jnp.concatenate(ms, axis=1), jnp.concatenate(ks, axis=1))
            a = jnp.where(diag_mask, ad, 0.0)
            for m_blk, mask in levels:
                d = b - _pair_reference(b, m_blk)
                qe = (q * jnp.exp2(d)).astype(BF16)
                ke = (kk * jnp.exp2(-d)).astype(BF16)
                a = jnp.where(mask, _dot_nt(qe, ke), a)

            st = st_ref[hd]
            o = _dot(a.astype(BF16), v) + _dot_nt((q * jnp.exp2(b)).astype(BF16), st.astype(BF16))
            b_last = b[C - 1:C, :]
            kd = (kk * jnp.exp2(b_last - b)).astype(BF16)
            vt = v.astype(F32).T.astype(BF16)
            st_ref[hd] = st * jnp.exp2(b_last) + _dot(vt, kd)

            on = _rms(o, gain_ref[:, sl], HG_D)
            o_ref[0, pl.ds(r0, C), sl] = (on * g_all[:, sl]).astype(o_ref.dtype)
        return carry

    lax.fori_loop(0, n_chunks, chunk, 0)


def _out_ffn_kernel(x_ref, ymla_ref, yhg_ref, ymem_ref, gmla_ref, w_out_ref, gffn_ref,
                    w_gate_ref, w_up_ref, w_down_ref, o_ref):
    x = x_ref[...].astype(F32)
    ymla = _rms(ymla_ref[...].astype(F32), gmla_ref[...], WIDTH).astype(BF16)
    mix = (_dot(ymla, w_out_ref[0:WIDTH, :])
           + _dot(yhg_ref[...], w_out_ref[WIDTH:2 * WIDTH, :])
           + _dot(ymem_ref[...], w_out_ref[2 * WIDTH:3 * WIDTH, :]))
    x1 = x + mix
    h2 = _rms(x1, gffn_ref[...], x1.shape[-1]).astype(BF16)
    g = _dot(h2, w_gate_ref[...])
    u = _dot(h2, w_up_ref[...])
    act = (g * _sigmoid(g) * u).astype(BF16)
    o_ref[...] = (x1 + _dot(act, w_down_ref[...])).astype(o_ref.dtype)


def _full(shape):
    nd = len(shape)
    return pl.BlockSpec(shape, lambda *_: (0,) * nd)


def _params(sem):
    return pltpu.CompilerParams(dimension_semantics=sem, vmem_limit_bytes=VMEM_LIMIT)


def _row(v):
    return v.reshape(1, -1).astype(F32)


def _layer(x, mem, positions, layer, norm_mix, norm_mem, w_in, q_a_norm, w_uq, kv_a_norm, w_ukv,
           mla_q_norm, mla_k_norm, hg_lb_logits, hg_out_norm, w_mem_kv, mem_q_norm, mem_k_norm,
           mla_out_norm, mem_out_norm, w_out, norm_ffn, w_gate, w_up, w_down):
    B, S, D = x.shape
    M = mem.shape[1]
    T = B * S
    half = D_ROPE // 2
    H = N_HEADS

    sizes = (Q_LORA, KV_LORA, D_ROPE, WIDTH, WIDTH, WIDTH, WIDTH, WIDTH)
    offs = [0]
    for sz in sizes:
        offs.append(offs[-1] + sz)
    w_cq, w_ckv, w_kr, w_hq, w_hf, w_hi, w_hg, w_mq = [
        w_in[:, offs[j]:offs[j + 1]] for j in range(len(sizes))]
    w_cqkr = jnp.concatenate([w_cq, w_kr, -w_kr[:, half:], w_kr[:, :half]], axis=1)
    uq = w_uq.reshape(Q_LORA, H, D_QK)
    uq2 = jnp.concatenate([uq, -uq[:, :, D_NOPE + half:], uq[:, :, D_NOPE:D_NOPE + half]], axis=2)
    ukv = w_ukv.reshape(KV_LORA, H, D_NOPE + D_V)
    w_uk = ukv[:, :, :D_NOPE].reshape(KV_LORA, H * D_NOPE)
    w_uv = ukv[:, :, D_NOPE:].reshape(KV_LORA, H * D_V)
    bf = lambda w: w.astype(BF16)

    pad = jnp.zeros((LANE - D_ROPE,), F32)

    def rotary_gains(g):
        g = g.astype(F32)
        g1 = jnp.concatenate([g[D_NOPE:], pad]).reshape(1, LANE)
        g2 = jnp.concatenate([g[D_NOPE + half:], g[D_NOPE:D_NOPE + half], pad]).reshape(1, LANE)
        return g[:D_NOPE].reshape(1, D_NOPE), g1, g2

    gq_nope, gq_r1, gq_r2 = rotary_gains(mla_q_norm)
    gk_nope, gk_r1, gk_r2 = rotary_gains(mla_k_norm)
    inv_freq = jnp.power(ROPE_THETA, -jnp.arange(half, dtype=F32) / half)
    invf = jnp.tile(inv_freq, LANE // half).reshape(1, LANE)
    phase = jnp.concatenate([jnp.zeros((D_ROPE,), F32), jnp.full((D_ROPE,), -jnp.pi / 2, F32)]).reshape(1, LANE)

    kmem, vmem = pl.pallas_call(
        _mem_kv_kernel,
        grid=(B,),
        in_specs=[pl.BlockSpec((1, M, D), lambda b: (b, 0, 0)),
                  _full((1, D)), _full((D, 2 * WIDTH)), _full((1, MEM_D))],
        out_specs=[pl.BlockSpec((1, M, WIDTH), lambda b: (b, 0, 0))] * 2,
        out_shape=[jax.ShapeDtypeStruct((B, M, WIDTH), BF16)] * 2,
        compiler_params=_params(("arbitrary",)),
        name="mem_kv",
    )(mem, _row(norm_mem), bf(w_mem_kv), _row(mem_k_norm))

    tm = min(512, S)
    assert S % tm == 0
    steps_per_batch = S // tm
    x2 = x.reshape(T, D)
    assert tm % LANE == 0
    pos2 = positions.reshape(T // tm, tm // LANE, LANE).astype(jnp.int32)
    pos_spec = pl.BlockSpec((1, tm // LANE, LANE), lambda i: (i, 0, 0))
    tok = lambda w: pl.BlockSpec((tm, w), lambda i: (i, 0))
    weights = [bf(w_cqkr), bf(w_ckv), bf(w_hq), bf(w_hf), bf(w_hi), bf(w_hg), bf(w_mq)]
    rest = [_row(q_a_norm), bf(uq2.reshape(Q_LORA, H * D_QK_PAD)), _row(kv_a_norm), bf(w_uk), bf(w_uv),
            gq_nope, gq_r1, gq_r2, gk_nope, gk_r1, gk_r2, _row(mem_q_norm)]
    mem_spec = pl.BlockSpec((1, M, WIDTH), lambda i: (i // steps_per_batch, 0, 0))
    vt_spec = pl.BlockSpec((1, H, 1, D_V, tm),
                           lambda i: (i // steps_per_batch, 0, i % steps_per_batch, 0, 0))
    q_all, k_all, vt_all, hq, hf, hi, hg, ymem = pl.pallas_call(
        _in_proj_kernel,
        grid=(T // tm,),
        in_specs=([tok(D), pos_spec, _full((1, LANE)), _full((1, LANE)), _full((1, D))]
                  + [_full(w.shape) for w in weights] + [_full(r.shape) for r in rest]
                  + [mem_spec, mem_spec, _full((1, WIDTH))]),
        out_specs=[tok(H * D_QK_PAD), tok(H * D_QK_PAD), vt_spec] + [tok(WIDTH)] * 5,
        out_shape=[jax.ShapeDtypeStruct((T, H * D_QK_PAD), BF16)] * 2
        + [jax.ShapeDtypeStruct((B, H, steps_per_batch, D_V, tm), BF16)]
        + [jax.ShapeDtypeStruct((T, WIDTH), dt) for dt in (BF16, F32, BF16, BF16, BF16)],
        compiler_params=_params(("arbitrary",)),
        name="in_proj",
    )(x2, pos2, invf, phase, _row(norm_mix), *weights, *rest, kmem, vmem, _row(mem_out_norm))

    bk = tm
    bq = 2 * bk
    assert S % bq == 0
    hpb = 2
    y_mla = pl.pallas_call(
        functools.partial(_attn_kernel, bq=bq, bk=bk, heads=hpb),
        grid=(B, H // hpb, S // bq),
        in_specs=[pl.BlockSpec((1, bq, hpb * D_QK_PAD), lambda b, h, i: (b, i, h)),
                  pl.BlockSpec((1, S, hpb * D_QK_PAD), lambda b, h, i: (b, 0, h)),
                  pl.BlockSpec((1, hpb, S // bk, D_V, bk), lambda b, h, i: (b, h, 0, 0, 0))],
        out_specs=pl.BlockSpec((1, bq, hpb * D_V), lambda b, h, i: (b, i, h)),
        out_shape=jax.ShapeDtypeStruct((B, S, H * D_V), BF16),
        scratch_shapes=[pltpu.VMEM((hpb, 1, bq), F32), pltpu.VMEM((hpb, 1, bq), F32),
                        pltpu.VMEM((hpb, D_V, bq), F32),
                        pltpu.VMEM((hpb, bk, bq), F32), pltpu.VMEM((hpb, bk, bq), F32),
                        pltpu.VMEM((hpb, 1, bq), F32), pltpu.VMEM((hpb, 1, bq), F32)],
        compiler_params=_params(("arbitrary", "arbitrary", "arbitrary")),
        name="mla_attn",
    )(q_all.reshape(B, S, H * D_QK_PAD), k_all.reshape(B, S, H * D_QK_PAD), vt_all)

    ts = min(512, S)
    assert S % ts == 0 and ts % HG_CHUNK == 0
    seq = lambda: pl.BlockSpec((1, ts, WIDTH), lambda b, i: (b, i, 0))
    n_layers = hg_lb_logits.shape[0]
    y_hg = pl.pallas_call(
        functools.partial(_hgrn_kernel, n_chunks=ts // HG_CHUNK, layer=layer),
        grid=(B, S // ts),
        in_specs=[seq(), seq(), seq(), seq(), _full((n_layers, WIDTH)), _full((1, WIDTH))],
        out_specs=seq(),
        out_shape=jax.ShapeDtypeStruct((B, S, WIDTH), BF16),
        scratch_shapes=[pltpu.VMEM((N_HEADS, HG_D, HG_D), F32)],
        compiler_params=_params(("arbitrary", "arbitrary")),
        name="hgrn",
    )(hq.reshape(B, S, WIDTH), hf.reshape(B, S, WIDTH), hi.reshape(B, S, WIDTH), hg.reshape(B, S, WIDTH),
      hg_lb_logits.astype(F32), _row(hg_out_norm))

    d_ff = w_gate.shape[1]
    once = lambda shape: pl.BlockSpec(shape, lambda i: (0, 0), pipeline_mode=pl.Buffered(1))
    out = pl.pallas_call(
        _out_ffn_kernel,
        grid=(T // tm,),
        in_specs=[tok(D), tok(WIDTH), tok(WIDTH), tok(WIDTH), _full((1, WIDTH)), once((3 * WIDTH, D)),
                  _full((1, D)), once((D, d_ff)), once((D, d_ff)), once((d_ff, D))],
        out_specs=tok(D),
        out_shape=jax.ShapeDtypeStruct((T, D), x.dtype),
        compiler_params=_params(("arbitrary",)),
        name="out_ffn",
    )(x2, y_mla.reshape(T, WIDTH), y_hg.reshape(T, WIDTH), ymem, _row(mla_out_norm), bf(w_out),
      _row(norm_ffn), bf(w_gate), bf(w_up), bf(w_down))
    return out.reshape(B, S, D)


def kernel(x, mem, positions, norm_mix, norm_mem, w_in, q_a_norm, w_uq, kv_a_norm, w_ukv, mla_q_norm, mla_k_norm, hg_lb_logits, hg_out_norm, w_mem_kv, mem_q_norm, mem_k_norm, mla_out_norm, mem_out_norm, w_out, norm_ffn, w_gate, w_up, w_down):
    depth = w_in.shape[0]
    for l in range(depth):
        x = _layer(x, mem, positions, l, norm_mix[l], norm_mem[l], w_in[l], q_a_norm[l], w_uq[l],
                   kv_a_norm[l], w_ukv[l], mla_q_norm[l], mla_k_norm[l], hg_lb_logits, hg_out_norm[l],
                   w_mem_kv[l], mem_q_norm[l], mem_k_norm[l], mla_out_norm[l], mem_out_norm[l],
                   w_out[l], norm_ffn[l], w_gate[l], w_up[l], w_down[l])
    return x
```

```python
import functools

import jax
import jax.numpy as jnp
from jax import lax
from jax.experimental import pallas as pl
from jax.experimental.pallas import tpu as pltpu

F32 = jnp.float32
BF16 = jnp.bfloat16

EPS = 1e-6
N_HEADS = 4
D_NOPE = 128
D_ROPE = 64
D_QK = D_NOPE + D_ROPE
D_QK_PAD = 256
D_V = 128
Q_LORA = 384
KV_LORA = 256
ROPE_THETA = 10000.0
LOG2E = 1.4426950408889634
HG_D = 128
MEM_D = 128
WIDTH = N_HEADS * 128

LANE = 128
HG_CHUNK = 128
HG_SUB = 8
ONES_ROWS = 16

VMEM_LIMIT = 56 * 1024 * 1024


def _dot(a, b):
    return jnp.dot(a, b, preferred_element_type=F32)


def _dot_nt(a, b):
    return lax.dot_general(a, b, (((1,), (1,)), ((), ())), preferred_element_type=F32)


def _rms(x, g, width):
    ss = jnp.sum(x * x, axis=-1, keepdims=True)
    return x * lax.rsqrt(ss * (1.0 / width) + EPS) * g


def _sigmoid(x):
    return 1.0 / (1.0 + jnp.exp(-x))


def _mem_kv_kernel(mem_ref, g_ref, w_ref, kn_ref, k_out, v_out):
    m = mem_ref[0].astype(F32)
    mh = _rms(m, g_ref[...], m.shape[-1]).astype(BF16)
    kv = _dot(mh, w_ref[...])
    for h in range(N_HEADS):
        kh = kv[:, h * MEM_D:(h + 1) * MEM_D]
        k_out[0, :, h * MEM_D:(h + 1) * MEM_D] = _rms(kh, kn_ref[...], MEM_D).astype(BF16)
    v_out[0] = kv[:, WIDTH:].astype(BF16)


def _in_proj_kernel(x_ref, pos_ref, invf_ref, phase_ref, gmix_ref,
                    w_cqkr_ref, w_ckv_ref, w_hq_ref, w_hf_ref, w_hi_ref, w_hg_ref, w_mq_ref,
                    gqa_ref, w_uq_ref, gkva_ref, w_uk_ref, w_uv_ref,
                    gq_nope_ref, gq_r1_ref, gq_r2_ref, gk_nope_ref, gk_r1_ref, gk_r2_ref,
                    gmq_ref, kmem_ref, vmem_ref, gmo_ref,
                    q_out, k_out, vt_out, hq_out, hf_out, hi_out, hg_out, ymem_out):
    x = x_ref[...].astype(F32)
    h = _rms(x, gmix_ref[...], x.shape[-1]).astype(BF16)

    big = _dot(h, w_cqkr_ref[...])
    ckv = _dot(h, w_ckv_ref[...])
    hq = _dot(h, w_hq_ref[...])
    cqn = _rms(big[:, :Q_LORA], gqa_ref[...], Q_LORA).astype(BF16)
    ckvn = _rms(ckv, gkva_ref[...], KV_LORA).astype(BF16)
    qa = _dot(cqn, w_uq_ref[...])
    kn = _dot(ckvn, w_uk_ref[...])
    vt = _dot(ckvn, w_uv_ref[...]).T
    mq = _dot(h, w_mq_ref[...])
    hf_out[...] = _dot(h, w_hf_ref[...])

    posf = pos_ref[0].astype(F32)
    pos = jnp.concatenate([jnp.broadcast_to(posf[r:r + 1, :], (LANE, LANE)).T
                           for r in range(posf.shape[0])], axis=0)
    cs = jnp.cos(pos * invf_ref[...] + phase_ref[...])
    sc = pltpu.roll(cs, D_ROPE, axis=1)

    def rotary(tile, g1_ref, g2_ref):
        return tile * (g1_ref[...] * cs) + pltpu.roll(tile, D_ROPE, axis=1) * (g2_ref[...] * sc)

    q_scale = LOG2E * D_QK ** -0.5
    for hd in range(N_HEADS):
        nope = qa[:, hd * D_QK_PAD:hd * D_QK_PAD + D_NOPE]
        tile = qa[:, hd * D_QK_PAD + D_NOPE:(hd + 1) * D_QK_PAD]
        ss = jnp.sum(nope * nope, axis=-1, keepdims=True) + 0.5 * jnp.sum(tile * tile, axis=-1, keepdims=True)
        rinv = lax.rsqrt(ss * (1.0 / D_QK) + EPS) * q_scale
        q_out[:, hd * D_QK_PAD:hd * D_QK_PAD + D_NOPE] = (nope * gq_nope_ref[...] * rinv).astype(BF16)
        q_out[:, hd * D_QK_PAD + D_NOPE:(hd + 1) * D_QK_PAD] = (
            rotary(tile, gq_r1_ref, gq_r2_ref) * rinv).astype(BF16)

    vt_out[0, :, 0] = vt.reshape(N_HEADS, D_V, vt.shape[-1]).astype(BF16)
    ktile = big[:, Q_LORA:]
    ss_r = 0.5 * jnp.sum(ktile * ktile, axis=-1, keepdims=True)
    rot = rotary(ktile, gk_r1_ref, gk_r2_ref)
    for hd in range(N_HEADS):
        a = kn[:, hd * D_NOPE:(hd + 1) * D_NOPE]
        ss = jnp.sum(a * a, axis=-1, keepdims=True) + ss_r
        rinv = lax.rsqrt(ss * (1.0 / D_QK) + EPS)
        k_out[:, hd * D_QK_PAD:hd * D_QK_PAD + D_NOPE] = (a * gk_nope_ref[...] * rinv).astype(BF16)
        k_out[:, hd * D_QK_PAD + D_NOPE:(hd + 1) * D_QK_PAD] = (rot * rinv).astype(BF16)

    ss_ = []
    for hd in range(N_HEADS):
        sl = slice(hd * MEM_D, (hd + 1) * MEM_D)
        qh = (_rms(mq[:, sl], gmq_ref[...], MEM_D) * (MEM_D ** -0.5)).astype(BF16)
        ss_.append(_dot_nt(qh, kmem_ref[0, :, sl]))
    hi_out[...] = _dot(h, w_hi_ref[...]).astype(BF16)
    hq_out[...] = (hq * _sigmoid(hq) * (HG_D ** -0.5)).astype(BF16)
    ys = []
    for hd in range(N_HEADS):
        sl = slice(hd * MEM_D, (hd + 1) * MEM_D)
        s = ss_[hd]
        p = jnp.exp(s - jnp.max(s, axis=-1, keepdims=True))
        l = jnp.sum(p, axis=-1, keepdims=True)
        ys.append(_dot(p.astype(BF16), vmem_ref[0, :, sl]) / l)
    hg = _dot(h, w_hg_ref[...])
    hg_out[...] = (hg * _sigmoid(hg)).astype(BF16)
    y = jnp.concatenate(ys, axis=-1)
    ymem_out[...] = _rms(y, gmo_ref[...], WIDTH).astype(BF16)


def _attn_kernel(q_ref, k_ref, vt_ref, o_ref, m_sc, acc_sc, s0_sc, s1_sc, cm0_sc, cm1_sc,
                 *, bq, bk, heads):
    qi = pl.program_id(2)
    s_bufs = (s0_sc, s1_sc)
    cm_bufs = (cm0_sc, cm1_sc)
    m_sc[...] = jnp.full(m_sc.shape, -jnp.inf, F32)
    acc_sc[...] = jnp.zeros(acc_sc.shape, F32)

    def scores(t, slot, hd):
        r0 = t * bk if isinstance(t, int) else pl.multiple_of(t * bk, bk)
        q = q_ref[0, :, hd * D_QK_PAD:(hd + 1) * D_QK_PAD]
        k = k_ref[0, pl.ds(r0, bk), hd * D_QK_PAD:(hd + 1) * D_QK_PAD]
        s = _dot_nt(k, q)
        s_bufs[slot][hd] = s
        cm_bufs[slot][hd] = jnp.max(s, axis=0, keepdims=True)

    def accumulate(t, slot, hd, key_offset=None):
        s = s_bufs[slot][hd]
        if key_offset is not None:
            kv = lax.broadcasted_iota(jnp.int32, (bk, bq), 0) + key_offset
            qq = lax.broadcasted_iota(jnp.int32, (bk, bq), 1)
            s = jnp.where(kv <= qq, s, -jnp.inf)
            cm = jnp.max(s, axis=0, keepdims=True)
        else:
            cm = cm_bufs[slot][hd]
        m_prev = m_sc[hd]
        m_new = jnp.maximum(m_prev, cm)
        p = jnp.exp2(s - m_new)
        alpha = jnp.exp2(m_prev - m_new)
        vt1 = jnp.concatenate([vt_ref[0, hd, t], jnp.ones((ONES_ROWS, bk), BF16)], axis=0)
        acc_sc[hd] = alpha * acc_sc[hd] + _dot(vt1, p.astype(BF16))
        m_sc[hd] = m_new

    for hd in range(heads):
        scores(0, 0, hd)

    def advance(t, slot, key_offset=None):
        for hd in range(heads):
            scores(t + 1, 1 - slot, hd)
            accumulate(t, slot, hd, key_offset)

    def body(jj, carry):
        advance(2 * jj, 0)
        advance(2 * jj + 1, 1)
        return carry

    lax.fori_loop(0, qi, body, 0)
    advance(2 * qi, 0, key_offset=0)
    for hd in range(heads):
        accumulate(2 * qi + 1, 1, hd, key_offset=bk)

    for hd in range(heads):
        o = (acc_sc[hd, :D_V] / acc_sc[hd, D_V:D_V + 1]).T
        o_ref[0, :, hd * D_V:(hd + 1) * D_V] = o.astype(o_ref.dtype)


def _pair_reference(b, m):
    c = b.shape[0]
    n2 = c // (2 * m)
    br = b.reshape(n2, 2 * m, b.shape[1])
    last = br[:, m - 1:m, :]
    return jnp.broadcast_to(last, br.shape).reshape(b.shape)


def _hgrn_kernel(hq_ref, hf_ref, hi_ref, hg_ref, lbl_ref, gain_ref, o_ref, st_ref, *, n_chunks, layer):
    C = HG_CHUNK

    @pl.when(pl.program_id(1) == 0)
    def _():
        st_ref[...] = jnp.zeros(st_ref.shape, F32)

    lg = lbl_ref[...].astype(F32)
    e = jnp.exp(lg - jnp.max(lg, axis=0, keepdims=True))
    lb = jnp.sum(e[:layer + 1], axis=0, keepdims=True) / jnp.sum(e, axis=0, keepdims=True)

    row = lax.broadcasted_iota(jnp.int32, (C, C), 0)
    col = lax.broadcasted_iota(jnp.int32, (C, C), 1)
    tri = (col <= row).astype(BF16)
    diag_mask = (((row ^ col) & ~(HG_SUB - 1)) | jnp.where(col <= row, 0, 1)) == 0
    sub_keep = [jnp.where((row & (HG_SUB - 1)) == s_off, 1.0, 0.0).astype(BF16) for s_off in range(HG_SUB)]
    levels = []
    m = HG_SUB
    while m < C:
        bad = ((row ^ col) & ~(2 * m - 1)) | ((row & m) ^ m) | (col & m)
        levels.append((m, bad == 0))
        m *= 2

    def chunk(c, carry):
        r0 = pl.multiple_of(c * C, C)
        fr = hf_ref[0, pl.ds(r0, C), :]
        f = lb + (1.0 - lb) * _sigmoid(fr)
        logf = jnp.log(f) * LOG2E
        kk_all = 1.0 - f
        t0 = logf.astype(BF16)
        r1 = logf - t0.astype(F32)
        t1 = r1.astype(BF16)
        t2 = (r1 - t1.astype(F32)).astype(BF16)
        b_all = _dot(tri, t0) + _dot(tri, t1) + _dot(tri, t2)
        q_all = hq_ref[0, pl.ds(r0, C), :].astype(F32)
        v_all = hi_ref[0, pl.ds(r0, C), :]
        g_all = hg_ref[0, pl.ds(r0, C), :].astype(F32)
        for hd in range(N_HEADS):
            sl = slice(hd * HG_D, (hd + 1) * HG_D)
            b = b_all[:, sl]
            q = q_all[:, sl]
            kk = kk_all[:, sl]
            v = v_all[:, sl]
            kb = kk.astype(BF16)

            b3 = b.reshape(C // HG_SUB, HG_SUB, HG_D)
            ms = []
            ks = []
            for s_off in range(HG_SUB):
                bs = jnp.broadcast_to(b3[:, s_off:s_off + 1, :], b3.shape).reshape(C, HG_D)
                ms.append((q * jnp.exp2(jnp.minimum(b - bs, 0.0))).astype(BF16))
                ks.append(kb * sub_keep[s_off])
            ad = _dot_nt(jnp.concatenate(ms, axis=1), jnp.concatenate(ks, axis=1))
            a = jnp.where(diag_mask, ad, 0.0)
            for m_blk, mask in levels:
                d = b - _pair_reference(b, m_blk)
                qe = (q * jnp.exp2(d)).astype(BF16)
                ke = (kk * jnp.exp2(-d)).astype(BF16)
                a = jnp.where(mask, _dot_nt(qe, ke), a)

            st = st_ref[hd]
            o = _dot(a.astype(BF16), v) + _dot_nt((q * jnp.exp2(b)).astype(BF16), st.astype(BF16))
            b_last = b[C - 1:C, :]
            kd = (kk * jnp.exp2(b_last - b)).astype(BF16)
            vt = v.astype(F32).T.astype(BF16)
            st_ref[hd] = st * jnp.exp2(b_last) + _dot(vt, kd)

            on = _rms(o, gain_ref[:, sl], HG_D)
            o_ref[0, pl.ds(r0, C), sl] = (on * g_all[:, sl]).astype(o_ref.dtype)
        return carry

    lax.fori_loop(0, n_chunks, chunk, 0)


def _out_ffn_kernel(x_ref, ymla_ref, yhg_ref, ymem_ref, gmla_ref, w_out_ref, gffn_ref,
                    w_gate_ref, w_up_ref, w_down_ref, o_ref):
    x = x_ref[...].astype(F32)
    ymla = _rms(ymla_ref[...].astype(F32), gmla_ref[...], WIDTH).astype(BF16)
    mix = (_dot(ymla, w_out_ref[0:WIDTH, :])
           + _dot(yhg_ref[...], w_out_ref[WIDTH:2 * WIDTH, :])
           + _dot(ymem_ref[...], w_out_ref[2 * WIDTH:3 * WIDTH, :]))
    x1 = x + mix
    h2 = _rms(x1, gffn_ref[...], x1.shape[-1]).astype(BF16)
    g = _dot(h2, w_gate_ref[...])
    u = _dot(h2, w_up_ref[...])
    act = (g * _sigmoid(g) * u).astype(BF16)
    o_ref[...] = (x1 + _dot(act, w_down_ref[...])).astype(o_ref.dtype)


def _full(shape):
    nd = len(shape)
    return pl.BlockSpec(shape, lambda *_: (0,) * nd)


def _params(sem):
    return pltpu.CompilerParams(dimension_semantics=sem, vmem_limit_bytes=VMEM_LIMIT)


def _row(v):
    return v.reshape(1, -1).astype(F32)


def _layer(x, mem, positions, layer, norm_mix, norm_mem, w_in, q_a_norm, w_uq, kv_a_norm, w_ukv,
           mla_q_norm, mla_k_norm, hg_lb_logits, hg_out_norm, w_mem_kv, mem_q_norm, mem_k_norm,
           mla_out_norm, mem_out_norm, w_out, norm_ffn, w_gate, w_up, w_down):
    B, S, D = x.shape
    M = mem.shape[1]
    T = B * S
    half = D_ROPE // 2
    H = N_HEADS

    sizes = (Q_LORA, KV_LORA, D_ROPE, WIDTH, WIDTH, WIDTH, WIDTH, WIDTH)
    offs = [0]
    for sz in sizes:
        offs.append(offs[-1] + sz)
    w_cq, w_ckv, w_kr, w_hq, w_hf, w_hi, w_hg, w_mq = [
        w_in[:, offs[j]:offs[j + 1]] for j in range(len(sizes))]
    w_cqkr = jnp.concatenate([w_cq, w_kr, -w_kr[:, half:], w_kr[:, :half]], axis=1)
    uq = w_uq.reshape(Q_LORA, H, D_QK)
    uq2 = jnp.concatenate([uq, -uq[:, :, D_NOPE + half:], uq[:, :, D_NOPE:D_NOPE + half]], axis=2)
    ukv = w_ukv.reshape(KV_LORA, H, D_NOPE + D_V)
    w_uk = ukv[:, :, :D_NOPE].reshape(KV_LORA, H * D_NOPE)
    w_uv = ukv[:, :, D_NOPE:].reshape(KV_LORA, H * D_V)
    bf = lambda w: w.astype(BF16)

    pad = jnp.zeros((LANE - D_ROPE,), F32)

    def rotary_gains(g):
        g = g.astype(F32)
        g1 = jnp.concatenate([g[D_NOPE:], pad]).reshape(1, LANE)
        g2 = jnp.concatenate([g[D_NOPE + half:], g[D_NOPE:D_NOPE + half], pad]).reshape(1, LANE)
        return g[:D_NOPE].reshape(1, D_NOPE), g1, g2

    gq_nope, gq_r1, gq_r2 = rotary_gains(mla_q_norm)
    gk_nope, gk_r1, gk_r2 = rotary_gains(mla_k_norm)
    inv_freq = jnp.power(ROPE_THETA, -jnp.arange(half, dtype=F32) / half)
    invf = jnp.tile(inv_freq, LANE // half).reshape(1, LANE)
    phase = jnp.concatenate([jnp.zeros((D_ROPE,), F32), jnp.full((D_ROPE,), -jnp.pi / 2, F32)]).reshape(1, LANE)

    kmem, vmem = pl.pallas_call(
        _mem_kv_kernel,
        grid=(B,),
        in_specs=[pl.BlockSpec((1, M, D), lambda b: (b, 0, 0)),
                  _full((1, D)), _full((D, 2 * WIDTH)), _full((1, MEM_D))],
        out_specs=[pl.BlockSpec((1, M, WIDTH), lambda b: (b, 0, 0))] * 2,
        out_shape=[jax.ShapeDtypeStruct((B, M, WIDTH), BF16)] * 2,
        compiler_params=_params(("arbitrary",)),
        name="mem_kv",
    )(mem, _row(norm_mem), bf(w_mem_kv), _row(mem_k_norm))

    tm = min(512, S)
    assert S % tm == 0
    steps_per_batch = S // tm
    x2 = x.reshape(T, D)
    assert tm % LANE == 0
    pos2 = positions.reshape(T // tm, tm // LANE, LANE).astype(jnp.int32)
    pos_spec = pl.BlockSpec((1, tm // LANE, LANE), lambda i: (i, 0, 0))
    tok = lambda w: pl.BlockSpec((tm, w), lambda i: (i, 0))
    weights = [bf(w_cqkr), bf(w_ckv), bf(w_hq), bf(w_hf), bf(w_hi), bf(w_hg), bf(w_mq)]
    rest = [_row(q_a_norm), bf(uq2.reshape(Q_LORA, H * D_QK_PAD)), _row(kv_a_norm), bf(w_uk), bf(w_uv),
            gq_nope, gq_r1, gq_r2, gk_nope, gk_r1, gk_r2, _row(mem_q_norm)]
    mem_spec = pl.BlockSpec((1, M, WIDTH), lambda i: (i // steps_per_batch, 0, 0))
    vt_spec = pl.BlockSpec((1, H, 1, D_V, tm),
                           lambda i: (i // steps_per_batch, 0, i % steps_per_batch, 0, 0))
    q_all, k_all, vt_all, hq, hf, hi, hg, ymem = pl.pallas_call(
        _in_proj_kernel,
        grid=(T // tm,),
        in_specs=([tok(D), pos_spec, _full((1, LANE)), _full((1, LANE)), _full((1, D))]
                  + [_full(w.shape) for w in weights] + [_full(r.shape) for r in rest]
                  + [mem_spec, mem_spec, _full((1, WIDTH))]),
        out_specs=[tok(H * D_QK_PAD), tok(H * D_QK_PAD), vt_spec] + [tok(WIDTH)] * 5,
        out_shape=[jax.ShapeDtypeStruct((T, H * D_QK_PAD), BF16)] * 2
        + [jax.ShapeDtypeStruct((B, H, steps_per_batch, D_V, tm), BF16)]
        + [jax.ShapeDtypeStruct((T, WIDTH), dt) for dt in (BF16, F32, BF16, BF16, BF16)],
        compiler_params=_params(("arbitrary",)),
        name="in_proj",
    )(x2, pos2, invf, phase, _row(norm_mix), *weights, *rest, kmem, vmem, _row(mem_out_norm))

    bk = tm
    bq = 2 * bk
    assert S % bq == 0
    hpb = 2
    y_mla = pl.pallas_call(
        functools.partial(_attn_kernel, bq=bq, bk=bk, heads=hpb),
        grid=(B, H // hpb, S // bq),
        in_specs=[pl.BlockSpec((1, bq, hpb * D_QK_PAD), lambda b, h, i: (b, i, h)),
                  pl.BlockSpec((1, S, hpb * D_QK_PAD), lambda b, h, i: (b, 0, h)),
                  pl.BlockSpec((1, hpb, S // bk, D_V, bk), lambda b, h, i: (b, h, 0, 0, 0))],
        out_specs=pl.BlockSpec((1, bq, hpb * D_V), lambda b, h, i: (b, i, h)),
        out_shape=jax.ShapeDtypeStruct((B, S, H * D_V), BF16),
        scratch_shapes=[pltpu.VMEM((hpb, 1, bq), F32),
                        pltpu.VMEM((hpb, D_V + ONES_ROWS, bq), F32),
                        pltpu.VMEM((hpb, bk, bq), F32), pltpu.VMEM((hpb, bk, bq), F32),
                        pltpu.VMEM((hpb, 1, bq), F32), pltpu.VMEM((hpb, 1, bq), F32)],
        compiler_params=_params(("arbitrary", "arbitrary", "arbitrary")),
        name="mla_attn",
    )(q_all.reshape(B, S, H * D_QK_PAD), k_all.reshape(B, S, H * D_QK_PAD), vt_all)

    ts = min(512, S)
    assert S % ts == 0 and ts % HG_CHUNK == 0
    seq = lambda: pl.BlockSpec((1, ts, WIDTH), lambda b, i: (b, i, 0))
    n_layers = hg_lb_logits.shape[0]
    y_hg = pl.pallas_call(
        functools.partial(_hgrn_kernel, n_chunks=ts // HG_CHUNK, layer=layer),
        grid=(B, S // ts),
        in_specs=[seq(), seq(), seq(), seq(), _full((n_layers, WIDTH)), _full((1, WIDTH))],
        out_specs=seq(),
        out_shape=jax.ShapeDtypeStruct((B, S, WIDTH), BF16),
        scratch_shapes=[pltpu.VMEM((N_HEADS, HG_D, HG_D), F32)],
        compiler_params=_params(("arbitrary", "arbitrary")),
        name="hgrn",
    )(hq.reshape(B, S, WIDTH), hf.reshape(B, S, WIDTH), hi.reshape(B, S, WIDTH), hg.reshape(B, S, WIDTH),
      hg_lb_logits.astype(F32), _row(hg_out_norm))

    d_ff = w_gate.shape[1]
    once = lambda shape: pl.BlockSpec(shape, lambda i: (0, 0), pipeline_mode=pl.Buffered(1))
    out = pl.pallas_call(
        _out_ffn_kernel,
        grid=(T // tm,),
        in_specs=[tok(D), tok(WIDTH), tok(WIDTH), tok(WIDTH), _full((1, WIDTH)), once((3 * WIDTH, D)),
                  _full((1, D)), once((D, d_ff)), once((D, d_ff)), once((d_ff, D))],
        out_specs=tok(D),
        out_shape=jax.ShapeDtypeStruct((T, D), x.dtype),
        compiler_params=_params(("arbitrary",)),
        name="out_ffn",
    )(x2, y_mla.reshape(T, WIDTH), y_hg.reshape(T, WIDTH), ymem, _row(mla_out_norm), bf(w_out),
      _row(norm_ffn), bf(w_gate), bf(w_up), bf(w_down))
    return out.reshape(B, S, D)


def kernel(x, mem, positions, norm_mix, norm_mem, w_in, q_a_norm, w_uq, kv_a_norm, w_ukv, mla_q_norm, mla_k_norm, hg_lb_logits, hg_out_norm, w_mem_kv, mem_q_norm, mem_k_norm, mla_out_norm, mem_out_norm, w_out, norm_ffn, w_gate, w_up, w_down):
    depth = w_in.shape[0]
    for l in range(depth):
        x = _layer(x, mem, positions, l, norm_mix[l], norm_mem[l], w_in[l], q_a_norm[l], w_uq[l],
                   kv_a_norm[l], w_ukv[l], mla_q_norm[l], mla_k_norm[l], hg_lb_logits, hg_out_norm[l],
                   w_mem_kv[l], mem_q_norm[l], mem_k_norm[l], mla_out_norm[l], mem_out_norm[l],
                   w_out[l], norm_ffn[l], w_gate[l], w_up[l], w_down[l])
    return x
```

```python
import functools

import jax
import jax.numpy as jnp
from jax import lax
from jax.experimental import pallas as pl
from jax.experimental.pallas import tpu as pltpu

F32 = jnp.float32
BF16 = jnp.bfloat16

EPS = 1e-6
N_HEADS = 4
D_NOPE = 128
D_ROPE = 64
D_QK = D_NOPE + D_ROPE
D_QK_PAD = 256
D_V = 128
Q_LORA = 384
KV_LORA = 256
ROPE_THETA = 10000.0
LOG2E = 1.4426950408889634
HG_D = 128
MEM_D = 128
WIDTH = N_HEADS * 128

LANE = 128
HG_CHUNK = 128
HG_SUB = 8
ONES_ROWS = 16

VMEM_LIMIT = 56 * 1024 * 1024


def _dot(a, b):
    return jnp.dot(a, b, preferred_element_type=F32)


def _dot_nt(a, b):
    return lax.dot_general(a, b, (((1,), (1,)), ((), ())), preferred_element_type=F32)


def _rms(x, g, width):
    ss = jnp.sum(x * x, axis=-1, keepdims=True)
    return x * lax.rsqrt(ss * (1.0 / width) + EPS) * g


def _sigmoid(x):
    return 1.0 / (1.0 + jnp.exp(-x))


def _mem_kv_kernel(mem_ref, g_ref, w_ref, kn_ref, k_out, v_out):
    m = mem_ref[0].astype(F32)
    mh = _rms(m, g_ref[...], m.shape[-1]).astype(BF16)
    kv = _dot(mh, w_ref[...])
    for h in range(N_HEADS):
        kh = kv[:, h * MEM_D:(h + 1) * MEM_D]
        k_out[0, :, h * MEM_D:(h + 1) * MEM_D] = _rms(kh, kn_ref[...], MEM_D).astype(BF16)
    v_out[0] = kv[:, WIDTH:].astype(BF16)


def _in_proj_kernel(x_ref, pos_ref, invf_ref, phase_ref, gmix_ref,
                    w_cqkr_ref, w_ckv_ref, w_hq_ref, w_hf_ref, w_hi_ref, w_hg_ref, w_mq_ref,
                    gqa_ref, w_uq_ref, gkva_ref, w_uk_ref, w_uv_ref,
                    gq_nope_ref, gq_r1_ref, gq_r2_ref, gk_nope_ref, gk_r1_ref, gk_r2_ref,
                    gmq_ref, kmem_ref, vmem_ref, gmo_ref,
                    q_out, k_out, vt_out, hq_out, hf_out, hi_out, hg_out, ymem_out):
    x = x_ref[...].astype(F32)
    h = _rms(x, gmix_ref[...], x.shape[-1]).astype(BF16)

    big = _dot(h, w_cqkr_ref[...])
    ckv = _dot(h, w_ckv_ref[...])
    hq = _dot(h, w_hq_ref[...])
    cqn = _rms(big[:, :Q_LORA], gqa_ref[...], Q_LORA).astype(BF16)
    ckvn = _rms(ckv, gkva_ref[...], KV_LORA).astype(BF16)
    qa = _dot(cqn, w_uq_ref[...])
    kn = _dot(ckvn, w_uk_ref[...])
    vt = _dot(ckvn, w_uv_ref[...]).T
    mq = _dot(h, w_mq_ref[...])
    hf_out[...] = _dot(h, w_hf_ref[...])

    posf = pos_ref[0].astype(F32)
    pos = jnp.concatenate([jnp.broadcast_to(posf[r:r + 1, :], (LANE, LANE)).T
                           for r in range(posf.shape[0])], axis=0)
    cs = jnp.cos(pos * invf_ref[...] + phase_ref[...])
    sc = pltpu.roll(cs, D_ROPE, axis=1)

    def rotary(tile, g1_ref, g2_ref):
        return tile * (g1_ref[...] * cs) + pltpu.roll(tile, D_ROPE, axis=1) * (g2_ref[...] * sc)

    q_scale = LOG2E * D_QK ** -0.5
    for hd in range(N_HEADS):
        nope = qa[:, hd * D_QK_PAD:hd * D_QK_PAD + D_NOPE]
        tile = qa[:, hd * D_QK_PAD + D_NOPE:(hd + 1) * D_QK_PAD]
        ss = jnp.sum(nope * nope, axis=-1, keepdims=True) + 0.5 * jnp.sum(tile * tile, axis=-1, keepdims=True)
        rinv = lax.rsqrt(ss * (1.0 / D_QK) + EPS) * q_scale
        q_out[:, hd * D_QK_PAD:hd * D_QK_PAD + D_NOPE] = (nope * gq_nope_ref[...] * rinv).astype(BF16)
        q_out[:, hd * D_QK_PAD + D_NOPE:(hd + 1) * D_QK_PAD] = (
            rotary(tile, gq_r1_ref, gq_r2_ref) * rinv).astype(BF16)

    vt_out[0, :, 0] = vt.reshape(N_HEADS, D_V, vt.shape[-1]).astype(BF16)
    ktile = big[:, Q_LORA:]
    ss_r = 0.5 * jnp.sum(ktile * ktile, axis=-1, keepdims=True)
    rot = rotary(ktile, gk_r1_ref, gk_r2_ref)
    for hd in range(N_HEADS):
        a = kn[:, hd * D_NOPE:(hd + 1) * D_NOPE]
        ss = jnp.sum(a * a, axis=-1, keepdims=True) + ss_r
        rinv = lax.rsqrt(ss * (1.0 / D_QK) + EPS)
        k_out[:, hd * D_QK_PAD:hd * D_QK_PAD + D_NOPE] = (a * gk_nope_ref[...] * rinv).astype(BF16)
        k_out[:, hd * D_QK_PAD + D_NOPE:(hd + 1) * D_QK_PAD] = (rot * rinv).astype(BF16)

    ss_ = []
    for hd in range(N_HEADS):
        sl = slice(hd * MEM_D, (hd + 1) * MEM_D)
        qh = (_rms(mq[:, sl], gmq_ref[...], MEM_D) * (MEM_D ** -0.5)).astype(BF16)
        ss_.append(_dot_nt(qh, kmem_ref[0, :, sl]))
    hi_out[...] = _dot(h, w_hi_ref[...]).astype(BF16)
    hq_out[...] = (hq * _sigmoid(hq) * (HG_D ** -0.5)).astype(BF16)
    ys = []
    for hd in range(N_HEADS):
        sl = slice(hd * MEM_D, (hd + 1) * MEM_D)
        s = ss_[hd]
        p = jnp.exp(s - jnp.max(s, axis=-1, keepdims=True))
        l = jnp.sum(p, axis=-1, keepdims=True)
        ys.append(_dot(p.astype(BF16), vmem_ref[0, :, sl]) / l)
    hg = _dot(h, w_hg_ref[...])
    hg_out[...] = (hg * _sigmoid(hg)).astype(BF16)
    y = jnp.concatenate(ys, axis=-1)
    ymem_out[...] = _rms(y, gmo_ref[...], WIDTH).astype(BF16)


def _attn_kernel(q_ref, k_ref, vt_ref, o_ref, m_sc, acc_sc, s0_sc, s1_sc, cm0_sc, cm1_sc,
                 *, bq, bk, heads):
    qi = pl.program_id(2)
    s_bufs = (s0_sc, s1_sc)
    cm_bufs = (cm0_sc, cm1_sc)
    m_sc[...] = jnp.full(m_sc.shape, -jnp.inf, F32)
    acc_sc[...] = jnp.zeros(acc_sc.shape, F32)

    def scores(t, slot, hd):
        r0 = t * bk if isinstance(t, int) else pl.multiple_of(t * bk, bk)
        q = q_ref[0, :, hd * D_QK_PAD:(hd + 1) * D_QK_PAD]
        k = k_ref[0, pl.ds(r0, bk), hd * D_QK_PAD:(hd + 1) * D_QK_PAD]
        s = _dot_nt(k, q)
        s_bufs[slot][hd] = s
        cm_bufs[slot][hd] = jnp.max(s, axis=0, keepdims=True)

    def accumulate(t, slot, hd, key_offset=None):
        s = s_bufs[slot][hd]
        if key_offset is not None:
            kv = lax.broadcasted_iota(jnp.int32, (bk, bq), 0) + key_offset
            qq = lax.broadcasted_iota(jnp.int32, (bk, bq), 1)
            s = jnp.where(kv <= qq, s, -jnp.inf)
            cm = jnp.max(s, axis=0, keepdims=True)
        else:
            cm = cm_bufs[slot][hd]
        m_prev = m_sc[hd]
        m_new = jnp.maximum(m_prev, cm)
        p = jnp.exp2(s - m_new)
        alpha = jnp.exp2(m_prev - m_new)
        vt1 = jnp.concatenate([vt_ref[0, hd, t], jnp.ones((ONES_ROWS, bk), BF16)], axis=0)
        acc_sc[hd] = alpha * acc_sc[hd] + _dot(vt1, p.astype(BF16))
        m_sc[hd] = m_new

    for hd in range(heads):
        scores(0, 0, hd)

    def advance(t, slot, key_offset=None):
        for hd in range(heads):
            scores(t + 1, 1 - slot, hd)
            accumulate(t, slot, hd, key_offset)

    def body(jj, carry):
        advance(2 * jj, 0)
        advance(2 * jj + 1, 1)
        return carry

    lax.fori_loop(0, qi, body, 0)
    advance(2 * qi, 0, key_offset=0)
    for hd in range(heads):
        accumulate(2 * qi + 1, 1, hd, key_offset=bk)

    for hd in range(heads):
        o = (acc_sc[hd, :D_V] / acc_sc[hd, D_V:D_V + 1]).T
        o_ref[0, :, hd * D_V:(hd + 1) * D_V] = o.astype(o_ref.dtype)


def _pair_reference(b, m):
    c = b.shape[0]
    n2 = c // (2 * m)
    br = b.reshape(n2, 2 * m, b.shape[1])
    last = br[:, m - 1:m, :]
    return jnp.broadcast_to(last, br.shape).reshape(b.shape)


def _hgrn_kernel(hq_ref, hf_ref, hi_ref, hg_ref, lbl_ref, gain_ref, o_ref, st_ref, b_sc, *, n_chunks, layer):
    C = HG_CHUNK

    @pl.when(pl.program_id(1) == 0)
    def _():
        st_ref[...] = jnp.zeros(st_ref.shape, F32)

    lg = lbl_ref[...].astype(F32)
    e = jnp.exp(lg - jnp.max(lg, axis=0, keepdims=True))
    lb = jnp.sum(e[:layer + 1], axis=0, keepdims=True) / jnp.sum(e, axis=0, keepdims=True)

    row = lax.broadcasted_iota(jnp.int32, (C, C), 0)
    col = lax.broadcasted_iota(jnp.int32, (C, C), 1)
    tri = (col <= row).astype(BF16)
    diag_mask = (((row ^ col) & ~(HG_SUB - 1)) | jnp.where(col <= row, 0, 1)) == 0
    sub_keep = [jnp.where((row & (HG_SUB - 1)) == s_off, 1.0, 0.0).astype(BF16) for s_off in range(HG_SUB)]
    levels = []
    m = HG_SUB
    while m < C:
        bad = ((row ^ col) & ~(2 * m - 1)) | ((row & m) ^ m) | (col & m)
        levels.append((m, bad == 0))
        m *= 2

    def chunk(c):
        r0 = c * C
        fr = hf_ref[0, pl.ds(r0, C), :]
        f = lb + (1.0 - lb) * _sigmoid(fr)
        logf = jnp.log(f) * LOG2E
        kk_all = 1.0 - f
        t0 = logf.astype(BF16)
        r1 = logf - t0.astype(F32)
        t1 = r1.astype(BF16)
        t2 = (r1 - t1.astype(F32)).astype(BF16)
        b_all = _dot(tri, t0) + _dot(tri, t1) + _dot(tri, t2)
        q_all = hq_ref[0, pl.ds(r0, C), :].astype(F32)
        v_all = hi_ref[0, pl.ds(r0, C), :]
        g_all = hg_ref[0, pl.ds(r0, C), :].astype(F32)
        b_sc[c] = b_all
        for hd in range(N_HEADS):
            sl = slice(hd * HG_D, (hd + 1) * HG_D)
            b = b_all[:, sl]
            q = q_all[:, sl]
            kk = kk_all[:, sl]
            v = v_all[:, sl]
            kb = kk.astype(BF16)

            ms = []
            ks = []
            for s_off in range(HG_SUB):
                bs = jnp.concatenate(
                    [jnp.broadcast_to(b_sc[c, i * HG_SUB + s_off:i * HG_SUB + s_off + 1, sl], (HG_SUB, HG_D))
                     for i in range(C // HG_SUB)], axis=0)
                ms.append((q * jnp.exp2(jnp.minimum(b - bs, 0.0))).astype(BF16))
                ks.append(kb * sub_keep[s_off])
            ad = _dot_nt(jnp.concatenate(ms, axis=1), jnp.concatenate(ks, axis=1))
            a = jnp.where(diag_mask, ad, 0.0)
            for m_blk, mask in levels:
                d = b - _pair_reference(b, m_blk)
                qe = (q * jnp.exp2(d)).astype(BF16)
                ke = (kk * jnp.exp2(-d)).astype(BF16)
                a = jnp.where(mask, _dot_nt(qe, ke), a)

            st = st_ref[hd]
            o = _dot(a.astype(BF16), v) + _dot_nt((q * jnp.exp2(b)).astype(BF16), st.astype(BF16))
            b_last = b[C - 1:C, :]
            kd = (kk * jnp.exp2(b_last - b)).astype(BF16)
            vt = v.astype(F32).T.astype(BF16)
            st_ref[hd] = st * jnp.exp2(b_last) + _dot(vt, kd)

            on = _rms(o, gain_ref[:, sl], HG_D)
            o_ref[0, pl.ds(r0, C), sl] = (on * g_all[:, sl]).astype(o_ref.dtype)

    for c in range(n_chunks):
        chunk(c)


def _out_ffn_kernel(x_ref, ymla_ref, yhg_ref, ymem_ref, gmla_ref, w_out_ref, gffn_ref,
                    w_gate_ref, w_up_ref, w_down_ref, o_ref):
    x = x_ref[...].astype(F32)
    ymla = _rms(ymla_ref[...].astype(F32), gmla_ref[...], WIDTH).astype(BF16)
    mix = (_dot(ymla, w_out_ref[0:WIDTH, :])
           + _dot(yhg_ref[...], w_out_ref[WIDTH:2 * WIDTH, :])
           + _dot(ymem_ref[...], w_out_ref[2 * WIDTH:3 * WIDTH, :]))
    x1 = x + mix
    h2 = _rms(x1, gffn_ref[...], x1.shape[-1]).astype(BF16)
    g = _dot(h2, w_gate_ref[...])
    u = _dot(h2, w_up_ref[...])
    act = (g * _sigmoid(g) * u).astype(BF16)
    o_ref[...] = (x1 + _dot(act, w_down_ref[...])).astype(o_ref.dtype)


def _full(shape):
    nd = len(shape)
    return pl.BlockSpec(shape, lambda *_: (0,) * nd)


def _params(sem):
    return pltpu.CompilerParams(dimension_semantics=sem, vmem_limit_bytes=VMEM_LIMIT)


def _row(v):
    return v.reshape(1, -1).astype(F32)


def _layer(x, mem, positions, layer, norm_mix, norm_mem, w_in, q_a_norm, w_uq, kv_a_norm, w_ukv,
           mla_q_norm, mla_k_norm, hg_lb_logits, hg_out_norm, w_mem_kv, mem_q_norm, mem_k_norm,
           mla_out_norm, mem_out_norm, w_out, norm_ffn, w_gate, w_up, w_down):
    B, S, D = x.shape
    M = mem.shape[1]
    T = B * S
    half = D_ROPE // 2
    H = N_HEADS

    sizes = (Q_LORA, KV_LORA, D_ROPE, WIDTH, WIDTH, WIDTH, WIDTH, WIDTH)
    offs = [0]
    for sz in sizes:
        offs.append(offs[-1] + sz)
    w_cq, w_ckv, w_kr, w_hq, w_hf, w_hi, w_hg, w_mq = [
        w_in[:, offs[j]:offs[j + 1]] for j in range(len(sizes))]
    w_cqkr = jnp.concatenate([w_cq, w_kr, -w_kr[:, half:], w_kr[:, :half]], axis=1)
    uq = w_uq.reshape(Q_LORA, H, D_QK)
    uq2 = jnp.concatenate([uq, -uq[:, :, D_NOPE + half:], uq[:, :, D_NOPE:D_NOPE + half]], axis=2)
    ukv = w_ukv.reshape(KV_LORA, H, D_NOPE + D_V)
    w_uk = ukv[:, :, :D_NOPE].reshape(KV_LORA, H * D_NOPE)
    w_uv = ukv[:, :, D_NOPE:].reshape(KV_LORA, H * D_V)
    bf = lambda w: w.astype(BF16)

    pad = jnp.zeros((LANE - D_ROPE,), F32)

    def rotary_gains(g):
        g = g.astype(F32)
        g1 = jnp.concatenate([g[D_NOPE:], pad]).reshape(1, LANE)
        g2 = jnp.concatenate([g[D_NOPE + half:], g[D_NOPE:D_NOPE + half], pad]).reshape(1, LANE)
        return g[:D_NOPE].reshape(1, D_NOPE), g1, g2

    gq_nope, gq_r1, gq_r2 = rotary_gains(mla_q_norm)
    gk_nope, gk_r1, gk_r2 = rotary_gains(mla_k_norm)
    inv_freq = jnp.power(ROPE_THETA, -jnp.arange(half, dtype=F32) / half)
    invf = jnp.tile(inv_freq, LANE // half).reshape(1, LANE)
    phase = jnp.concatenate([jnp.zeros((D_ROPE,), F32), jnp.full((D_ROPE,), -jnp.pi / 2, F32)]).reshape(1, LANE)

    kmem, vmem = pl.pallas_call(
        _mem_kv_kernel,
        grid=(B,),
        in_specs=[pl.BlockSpec((1, M, D), lambda b: (b, 0, 0)),
                  _full((1, D)), _full((D, 2 * WIDTH)), _full((1, MEM_D))],
        out_specs=[pl.BlockSpec((1, M, WIDTH), lambda b: (b, 0, 0))] * 2,
        out_shape=[jax.ShapeDtypeStruct((B, M, WIDTH), BF16)] * 2,
        compiler_params=_params(("arbitrary",)),
        name="mem_kv",
    )(mem, _row(norm_mem), bf(w_mem_kv), _row(mem_k_norm))

    tm = min(512, S)
    assert S % tm == 0
    steps_per_batch = S // tm
    x2 = x.reshape(T, D)
    assert tm % LANE == 0
    pos2 = positions.reshape(T // tm, tm // LANE, LANE).astype(jnp.int32)
    pos_spec = pl.BlockSpec((1, tm // LANE, LANE), lambda i: (i, 0, 0))
    tok = lambda w: pl.BlockSpec((tm, w), lambda i: (i, 0))
    weights = [bf(w_cqkr), bf(w_ckv), bf(w_hq), bf(w_hf), bf(w_hi), bf(w_hg), bf(w_mq)]
    rest = [_row(q_a_norm), bf(uq2.reshape(Q_LORA, H * D_QK_PAD)), _row(kv_a_norm), bf(w_uk), bf(w_uv),
            gq_nope, gq_r1, gq_r2, gk_nope, gk_r1, gk_r2, _row(mem_q_norm)]
    mem_spec = pl.BlockSpec((1, M, WIDTH), lambda i: (i // steps_per_batch, 0, 0))
    vt_spec = pl.BlockSpec((1, H, 1, D_V, tm),
                           lambda i: (i // steps_per_batch, 0, i % steps_per_batch, 0, 0))
    q_all, k_all, vt_all, hq, hf, hi, hg, ymem = pl.pallas_call(
        _in_proj_kernel,
        grid=(T // tm,),
        in_specs=([tok(D), pos_spec, _full((1, LANE)), _full((1, LANE)), _full((1, D))]
                  + [_full(w.shape) for w in weights] + [_full(r.shape) for r in rest]
                  + [mem_spec, mem_spec, _full((1, WIDTH))]),
        out_specs=[tok(H * D_QK_PAD), tok(H * D_QK_PAD), vt_spec] + [tok(WIDTH)] * 5,
        out_shape=[jax.ShapeDtypeStruct((T, H * D_QK_PAD), BF16)] * 2
        + [jax.ShapeDtypeStruct((B, H, steps_per_batch, D_V, tm), BF16)]
        + [jax.ShapeDtypeStruct((T, WIDTH), dt) for dt in (BF16, F32, BF16, BF16, BF16)],
        compiler_params=_params(("arbitrary",)),
        name="in_proj",
    )(x2, pos2, invf, phase, _row(norm_mix), *weights, *rest, kmem, vmem, _row(mem_out_norm))

    bk = tm
    bq = 2 * bk
    assert S % bq == 0
    hpb = 2
    y_mla = pl.pallas_call(
        functools.partial(_attn_kernel, bq=bq, bk=bk, heads=hpb),
        grid=(B, H // hpb, S // bq),
        in_specs=[pl.BlockSpec((1, bq, hpb * D_QK_PAD), lambda b, h, i: (b, i, h)),
                  pl.BlockSpec((1, S, hpb * D_QK_PAD), lambda b, h, i: (b, 0, h)),
                  pl.BlockSpec((1, hpb, S // bk, D_V, bk), lambda b, h, i: (b, h, 0, 0, 0))],
        out_specs=pl.BlockSpec((1, bq, hpb * D_V), lambda b, h, i: (b, i, h)),
        out_shape=jax.ShapeDtypeStruct((B, S, H * D_V), BF16),
        scratch_shapes=[pltpu.VMEM((hpb, 1, bq), F32),
                        pltpu.VMEM((hpb, D_V + ONES_ROWS, bq), F32),
                        pltpu.VMEM((hpb, bk, bq), F32), pltpu.VMEM((hpb, bk, bq), F32),
                        pltpu.VMEM((hpb, 1, bq), F32), pltpu.VMEM((hpb, 1, bq), F32)],
        compiler_params=_params(("arbitrary", "arbitrary", "arbitrary")),
        name="mla_attn",
    )(q_all.reshape(B, S, H * D_QK_PAD), k_all.reshape(B, S, H * D_QK_PAD), vt_all)

    ts = min(1024, S)
    assert S % ts == 0 and ts % HG_CHUNK == 0
    seq = lambda: pl.BlockSpec((1, ts, WIDTH), lambda b, i: (b, i, 0))
    n_layers = hg_lb_logits.shape[0]
    y_hg = pl.pallas_call(
        functools.partial(_hgrn_kernel, n_chunks=ts // HG_CHUNK, layer=layer),
        grid=(B, S // ts),
        in_specs=[seq(), seq(), seq(), seq(), _full((n_layers, WIDTH)), _full((1, WIDTH))],
        out_specs=seq(),
        out_shape=jax.ShapeDtypeStruct((B, S, WIDTH), BF16),
        scratch_shapes=[pltpu.VMEM((N_HEADS, HG_D, HG_D), F32),
                        pltpu.VMEM((ts // HG_CHUNK, HG_CHUNK, WIDTH), F32)],
        compiler_params=_params(("arbitrary", "arbitrary")),
        name="hgrn",
    )(hq.reshape(B, S, WIDTH), hf.reshape(B, S, WIDTH), hi.reshape(B, S, WIDTH), hg.reshape(B, S, WIDTH),
      hg_lb_logits.astype(F32), _row(hg_out_norm))

    d_ff = w_gate.shape[1]
    once = lambda shape: pl.BlockSpec(shape, lambda i: (0, 0), pipeline_mode=pl.Buffered(1))
    out = pl.pallas_call(
        _out_ffn_kernel,
        grid=(T // tm,),
        in_specs=[tok(D), tok(WIDTH), tok(WIDTH), tok(WIDTH), _full((1, WIDTH)), once((3 * WIDTH, D)),
                  _full((1, D)), once((D, d_ff)), once((D, d_ff)), once((d_ff, D))],
        out_specs=tok(D),
        out_shape=jax.ShapeDtypeStruct((T, D), x.dtype),
        compiler_params=_params(("arbitrary",)),
        name="out_ffn",
    )(x2, y_mla.reshape(T, WIDTH), y_hg.reshape(T, WIDTH), ymem, _row(mla_out_norm), bf(w_out),
      _row(norm_ffn), bf(w_gate), bf(w_up), bf(w_down))
    return out.reshape(B, S, D)


def kernel(x, mem, positions, norm_mix, norm_mem, w_in, q_a_norm, w_uq, kv_a_norm, w_ukv, mla_q_norm, mla_k_norm, hg_lb_logits, hg_out_norm, w_mem_kv, mem_q_norm, mem_k_norm, mla_out_norm, mem_out_norm, w_out, norm_ffn, w_gate, w_up, w_down):
    depth = w_in.shape[0]
    for l in range(depth):
        x = _layer(x, mem, positions, l, norm_mix[l], norm_mem[l], w_in[l], q_a_norm[l], w_uq[l],
                   kv_a_norm[l], w_ukv[l], mla_q_norm[l], mla_k_norm[l], hg_lb_logits, hg_out_norm[l],
                   w_mem_kv[l], mem_q_norm[l], mem_k_norm[l], mla_out_norm[l], mem_out_norm[l],
                   w_out[l], norm_ffn[l], w_gate[l], w_up[l], w_down[l])
    return x
```

```python
import functools

import jax
import jax.numpy as jnp
from jax import lax
from jax.experimental import pallas as pl
from jax.experimental.pallas import tpu as pltpu

F32 = jnp.float32
BF16 = jnp.bfloat16

EPS = 1e-6
N_HEADS = 4
D_NOPE = 128
D_ROPE = 64
D_QK = D_NOPE + D_ROPE
D_QK_PAD = 256
D_V = 128
Q_LORA = 384
KV_LORA = 256
ROPE_THETA = 10000.0
LOG2E = 1.4426950408889634
HG_D = 128
MEM_D = 128
WIDTH = N_HEADS * 128

LANE = 128
HG_CHUNK = 128
HG_SUB = 8
ONES_ROWS = 16

VMEM_LIMIT = 56 * 1024 * 1024


def _dot(a, b):
    return jnp.dot(a, b, preferred_element_type=F32)


def _dot_nt(a, b):
    return lax.dot_general(a, b, (((1,), (1,)), ((), ())), preferred_element_type=F32)


def _rms(x, g, width):
    ss = jnp.sum(x * x, axis=-1, keepdims=True)
    return x * lax.rsqrt(ss * (1.0 / width) + EPS) * g


def _sigmoid(x):
    return 1.0 / (1.0 + jnp.exp(-x))


def _mem_kv_kernel(mem_ref, g_ref, w_ref, kn_ref, k_out, v_out):
    m = mem_ref[0].astype(F32)
    mh = _rms(m, g_ref[...], m.shape[-1]).astype(BF16)
    kv = _dot(mh, w_ref[...])
    for h in range(N_HEADS):
        kh = kv[:, h * MEM_D:(h + 1) * MEM_D]
        k_out[0, :, h * MEM_D:(h + 1) * MEM_D] = _rms(kh, kn_ref[...], MEM_D).astype(BF16)
    v_out[0] = kv[:, WIDTH:].astype(BF16)


def _in_proj_kernel(x_ref, pos_ref, pos_next_ref, invf_ref, phase_ref, sign_ref, gmix_ref,
                    w_cqkr_ref, w_ckv_ref, w_hq_ref, w_hf_ref, w_hi_ref, w_hg_ref, w_mq_ref,
                    gqa_ref, w_uq_ref, gkva_ref, w_uk_ref, w_uv_ref,
                    gq_nope_ref, gq_r1_ref, gq_r2_ref, gk_nope_ref, gk_r1_ref, gk_r2_ref,
                    gmq_ref, kmem_ref, vmem_ref, gmo_ref,
                    q_out, k_out, vt_out, hq_out, hf_out, hi_out, hg_out, ymem_out,
                    cs_sc, tc_sc, ts_sc):
    tm = x_ref.shape[0]

    def fill_cs(p_ref):
        posi = p_ref[0]
        first = jnp.broadcast_to(posi[0:1, 0:1], posi.shape)
        step = (lax.broadcasted_iota(jnp.int32, posi.shape, 0) * LANE
                + lax.broadcasted_iota(jnp.int32, posi.shape, 1))
        gap = jnp.max(jnp.abs((posi - first - step).astype(F32)))

        @pl.when(gap == 0.0)
        def _():
            p0 = jnp.broadcast_to(posi[0:1, 0:1], (8, LANE)).astype(F32)
            a = jnp.cos(p0 * invf_ref[...] + phase_ref[...])
            b = pltpu.roll(a, D_ROPE, axis=1) * sign_ref[...]
            cs_sc[...] = a[0:1] * tc_sc[...] + b[0:1] * ts_sc[...]

        @pl.when(gap != 0.0)
        def _():
            posf = posi.astype(F32)
            pos = jnp.concatenate([jnp.broadcast_to(posf[r:r + 1, :], (LANE, LANE)).T
                                   for r in range(posf.shape[0])], axis=0)
            cs_sc[...] = jnp.cos(pos * invf_ref[...] + phase_ref[...])

    @pl.when(pl.program_id(0) == 0)
    def _():
        tf = lax.broadcasted_iota(jnp.int32, (tm, LANE), 0).astype(F32) * invf_ref[...]
        tc_sc[...] = jnp.cos(tf)
        ts_sc[...] = jnp.sin(tf)
        fill_cs(pos_ref)

    x = x_ref[...].astype(F32)
    h = _rms(x, gmix_ref[...], x.shape[-1]).astype(BF16)

    big = _dot(h, w_cqkr_ref[...])
    ckv = _dot(h, w_ckv_ref[...])
    hq = _dot(h, w_hq_ref[...])
    cqn = _rms(big[:, :Q_LORA], gqa_ref[...], Q_LORA).astype(BF16)
    ckvn = _rms(ckv, gkva_ref[...], KV_LORA).astype(BF16)
    qa = _dot(cqn, w_uq_ref[...])
    kn = _dot(ckvn, w_uk_ref[...])
    vt = _dot(ckvn, w_uv_ref[...]).T
    mq = _dot(h, w_mq_ref[...])
    hf_out[...] = _dot(h, w_hf_ref[...])

    cs = cs_sc[...]
    sc = pltpu.roll(cs, D_ROPE, axis=1)

    def rotary(tile, g1_ref, g2_ref):
        return tile * (g1_ref[...] * cs) + pltpu.roll(tile, D_ROPE, axis=1) * (g2_ref[...] * sc)

    q_scale = LOG2E * D_QK ** -0.5
    for hd in range(N_HEADS):
        nope = qa[:, hd * D_QK_PAD:hd * D_QK_PAD + D_NOPE]
        tile = qa[:, hd * D_QK_PAD + D_NOPE:(hd + 1) * D_QK_PAD]
        ss = jnp.sum(nope * nope, axis=-1, keepdims=True) + 0.5 * jnp.sum(tile * tile, axis=-1, keepdims=True)
        rinv = lax.rsqrt(ss * (1.0 / D_QK) + EPS) * q_scale
        q_out[:, hd * D_QK_PAD:hd * D_QK_PAD + D_NOPE] = (nope * gq_nope_ref[...] * rinv).astype(BF16)
        q_out[:, hd * D_QK_PAD + D_NOPE:(hd + 1) * D_QK_PAD] = (
            rotary(tile, gq_r1_ref, gq_r2_ref) * rinv).astype(BF16)

    vt_out[0, :, 0] = vt.reshape(N_HEADS, D_V, vt.shape[-1]).astype(BF16)
    ktile = big[:, Q_LORA:]
    ss_r = 0.5 * jnp.sum(ktile * ktile, axis=-1, keepdims=True)
    rot = rotary(ktile, gk_r1_ref, gk_r2_ref)
    for hd in range(N_HEADS):
        a = kn[:, hd * D_NOPE:(hd + 1) * D_NOPE]
        ss = jnp.sum(a * a, axis=-1, keepdims=True) + ss_r
        rinv = lax.rsqrt(ss * (1.0 / D_QK) + EPS)
        k_out[:, hd * D_QK_PAD:hd * D_QK_PAD + D_NOPE] = (a * gk_nope_ref[...] * rinv).astype(BF16)
        k_out[:, hd * D_QK_PAD + D_NOPE:(hd + 1) * D_QK_PAD] = (rot * rinv).astype(BF16)

    ss_ = []
    for hd in range(N_HEADS):
        sl = slice(hd * MEM_D, (hd + 1) * MEM_D)
        qh = (_rms(mq[:, sl], gmq_ref[...], MEM_D) * (MEM_D ** -0.5)).astype(BF16)
        ss_.append(_dot_nt(qh, kmem_ref[0, :, sl]))
    hi_out[...] = _dot(h, w_hi_ref[...]).astype(BF16)
    hq_out[...] = (hq * _sigmoid(hq) * (HG_D ** -0.5)).astype(BF16)
    ys = []
    for hd in range(N_HEADS):
        sl = slice(hd * MEM_D, (hd + 1) * MEM_D)
        s = ss_[hd]
        p = jnp.exp(s - jnp.max(s, axis=-1, keepdims=True))
        l = jnp.sum(p, axis=-1, keepdims=True)
        ys.append(_dot(p.astype(BF16), vmem_ref[0, :, sl]) / l)
    hg = _dot(h, w_hg_ref[...])
    hg_out[...] = (hg * _sigmoid(hg)).astype(BF16)
    y = jnp.concatenate(ys, axis=-1)
    ymem_out[...] = _rms(y, gmo_ref[...], WIDTH).astype(BF16)

    fill_cs(pos_next_ref)


def _attn_kernel(q_ref, k_ref, vt_ref, o_ref, m_sc, acc_sc, s0_sc, s1_sc, cm0_sc, cm1_sc,
                 *, bq, bk, heads):
    qi = pl.program_id(2)
    s_bufs = (s0_sc, s1_sc)
    cm_bufs = (cm0_sc, cm1_sc)
    m_sc[...] = jnp.full(m_sc.shape, -jnp.inf, F32)
    acc_sc[...] = jnp.zeros(acc_sc.shape, F32)

    def scores(t, slot, hd):
        r0 = t * bk if isinstance(t, int) else pl.multiple_of(t * bk, bk)
        q = q_ref[0, :, hd * D_QK_PAD:(hd + 1) * D_QK_PAD]
        k = k_ref[0, pl.ds(r0, bk), hd * D_QK_PAD:(hd + 1) * D_QK_PAD]
        s = _dot_nt(k, q)
        s_bufs[slot][hd] = s
        cm_bufs[slot][hd] = jnp.max(s, axis=0, keepdims=True)

    def accumulate(t, slot, hd, key_offset=None):
        s = s_bufs[slot][hd]
        if key_offset is not None:
            kv = lax.broadcasted_iota(jnp.int32, (bk, bq), 0) + key_offset
            qq = lax.broadcasted_iota(jnp.int32, (bk, bq), 1)
            s = jnp.where(kv <= qq, s, -jnp.inf)
            cm = jnp.max(s, axis=0, keepdims=True)
        else:
            cm = cm_bufs[slot][hd]
        m_prev = m_sc[hd]
        m_new = jnp.maximum(m_prev, cm)
        p = jnp.exp2(s - m_new)
        alpha = jnp.exp2(m_prev - m_new)
        vt1 = jnp.concatenate([vt_ref[0, hd, t], jnp.ones((ONES_ROWS, bk), BF16)], axis=0)
        acc_sc[hd] = alpha * acc_sc[hd] + _dot(vt1, p.astype(BF16))
        m_sc[hd] = m_new

    for hd in range(heads):
        scores(0, 0, hd)

    def advance(t, slot, key_offset=None):
        for hd in range(heads):
            scores(t + 1, 1 - slot, hd)
            accumulate(t, slot, hd, key_offset)

    def body(jj, carry):
        advance(2 * jj, 0)
        advance(2 * jj + 1, 1)
        return carry

    lax.fori_loop(0, qi, body, 0)
    advance(2 * qi, 0, key_offset=0)
    for hd in range(heads):
        accumulate(2 * qi + 1, 1, hd, key_offset=bk)

    for hd in range(heads):
        o = (acc_sc[hd, :D_V] / acc_sc[hd, D_V:D_V + 1]).T
        o_ref[0, :, hd * D_V:(hd + 1) * D_V] = o.astype(o_ref.dtype)


def _pair_reference(b, m):
    c = b.shape[0]
    n2 = c // (2 * m)
    br = b.reshape(n2, 2 * m, b.shape[1])
    last = br[:, m - 1:m, :]
    return jnp.broadcast_to(last, br.shape).reshape(b.shape)


def _hgrn_kernel(hq_ref, hf_ref, hi_ref, hg_ref, lbl_ref, gain_ref, o_ref, st_ref, b_sc, *, n_chunks, layer):
    C = HG_CHUNK

    @pl.when(pl.program_id(1) == 0)
    def _():
        st_ref[...] = jnp.zeros(st_ref.shape, F32)

    lg = lbl_ref[...].astype(F32)
    e = jnp.exp(lg - jnp.max(lg, axis=0, keepdims=True))
    lb = jnp.sum(e[:layer + 1], axis=0, keepdims=True) / jnp.sum(e, axis=0, keepdims=True)

    row = lax.broadcasted_iota(jnp.int32, (C, C), 0)
    col = lax.broadcasted_iota(jnp.int32, (C, C), 1)
    tri = (col <= row).astype(BF16)
    diag_mask = (((row ^ col) & ~(HG_SUB - 1)) | jnp.where(col <= row, 0, 1)) == 0
    sub_keep = [jnp.where((row & (HG_SUB - 1)) == s_off, 1.0, 0.0).astype(BF16) for s_off in range(HG_SUB)]
    levels = []
    m = HG_SUB
    while m < C:
        bad = ((row ^ col) & ~(2 * m - 1)) | ((row & m) ^ m) | (col & m)
        levels.append((m, bad == 0))
        m *= 2

    def chunk(c):
        r0 = c * C
        fr = hf_ref[0, pl.ds(r0, C), :]
        f = lb + (1.0 - lb) * _sigmoid(fr)
        logf = jnp.log(f) * LOG2E
        kk_all = 1.0 - f
        t0 = logf.astype(BF16)
        r1 = logf - t0.astype(F32)
        t1 = r1.astype(BF16)
        t2 = (r1 - t1.astype(F32)).astype(BF16)
        b_all = _dot(tri, t0) + _dot(tri, t1) + _dot(tri, t2)
        q_all = hq_ref[0, pl.ds(r0, C), :].astype(F32)
        v_all = hi_ref[0, pl.ds(r0, C), :]
        g_all = hg_ref[0, pl.ds(r0, C), :].astype(F32)
        b_sc[c] = b_all
        for hd in range(N_HEADS):
            sl = slice(hd * HG_D, (hd + 1) * HG_D)
            b = b_all[:, sl]
            q = q_all[:, sl]
            kk = kk_all[:, sl]
            v = v_all[:, sl]
            kb = kk.astype(BF16)

            ms = []
            ks = []
            for s_off in range(HG_SUB):
                bs = jnp.concatenate(
                    [jnp.broadcast_to(b_sc[c, i * HG_SUB + s_off:i * HG_SUB + s_off + 1, sl], (HG_SUB, HG_D))
                     for i in range(C // HG_SUB)], axis=0)
                ms.append((q * jnp.exp2(jnp.minimum(b - bs, 0.0))).astype(BF16))
                ks.append(kb * sub_keep[s_off])
            ad = _dot_nt(jnp.concatenate(ms, axis=1), jnp.concatenate(ks, axis=1))
            a = jnp.where(diag_mask, ad, 0.0)
            for m_blk, mask in levels:
                d = b - _pair_reference(b, m_blk)
                qe = (q * jnp.exp2(d)).astype(BF16)
                ke = (kk * jnp.exp2(-d)).astype(BF16)
                a = jnp.where(mask, _dot_nt(qe, ke), a)

            st = st_ref[hd]
            o = _dot(a.astype(BF16), v) + _dot_nt((q * jnp.exp2(b)).astype(BF16), st.astype(BF16))
            b_last = b[C - 1:C, :]
            kd = (kk * jnp.exp2(b_last - b)).astype(BF16)
            vt = v.astype(F32).T.astype(BF16)
            st_ref[hd] = st * jnp.exp2(b_last) + _dot(vt, kd)

            on = _rms(o, gain_ref[:, sl], HG_D)
            o_ref[0, pl.ds(r0, C), sl] = (on * g_all[:, sl]).astype(o_ref.dtype)

    for c in range(n_chunks):
        chunk(c)


def _out_ffn_kernel(x_ref, ymla_ref, yhg_ref, ymem_ref, gmla_ref, w_out_ref, gffn_ref,
                    w_gate_ref, w_up_ref, w_down_ref, o_ref):
    x = x_ref[...].astype(F32)
    ymla = _rms(ymla_ref[...].astype(F32), gmla_ref[...], WIDTH).astype(BF16)
    mix = (_dot(ymla, w_out_ref[0:WIDTH, :])
           + _dot(yhg_ref[...], w_out_ref[WIDTH:2 * WIDTH, :])
           + _dot(ymem_ref[...], w_out_ref[2 * WIDTH:3 * WIDTH, :]))
    x1 = x + mix
    h2 = _rms(x1, gffn_ref[...], x1.shape[-1]).astype(BF16)
    g = _dot(h2, w_gate_ref[...])
    u = _dot(h2, w_up_ref[...])
    act = (g * _sigmoid(g) * u).astype(BF16)
    o_ref[...] = (x1 + _dot(act, w_down_ref[...])).astype(o_ref.dtype)


def _full(shape):
    nd = len(shape)
    return pl.BlockSpec(shape, lambda *_: (0,) * nd)


def _params(sem):
    return pltpu.CompilerParams(dimension_semantics=sem, vmem_limit_bytes=VMEM_LIMIT)


def _row(v):
    return v.reshape(1, -1).astype(F32)


def _layer(x, mem, positions, layer, norm_mix, norm_mem, w_in, q_a_norm, w_uq, kv_a_norm, w_ukv,
           mla_q_norm, mla_k_norm, hg_lb_logits, hg_out_norm, w_mem_kv, mem_q_norm, mem_k_norm,
           mla_out_norm, mem_out_norm, w_out, norm_ffn, w_gate, w_up, w_down):
    B, S, D = x.shape
    M = mem.shape[1]
    T = B * S
    half = D_ROPE // 2
    H = N_HEADS

    sizes = (Q_LORA, KV_LORA, D_ROPE, WIDTH, WIDTH, WIDTH, WIDTH, WIDTH)
    offs = [0]
    for sz in sizes:
        offs.append(offs[-1] + sz)
    w_cq, w_ckv, w_kr, w_hq, w_hf, w_hi, w_hg, w_mq = [
        w_in[:, offs[j]:offs[j + 1]] for j in range(len(sizes))]
    w_cqkr = jnp.concatenate([w_cq, w_kr, -w_kr[:, half:], w_kr[:, :half]], axis=1)
    uq = w_uq.reshape(Q_LORA, H, D_QK)
    uq2 = jnp.concatenate([uq, -uq[:, :, D_NOPE + half:], uq[:, :, D_NOPE:D_NOPE + half]], axis=2)
    ukv = w_ukv.reshape(KV_LORA, H, D_NOPE + D_V)
    w_uk = ukv[:, :, :D_NOPE].reshape(KV_LORA, H * D_NOPE)
    w_uv = ukv[:, :, D_NOPE:].reshape(KV_LORA, H * D_V)
    bf = lambda w: w.astype(BF16)

    pad = jnp.zeros((LANE - D_ROPE,), F32)

    def rotary_gains(g):
        g = g.astype(F32)
        g1 = jnp.concatenate([g[D_NOPE:], pad]).reshape(1, LANE)
        g2 = jnp.concatenate([g[D_NOPE + half:], g[D_NOPE:D_NOPE + half], pad]).reshape(1, LANE)
        return g[:D_NOPE].reshape(1, D_NOPE), g1, g2

    gq_nope, gq_r1, gq_r2 = rotary_gains(mla_q_norm)
    gk_nope, gk_r1, gk_r2 = rotary_gains(mla_k_norm)
    inv_freq = jnp.power(ROPE_THETA, -jnp.arange(half, dtype=F32) / half)
    invf = jnp.tile(inv_freq, LANE // half).reshape(1, LANE)
    phase = jnp.concatenate([jnp.zeros((D_ROPE,), F32), jnp.full((D_ROPE,), -jnp.pi / 2, F32)]).reshape(1, LANE)

    kmem, vmem = pl.pallas_call(
        _mem_kv_kernel,
        grid=(B,),
        in_specs=[pl.BlockSpec((1, M, D), lambda b: (b, 0, 0)),
                  _full((1, D)), _full((D, 2 * WIDTH)), _full((1, MEM_D))],
        out_specs=[pl.BlockSpec((1, M, WIDTH), lambda b: (b, 0, 0))] * 2,
        out_shape=[jax.ShapeDtypeStruct((B, M, WIDTH), BF16)] * 2,
        compiler_params=_params(("arbitrary",)),
        name="mem_kv",
    )(mem, _row(norm_mem), bf(w_mem_kv), _row(mem_k_norm))

    tm = min(512, S)
    assert S % tm == 0
    steps_per_batch = S // tm
    x2 = x.reshape(T, D)
    assert tm % LANE == 0
    pos2 = positions.reshape(T // tm, tm // LANE, LANE).astype(jnp.int32)
    pos_spec = pl.BlockSpec((1, tm // LANE, LANE), lambda i: (i, 0, 0))
    n_tok_steps = T // tm
    pos_next_spec = pl.BlockSpec((1, tm // LANE, LANE), lambda i: (jnp.minimum(i + 1, n_tok_steps - 1), 0, 0))
    sign = jnp.concatenate([jnp.full((D_ROPE,), -1.0, F32), jnp.ones((D_ROPE,), F32)]).reshape(1, LANE)
    tok = lambda w: pl.BlockSpec((tm, w), lambda i: (i, 0))
    weights = [bf(w_cqkr), bf(w_ckv), bf(w_hq), bf(w_hf), bf(w_hi), bf(w_hg), bf(w_mq)]
    rest = [_row(q_a_norm), bf(uq2.reshape(Q_LORA, H * D_QK_PAD)), _row(kv_a_norm), bf(w_uk), bf(w_uv),
            gq_nope, gq_r1, gq_r2, gk_nope, gk_r1, gk_r2, _row(mem_q_norm)]
    mem_spec = pl.BlockSpec((1, M, WIDTH), lambda i: (i // steps_per_batch, 0, 0))
    vt_spec = pl.BlockSpec((1, H, 1, D_V, tm),
                           lambda i: (i // steps_per_batch, 0, i % steps_per_batch, 0, 0))
    q_all, k_all, vt_all, hq, hf, hi, hg, ymem = pl.pallas_call(
        _in_proj_kernel,
        grid=(T // tm,),
        in_specs=([tok(D), pos_spec, pos_next_spec, _full((1, LANE)), _full((1, LANE)), _full((1, LANE)),
                   _full((1, D))]
                  + [_full(w.shape) for w in weights] + [_full(r.shape) for r in rest]
                  + [mem_spec, mem_spec, _full((1, WIDTH))]),
        out_specs=[tok(H * D_QK_PAD), tok(H * D_QK_PAD), vt_spec] + [tok(WIDTH)] * 5,
        out_shape=[jax.ShapeDtypeStruct((T, H * D_QK_PAD), BF16)] * 2
        + [jax.ShapeDtypeStruct((B, H, steps_per_batch, D_V, tm), BF16)]
        + [jax.ShapeDtypeStruct((T, WIDTH), dt) for dt in (BF16, F32, BF16, BF16, BF16)],
        scratch_shapes=[pltpu.VMEM((tm, LANE), F32)] * 3,
        compiler_params=_params(("arbitrary",)),
        name="in_proj",
    )(x2, pos2, pos2, invf, phase, sign, _row(norm_mix), *weights, *rest, kmem, vmem, _row(mem_out_norm))

    bk = tm
    bq = 2 * bk
    assert S % bq == 0
    hpb = 2
    y_mla = pl.pallas_call(
        functools.partial(_attn_kernel, bq=bq, bk=bk, heads=hpb),
        grid=(B, H // hpb, S // bq),
        in_specs=[pl.BlockSpec((1, bq, hpb * D_QK_PAD), lambda b, h, i: (b, i, h)),
                  pl.BlockSpec((1, S, hpb * D_QK_PAD), lambda b, h, i: (b, 0, h)),
                  pl.BlockSpec((1, hpb, S // bk, D_V, bk), lambda b, h, i: (b, h, 0, 0, 0))],
        out_specs=pl.BlockSpec((1, bq, hpb * D_V), lambda b, h, i: (b, i, h)),
        out_shape=jax.ShapeDtypeStruct((B, S, H * D_V), BF16),
        scratch_shapes=[pltpu.VMEM((hpb, 1, bq), F32),
                        pltpu.VMEM((hpb, D_V + ONES_ROWS, bq), F32),
                        pltpu.VMEM((hpb, bk, bq), F32), pltpu.VMEM((hpb, bk, bq), F32),
                        pltpu.VMEM((hpb, 1, bq), F32), pltpu.VMEM((hpb, 1, bq), F32)],
        compiler_params=_params(("arbitrary", "arbitrary", "arbitrary")),
        name="mla_attn",
    )(q_all.reshape(B, S, H * D_QK_PAD), k_all.reshape(B, S, H * D_QK_PAD), vt_all)

    ts = min(1024, S)
    assert S % ts == 0 and ts % HG_CHUNK == 0
    seq = lambda: pl.BlockSpec((1, ts, WIDTH), lambda b, i: (b, i, 0))
    n_layers = hg_lb_logits.shape[0]
    y_hg = pl.pallas_call(
        functools.partial(_hgrn_kernel, n_chunks=ts // HG_CHUNK, layer=layer),
        grid=(B, S // ts),
        in_specs=[seq(), seq(), seq(), seq(), _full((n_layers, WIDTH)), _full((1, WIDTH))],
        out_specs=seq(),
        out_shape=jax.ShapeDtypeStruct((B, S, WIDTH), BF16),
        scratch_shapes=[pltpu.VMEM((N_HEADS, HG_D, HG_D), F32),
                        pltpu.VMEM((ts // HG_CHUNK, HG_CHUNK, WIDTH), F32)],
        compiler_params=_params(("arbitrary", "arbitrary")),
        name="hgrn",
    )(hq.reshape(B, S, WIDTH), hf.reshape(B, S, WIDTH), hi.reshape(B, S, WIDTH), hg.reshape(B, S, WIDTH),
      hg_lb_logits.astype(F32), _row(hg_out_norm))

    d_ff = w_gate.shape[1]
    once = lambda shape: pl.BlockSpec(shape, lambda i: (0, 0), pipeline_mode=pl.Buffered(1))
    out = pl.pallas_call(
        _out_ffn_kernel,
        grid=(T // tm,),
        in_specs=[tok(D), tok(WIDTH), tok(WIDTH), tok(WIDTH), _full((1, WIDTH)), once((3 * WIDTH, D)),
                  _full((1, D)), once((D, d_ff)), once((D, d_ff)), once((d_ff, D))],
        out_specs=tok(D),
        out_shape=jax.ShapeDtypeStruct((T, D), x.dtype),
        compiler_params=_params(("arbitrary",)),
        name="out_ffn",
    )(x2, y_mla.reshape(T, WIDTH), y_hg.reshape(T, WIDTH), ymem, _row(mla_out_norm), bf(w_out),
      _row(norm_ffn), bf(w_gate), bf(w_up), bf(w_down))
    return out.reshape(B, S, D)


def kernel(x, mem, positions, norm_mix, norm_mem, w_in, q_a_norm, w_uq, kv_a_norm, w_ukv, mla_q_norm, mla_k_norm, hg_lb_logits, hg_out_norm, w_mem_kv, mem_q_norm, mem_k_norm, mla_out_norm, mem_out_norm, w_out, norm_ffn, w_gate, w_up, w_down):
    depth = w_in.shape[0]
    for l in range(depth):
        x = _layer(x, mem, positions, l, norm_mix[l], norm_mem[l], w_in[l], q_a_norm[l], w_uq[l],
                   kv_a_norm[l], w_ukv[l], mla_q_norm[l], mla_k_norm[l], hg_lb_logits, hg_out_norm[l],
                   w_mem_kv[l], mem_q_norm[l], mem_k_norm[l], mla_out_norm[l], mem_out_norm[l],
                   w_out[l], norm_ffn[l], w_gate[l], w_up[l], w_down[l])
    return x
```

```python
import functools

import jax
import jax.numpy as jnp
from jax import lax
from jax.experimental import pallas as pl
from jax.experimental.pallas import tpu as pltpu

F32 = jnp.float32
BF16 = jnp.bfloat16

EPS = 1e-6
N_HEADS = 4
D_NOPE = 128
D_ROPE = 64
D_QK = D_NOPE + D_ROPE
D_QK_PAD = 256
D_V = 128
Q_LORA = 384
KV_LORA = 256
ROPE_THETA = 10000.0
LOG2E = 1.4426950408889634
HG_D = 128
MEM_D = 128
WIDTH = N_HEADS * 128

LANE = 128
HG_CHUNK = 128
HG_SUB = 8
ONES_ROWS = 16

VMEM_LIMIT = 56 * 1024 * 1024


def _dot(a, b):
    return jnp.dot(a, b, preferred_element_type=F32)


def _dot_nt(a, b):
    return lax.dot_general(a, b, (((1,), (1,)), ((), ())), preferred_element_type=F32)


def _rms(x, g, width):
    ss = jnp.sum(x * x, axis=-1, keepdims=True)
    return x * lax.rsqrt(ss * (1.0 / width) + EPS) * g


def _sigmoid(x):
    return 1.0 / (1.0 + jnp.exp(-x))


def _mem_kv_kernel(mem_ref, g_ref, w_ref, kn_ref, k_out, v_out):
    m = mem_ref[0].astype(F32)
    mh = _rms(m, g_ref[...], m.shape[-1]).astype(BF16)
    kv = _dot(mh, w_ref[...])
    for h in range(N_HEADS):
        kh = kv[:, h * MEM_D:(h + 1) * MEM_D]
        k_out[0, :, h * MEM_D:(h + 1) * MEM_D] = _rms(kh, kn_ref[...], MEM_D).astype(BF16)
    v_out[0] = kv[:, WIDTH:].astype(BF16)


W_ALL = Q_LORA + 2 * D_ROPE + 5 * WIDTH + KV_LORA


def _w_in_kernel(w_ref, o_ref):
    w = w_ref[...]
    lo = Q_LORA + KV_LORA
    half = D_ROPE // 2
    o_ref[:, :Q_LORA] = w[:, :Q_LORA].astype(BF16)
    t = w[:, lo:lo + LANE]
    lane = lax.broadcasted_iota(jnp.int32, t.shape, 1)
    rot = jnp.where(lane < D_ROPE, t,
                    jnp.where(lane < D_ROPE + half, -pltpu.roll(t, half, axis=1),
                              pltpu.roll(t, LANE - half, axis=1)))
    o_ref[:, Q_LORA:Q_LORA + LANE] = rot.astype(BF16)
    o_ref[:, Q_LORA + LANE:Q_LORA + LANE + 5 * WIDTH] = w[:, lo + D_ROPE:].astype(BF16)
    o_ref[:, Q_LORA + LANE + 5 * WIDTH:] = w[:, Q_LORA:lo].astype(BF16)


def _in_proj_kernel(x_ref, pos_ref, pos_next_ref, invf_ref, phase_ref, sign_ref, gmix_ref,
                    w_cqkr_ref, w_ckv_ref, w_hq_ref, w_hf_ref, w_hi_ref, w_hg_ref, w_mq_ref,
                    gqa_ref, w_uq_ref, gkva_ref, w_uk_ref, w_uv_ref,
                    gq_nope_ref, gq_r1_ref, gq_r2_ref, gk_nope_ref, gk_r1_ref, gk_r2_ref,
                    gmq_ref, kmem_ref, vmem_ref, gmo_ref,
                    q_out, k_out, vt_out, hq_out, hf_out, hi_out, hg_out, ymem_out,
                    cs_sc, tc_sc, ts_sc):
    tm = x_ref.shape[0]

    def fill_cs(p_ref):
        posi = p_ref[0]
        first = jnp.broadcast_to(posi[0:1, 0:1], posi.shape)
        step = (lax.broadcasted_iota(jnp.int32, posi.shape, 0) * LANE
                + lax.broadcasted_iota(jnp.int32, posi.shape, 1))
        gap = jnp.max(jnp.abs((posi - first - step).astype(F32)))

        @pl.when(gap == 0.0)
        def _():
            p0 = jnp.broadcast_to(posi[0:1, 0:1], (8, LANE)).astype(F32)
            a = jnp.cos(p0 * invf_ref[...] + phase_ref[...])
            b = pltpu.roll(a, D_ROPE, axis=1) * sign_ref[...]
            cs_sc[...] = a[0:1] * tc_sc[...] + b[0:1] * ts_sc[...]

        @pl.when(gap != 0.0)
        def _():
            posf = posi.astype(F32)
            pos = jnp.concatenate([jnp.broadcast_to(posf[r:r + 1, :], (LANE, LANE)).T
                                   for r in range(posf.shape[0])], axis=0)
            cs_sc[...] = jnp.cos(pos * invf_ref[...] + phase_ref[...])

    @pl.when(pl.program_id(0) == 0)
    def _():
        tf = lax.broadcasted_iota(jnp.int32, (tm, LANE), 0).astype(F32) * invf_ref[...]
        tc_sc[...] = jnp.cos(tf)
        ts_sc[...] = jnp.sin(tf)
        fill_cs(pos_ref)

    x = x_ref[...].astype(F32)
    h = _rms(x, gmix_ref[...], x.shape[-1]).astype(BF16)

    big = _dot(h, w_cqkr_ref[...])
    ckv = _dot(h, w_ckv_ref[...])
    hq = _dot(h, w_hq_ref[...])
    cqn = _rms(big[:, :Q_LORA], gqa_ref[...], Q_LORA).astype(BF16)
    ckvn = _rms(ckv, gkva_ref[...], KV_LORA).astype(BF16)
    qa = _dot(cqn, w_uq_ref[...])
    kn = _dot(ckvn, w_uk_ref[...])
    vt = _dot(ckvn, w_uv_ref[...]).T
    mq = _dot(h, w_mq_ref[...])
    hf_out[...] = _dot(h, w_hf_ref[...])

    cs = cs_sc[...]
    sc = pltpu.roll(cs, D_ROPE, axis=1)

    def rotary(tile, g1_ref, g2_ref):
        return tile * (g1_ref[...] * cs) + pltpu.roll(tile, D_ROPE, axis=1) * (g2_ref[...] * sc)

    q_scale = LOG2E * D_QK ** -0.5
    for hd in range(N_HEADS):
        nope = qa[:, hd * D_QK_PAD:hd * D_QK_PAD + D_NOPE]
        tile = qa[:, hd * D_QK_PAD + D_NOPE:(hd + 1) * D_QK_PAD]
        ss = jnp.sum(nope * nope, axis=-1, keepdims=True) + 0.5 * jnp.sum(tile * tile, axis=-1, keepdims=True)
        rinv = lax.rsqrt(ss * (1.0 / D_QK) + EPS) * q_scale
        q_out[:, hd * D_QK_PAD:hd * D_QK_PAD + D_NOPE] = (nope * gq_nope_ref[...] * rinv).astype(BF16)
        q_out[:, hd * D_QK_PAD + D_NOPE:(hd + 1) * D_QK_PAD] = (
            rotary(tile, gq_r1_ref, gq_r2_ref) * rinv).astype(BF16)

    vt_out[0, :, 0] = vt.reshape(N_HEADS, D_V, vt.shape[-1]).astype(BF16)
    ktile = big[:, Q_LORA:]
    ss_r = 0.5 * jnp.sum(ktile * ktile, axis=-1, keepdims=True)
    rot = rotary(ktile, gk_r1_ref, gk_r2_ref)
    for hd in range(N_HEADS):
        a = kn[:, hd * D_NOPE:(hd + 1) * D_NOPE]
        ss = jnp.sum(a * a, axis=-1, keepdims=True) + ss_r
        rinv = lax.rsqrt(ss * (1.0 / D_QK) + EPS)
        k_out[:, hd * D_QK_PAD:hd * D_QK_PAD + D_NOPE] = (a * gk_nope_ref[...] * rinv).astype(BF16)
        k_out[:, hd * D_QK_PAD + D_NOPE:(hd + 1) * D_QK_PAD] = (rot * rinv).astype(BF16)

    ss_ = []
    for hd in range(N_HEADS):
        sl = slice(hd * MEM_D, (hd + 1) * MEM_D)
        qh = (_rms(mq[:, sl], gmq_ref[...], MEM_D) * (MEM_D ** -0.5)).astype(BF16)
        ss_.append(_dot_nt(qh, kmem_ref[0, :, sl]))
    hi_out[...] = _dot(h, w_hi_ref[...]).astype(BF16)
    hq_out[...] = (hq * _sigmoid(hq) * (HG_D ** -0.5)).astype(BF16)
    ys = []
    for hd in range(N_HEADS):
        sl = slice(hd * MEM_D, (hd + 1) * MEM_D)
        s = ss_[hd]
        p = jnp.exp(s - jnp.max(s, axis=-1, keepdims=True))
        l = jnp.sum(p, axis=-1, keepdims=True)
        ys.append(_dot(p.astype(BF16), vmem_ref[0, :, sl]) / l)
    hg = _dot(h, w_hg_ref[...])
    hg_out[...] = (hg * _sigmoid(hg)).astype(BF16)
    y = jnp.concatenate(ys, axis=-1)
    ymem_out[...] = _rms(y, gmo_ref[...], WIDTH).astype(BF16)

    fill_cs(pos_next_ref)


def _attn_kernel(q_ref, k_ref, vt_ref, o_ref, m_sc, acc_sc, s0_sc, s1_sc, cm0_sc, cm1_sc,
                 *, bq, bk, heads):
    qi = pl.program_id(2)
    s_bufs = (s0_sc, s1_sc)
    cm_bufs = (cm0_sc, cm1_sc)
    m_sc[...] = jnp.full(m_sc.shape, -jnp.inf, F32)
    acc_sc[...] = jnp.zeros(acc_sc.shape, F32)

    def scores(t, slot, hd):
        r0 = t * bk if isinstance(t, int) else pl.multiple_of(t * bk, bk)
        q = q_ref[0, :, hd * D_QK_PAD:(hd + 1) * D_QK_PAD]
        k = k_ref[0, pl.ds(r0, bk), hd * D_QK_PAD:(hd + 1) * D_QK_PAD]
        s = _dot_nt(k, q)
        s_bufs[slot][hd] = s
        cm_bufs[slot][hd] = jnp.max(s, axis=0, keepdims=True)

    def accumulate(t, slot, hd, key_offset=None):
        s = s_bufs[slot][hd]
        if key_offset is not None:
            kv = lax.broadcasted_iota(jnp.int32, (bk, bq), 0) + key_offset
            qq = lax.broadcasted_iota(jnp.int32, (bk, bq), 1)
            s = jnp.where(kv <= qq, s, -jnp.inf)
            cm = jnp.max(s, axis=0, keepdims=True)
        else:
            cm = cm_bufs[slot][hd]
        m_prev = m_sc[hd]
        m_new = jnp.maximum(m_prev, cm)
        p = jnp.exp2(s - m_new)
        alpha = jnp.exp2(m_prev - m_new)
        vt1 = jnp.concatenate([vt_ref[0, hd, t], jnp.ones((ONES_ROWS, bk), BF16)], axis=0)
        acc_sc[hd] = alpha * acc_sc[hd] + _dot(vt1, p.astype(BF16))
        m_sc[hd] = m_new

    for hd in range(heads):
        scores(0, 0, hd)

    def advance(t, slot, key_offset=None):
        for hd in range(heads):
            scores(t + 1, 1 - slot, hd)
            accumulate(t, slot, hd, key_offset)

    def body(jj, carry):
        advance(2 * jj, 0)
        advance(2 * jj + 1, 1)
        return carry

    lax.fori_loop(0, qi, body, 0)
    advance(2 * qi, 0, key_offset=0)
    for hd in range(heads):
        accumulate(2 * qi + 1, 1, hd, key_offset=bk)

    for hd in range(heads):
        o = (acc_sc[hd, :D_V] / acc_sc[hd, D_V:D_V + 1]).T
        o_ref[0, :, hd * D_V:(hd + 1) * D_V] = o.astype(o_ref.dtype)


def _pair_reference(b, m):
    c = b.shape[0]
    n2 = c // (2 * m)
    br = b.reshape(n2, 2 * m, b.shape[1])
    last = br[:, m - 1:m, :]
    return jnp.broadcast_to(last, br.shape).reshape(b.shape)


def _hgrn_kernel(hq_ref, hf_ref, hi_ref, hg_ref, lbl_ref, gain_ref, o_ref, st_ref, b_sc, *, n_chunks, layer):
    C = HG_CHUNK

    @pl.when(pl.program_id(1) == 0)
    def _():
        st_ref[...] = jnp.zeros(st_ref.shape, F32)

    lg = lbl_ref[...].astype(F32)
    e = jnp.exp(lg - jnp.max(lg, axis=0, keepdims=True))
    lb = jnp.sum(e[:layer + 1], axis=0, keepdims=True) / jnp.sum(e, axis=0, keepdims=True)

    row = lax.broadcasted_iota(jnp.int32, (C, C), 0)
    col = lax.broadcasted_iota(jnp.int32, (C, C), 1)
    tri = (col <= row).astype(BF16)
    diag_mask = (((row ^ col) & ~(HG_SUB - 1)) | jnp.where(col <= row, 0, 1)) == 0
    sub_keep = [jnp.where((row & (HG_SUB - 1)) == s_off, 1.0, 0.0).astype(BF16) for s_off in range(HG_SUB)]
    levels = []
    m = HG_SUB
    while m < C:
        bad = ((row ^ col) & ~(2 * m - 1)) | ((row & m) ^ m) | (col & m)
        levels.append((m, bad == 0))
        m *= 2

    def chunk(c):
        r0 = c * C
        fr = hf_ref[0, pl.ds(r0, C), :]
        f = lb + (1.0 - lb) * _sigmoid(fr)
        logf = jnp.log(f) * LOG2E
        kk_all = 1.0 - f
        t0 = logf.astype(BF16)
        r1 = logf - t0.astype(F32)
        t1 = r1.astype(BF16)
        t2 = (r1 - t1.astype(F32)).astype(BF16)
        b_all = _dot(tri, t0) + _dot(tri, t1) + _dot(tri, t2)
        q_all = hq_ref[0, pl.ds(r0, C), :].astype(F32)
        v_all = hi_ref[0, pl.ds(r0, C), :]
        g_all = hg_ref[0, pl.ds(r0, C), :].astype(F32)
        b_sc[c] = b_all
        for hd in range(N_HEADS):
            sl = slice(hd * HG_D, (hd + 1) * HG_D)
            b = b_all[:, sl]
            q = q_all[:, sl]
            kk = kk_all[:, sl]
            v = v_all[:, sl]
            kb = kk.astype(BF16)

            ms = []
            ks = []
            for s_off in range(HG_SUB):
                bs = jnp.concatenate(
                    [jnp.broadcast_to(b_sc[c, i * HG_SUB + s_off:i * HG_SUB + s_off + 1, sl], (HG_SUB, HG_D))
                     for i in range(C // HG_SUB)], axis=0)
                ms.append((q * jnp.exp2(jnp.minimum(b - bs, 0.0))).astype(BF16))
                ks.append(kb * sub_keep[s_off])
            ad = _dot_nt(jnp.concatenate(ms, axis=1), jnp.concatenate(ks, axis=1))
            a = jnp.where(diag_mask, ad, 0.0)
            for m_blk, mask in levels:
                d = b - _pair_reference(b, m_blk)
                qe = (q * jnp.exp2(d)).astype(BF16)
                ke = (kk * jnp.exp2(-d)).astype(BF16)
                a = jnp.where(mask, _dot_nt(qe, ke), a)

            st = st_ref[hd]
            o = _dot(a.astype(BF16), v) + _dot_nt((q * jnp.exp2(b)).astype(BF16), st.astype(BF16))
            b_last = b[C - 1:C, :]
            kd = (kk * jnp.exp2(b_last - b)).astype(BF16)
            vt = v.astype(F32).T.astype(BF16)
            st_ref[hd] = st * jnp.exp2(b_last) + _dot(vt, kd)

            on = _rms(o, gain_ref[:, sl], HG_D)
            o_ref[0, pl.ds(r0, C), sl] = (on * g_all[:, sl]).astype(o_ref.dtype)

    for c in range(n_chunks):
        chunk(c)


def _out_ffn_kernel(x_ref, ymla_ref, yhg_ref, ymem_ref, gmla_ref, w_out_ref, gffn_ref,
                    w_gate_ref, w_up_ref, w_down_ref, o_ref):
    x = x_ref[...].astype(F32)
    ymla = _rms(ymla_ref[...].astype(F32), gmla_ref[...], WIDTH).astype(BF16)
    mix = (_dot(ymla, w_out_ref[0:WIDTH, :])
           + _dot(yhg_ref[...], w_out_ref[WIDTH:2 * WIDTH, :])
           + _dot(ymem_ref[...], w_out_ref[2 * WIDTH:3 * WIDTH, :]))
    x1 = x + mix
    h2 = _rms(x1, gffn_ref[...], x1.shape[-1]).astype(BF16)
    g = _dot(h2, w_gate_ref[...])
    u = _dot(h2, w_up_ref[...])
    act = (g * _sigmoid(g) * u).astype(BF16)
    o_ref[...] = (x1 + _dot(act, w_down_ref[...])).astype(o_ref.dtype)


def _full(shape):
    nd = len(shape)
    return pl.BlockSpec(shape, lambda *_: (0,) * nd)


def _params(sem):
    return pltpu.CompilerParams(dimension_semantics=sem, vmem_limit_bytes=VMEM_LIMIT)


def _row(v):
    return v.reshape(1, -1).astype(F32)


def _layer(x, mem, positions, layer, norm_mix, norm_mem, w_in, q_a_norm, w_uq, kv_a_norm, w_ukv,
           mla_q_norm, mla_k_norm, hg_lb_logits, hg_out_norm, w_mem_kv, mem_q_norm, mem_k_norm,
           mla_out_norm, mem_out_norm, w_out, norm_ffn, w_gate, w_up, w_down):
    B, S, D = x.shape
    M = mem.shape[1]
    T = B * S
    half = D_ROPE // 2
    H = N_HEADS

    assert w_in.shape == (D, W_ALL - D_ROPE)
    wrows = 256
    w_all = pl.pallas_call(
        _w_in_kernel,
        grid=(D // wrows,),
        in_specs=[pl.BlockSpec((wrows, w_in.shape[1]), lambda i: (i, 0))],
        out_specs=pl.BlockSpec((wrows, W_ALL), lambda i: (i, 0)),
        out_shape=jax.ShapeDtypeStruct((D, W_ALL), BF16),
        compiler_params=_params(("arbitrary",)),
        name="w_in_layout",
    )(w_in)
    uq = w_uq.reshape(Q_LORA, H, D_QK)
    uq2 = jnp.concatenate([uq, -uq[:, :, D_NOPE + half:], uq[:, :, D_NOPE:D_NOPE + half]], axis=2)
    ukv = w_ukv.reshape(KV_LORA, H, D_NOPE + D_V)
    w_uk = ukv[:, :, :D_NOPE].reshape(KV_LORA, H * D_NOPE)
    w_uv = ukv[:, :, D_NOPE:].reshape(KV_LORA, H * D_V)
    bf = lambda w: w.astype(BF16)

    pad = jnp.zeros((LANE - D_ROPE,), F32)

    def rotary_gains(g):
        g = g.astype(F32)
        g1 = jnp.concatenate([g[D_NOPE:], pad]).reshape(1, LANE)
        g2 = jnp.concatenate([g[D_NOPE + half:], g[D_NOPE:D_NOPE + half], pad]).reshape(1, LANE)
        return g[:D_NOPE].reshape(1, D_NOPE), g1, g2

    gq_nope, gq_r1, gq_r2 = rotary_gains(mla_q_norm)
    gk_nope, gk_r1, gk_r2 = rotary_gains(mla_k_norm)
    inv_freq = jnp.power(ROPE_THETA, -jnp.arange(half, dtype=F32) / half)
    invf = jnp.tile(inv_freq, LANE // half).reshape(1, LANE)
    phase = jnp.concatenate([jnp.zeros((D_ROPE,), F32), jnp.full((D_ROPE,), -jnp.pi / 2, F32)]).reshape(1, LANE)

    kmem, vmem = pl.pallas_call(
        _mem_kv_kernel,
        grid=(B,),
        in_specs=[pl.BlockSpec((1, M, D), lambda b: (b, 0, 0)),
                  _full((1, D)), _full((D, 2 * WIDTH)), _full((1, MEM_D))],
        out_specs=[pl.BlockSpec((1, M, WIDTH), lambda b: (b, 0, 0))] * 2,
        out_shape=[jax.ShapeDtypeStruct((B, M, WIDTH), BF16)] * 2,
        compiler_params=_params(("arbitrary",)),
        name="mem_kv",
    )(mem, _row(norm_mem), bf(w_mem_kv), _row(mem_k_norm))

    tm = min(512, S)
    assert S % tm == 0
    steps_per_batch = S // tm
    x2 = x.reshape(T, D)
    assert tm % LANE == 0
    pos2 = positions.reshape(T // tm, tm // LANE, LANE).astype(jnp.int32)
    pos_spec = pl.BlockSpec((1, tm // LANE, LANE), lambda i: (i, 0, 0))
    n_tok_steps = T // tm
    pos_next_spec = pl.BlockSpec((1, tm // LANE, LANE), lambda i: (jnp.minimum(i + 1, n_tok_steps - 1), 0, 0))
    sign = jnp.concatenate([jnp.full((D_ROPE,), -1.0, F32), jnp.ones((D_ROPE,), F32)]).reshape(1, LANE)
    tok = lambda w: pl.BlockSpec((tm, w), lambda i: (i, 0))
    col = lambda width, start: pl.BlockSpec((D, width), lambda i: (0, start // width))
    weight_specs = [col(WIDTH, 0), col(KV_LORA, 6 * WIDTH), col(WIDTH, WIDTH), col(WIDTH, 2 * WIDTH),
                    col(WIDTH, 3 * WIDTH), col(WIDTH, 4 * WIDTH), col(WIDTH, 5 * WIDTH)]
    weights = [w_all] * len(weight_specs)
    rest = [_row(q_a_norm), bf(uq2.reshape(Q_LORA, H * D_QK_PAD)), _row(kv_a_norm), bf(w_uk), bf(w_uv),
            gq_nope, gq_r1, gq_r2, gk_nope, gk_r1, gk_r2, _row(mem_q_norm)]
    mem_spec = pl.BlockSpec((1, M, WIDTH), lambda i: (i // steps_per_batch, 0, 0))
    vt_spec = pl.BlockSpec((1, H, 1, D_V, tm),
                           lambda i: (i // steps_per_batch, 0, i % steps_per_batch, 0, 0))
    q_all, k_all, vt_all, hq, hf, hi, hg, ymem = pl.pallas_call(
        _in_proj_kernel,
        grid=(T // tm,),
        in_specs=([tok(D), pos_spec, pos_next_spec, _full((1, LANE)), _full((1, LANE)), _full((1, LANE)),
                   _full((1, D))]
                  + weight_specs + [_full(r.shape) for r in rest]
                  + [mem_spec, mem_spec, _full((1, WIDTH))]),
        out_specs=[tok(H * D_QK_PAD), tok(H * D_QK_PAD), vt_spec] + [tok(WIDTH)] * 5,
        out_shape=[jax.ShapeDtypeStruct((T, H * D_QK_PAD), BF16)] * 2
        + [jax.ShapeDtypeStruct((B, H, steps_per_batch, D_V, tm), BF16)]
        + [jax.ShapeDtypeStruct((T, WIDTH), dt) for dt in (BF16, F32, BF16, BF16, BF16)],
        scratch_shapes=[pltpu.VMEM((tm, LANE), F32)] * 3,
        compiler_params=_params(("arbitrary",)),
        name="in_proj",
    )(x2, pos2, pos2, invf, phase, sign, _row(norm_mix), *weights, *rest, kmem, vmem, _row(mem_out_norm))

    bk = tm
    bq = 2 * bk
    assert S % bq == 0
    hpb = 2
    y_mla = pl.pallas_call(
        functools.partial(_attn_kernel, bq=bq, bk=bk, heads=hpb),
        grid=(B, H // hpb, S // bq),
        in_specs=[pl.BlockSpec((1, bq, hpb * D_QK_PAD), lambda b, h, i: (b, i, h)),
                  pl.BlockSpec((1, S, hpb * D_QK_PAD), lambda b, h, i: (b, 0, h)),
                  pl.BlockSpec((1, hpb, S // bk, D_V, bk), lambda b, h, i: (b, h, 0, 0, 0))],
        out_specs=pl.BlockSpec((1, bq, hpb * D_V), lambda b, h, i: (b, i, h)),
        out_shape=jax.ShapeDtypeStruct((B, S, H * D_V), BF16),
        scratch_shapes=[pltpu.VMEM((hpb, 1, bq), F32),
                        pltpu.VMEM((hpb, D_V + ONES_ROWS, bq), F32),
                        pltpu.VMEM((hpb, bk, bq), F32), pltpu.VMEM((hpb, bk, bq), F32),
                        pltpu.VMEM((hpb, 1, bq), F32), pltpu.VMEM((hpb, 1, bq), F32)],
        compiler_params=_params(("arbitrary", "arbitrary", "arbitrary")),
        name="mla_attn",
    )(q_all.reshape(B, S, H * D_QK_PAD), k_all.reshape(B, S, H * D_QK_PAD), vt_all)

    ts = min(1024, S)
    assert S % ts == 0 and ts % HG_CHUNK == 0
    seq = lambda: pl.BlockSpec((1, ts, WIDTH), lambda b, i: (b, i, 0))
    n_layers = hg_lb_logits.shape[0]
    y_hg = pl.pallas_call(
        functools.partial(_hgrn_kernel, n_chunks=ts // HG_CHUNK, layer=layer),
        grid=(B, S // ts),
        in_specs=[seq(), seq(), seq(), seq(), _full((n_layers, WIDTH)), _full((1, WIDTH))],
        out_specs=seq(),
        out_shape=jax.ShapeDtypeStruct((B, S, WIDTH), BF16),
        scratch_shapes=[pltpu.VMEM((N_HEADS, HG_D, HG_D), F32),
                        pltpu.VMEM((ts // HG_CHUNK, HG_CHUNK, WIDTH), F32)],
        compiler_params=_params(("arbitrary", "arbitrary")),
        name="hgrn",
    )(hq.reshape(B, S, WIDTH), hf.reshape(B, S, WIDTH), hi.reshape(B, S, WIDTH), hg.reshape(B, S, WIDTH),
      hg_lb_logits.astype(F32), _row(hg_out_norm))

    d_ff = w_gate.shape[1]
    once = lambda shape: pl.BlockSpec(shape, lambda i: (0, 0), pipeline_mode=pl.Buffered(1))
    out = pl.pallas_call(
        _out_ffn_kernel,
        grid=(T // tm,),
        in_specs=[tok(D), tok(WIDTH), tok(WIDTH), tok(WIDTH), _full((1, WIDTH)), once((3 * WIDTH, D)),
                  _full((1, D)), once((D, d_ff)), once((D, d_ff)), once((d_ff, D))],
        out_specs=tok(D),
        out_shape=jax.ShapeDtypeStruct((T, D), x.dtype),
        compiler_params=_params(("arbitrary",)),
        name="out_ffn",
    )(x2, y_mla.reshape(T, WIDTH), y_hg.reshape(T, WIDTH), ymem, _row(mla_out_norm), bf(w_out),
      _row(norm_ffn), bf(w_gate), bf(w_up), bf(w_down))
    return out.reshape(B, S, D)


def kernel(x, mem, positions, norm_mix, norm_mem, w_in, q_a_norm, w_uq, kv_a_norm, w_ukv, mla_q_norm, mla_k_norm, hg_lb_logits, hg_out_norm, w_mem_kv, mem_q_norm, mem_k_norm, mla_out_norm, mem_out_norm, w_out, norm_ffn, w_gate, w_up, w_down):
    depth = w_in.shape[0]
    for l in range(depth):
        x = _layer(x, mem, positions, l, norm_mix[l], norm_mem[l], w_in[l], q_a_norm[l], w_uq[l],
                   kv_a_norm[l], w_ukv[l], mla_q_norm[l], mla_k_norm[l], hg_lb_logits, hg_out_norm[l],
                   w_mem_kv[l], mem_q_norm[l], mem_k_norm[l], mla_out_norm[l], mem_out_norm[l],
                   w_out[l], norm_ffn[l], w_gate[l], w_up[l], w_down[l])
    return x
```

```python
import functools

import jax
import jax.numpy as jnp
from jax import lax
from jax.experimental import pallas as pl
from jax.experimental.pallas import tpu as pltpu

F32 = jnp.float32
BF16 = jnp.bfloat16

EPS = 1e-6
N_HEADS = 4
D_NOPE = 128
D_ROPE = 64
D_QK = D_NOPE + D_ROPE
D_QK_PAD = 256
D_V = 128
Q_LORA = 384
KV_LORA = 256
ROPE_THETA = 10000.0
LOG2E = 1.4426950408889634
HG_D = 128
MEM_D = 128
WIDTH = N_HEADS * 128

LANE = 128
HG_CHUNK = 128
HG_SUB = 8
ONES_ROWS = 16

VMEM_LIMIT = 56 * 1024 * 1024


def _dot(a, b):
    return jnp.dot(a, b, preferred_element_type=F32)


def _dot_nt(a, b):
    return lax.dot_general(a, b, (((1,), (1,)), ((), ())), preferred_element_type=F32)


def _rms(x, g, width):
    ss = jnp.sum(x * x, axis=-1, keepdims=True)
    return x * lax.rsqrt(ss * (1.0 / width) + EPS) * g


def _sigmoid(x):
    return 1.0 / (1.0 + jnp.exp(-x))


def _mem_kv_kernel(mem_ref, g_ref, w_ref, kn_ref, k_out, v_out):
    m = mem_ref[0].astype(F32)
    mh = _rms(m, g_ref[...], m.shape[-1]).astype(BF16)
    kv = _dot(mh, w_ref[...])
    for h in range(N_HEADS):
        kh = kv[:, h * MEM_D:(h + 1) * MEM_D]
        k_out[0, :, h * MEM_D:(h + 1) * MEM_D] = _rms(kh, kn_ref[...], MEM_D).astype(BF16)
    v_out[0] = kv[:, WIDTH:].astype(BF16)


W_ALL = Q_LORA + 2 * D_ROPE + 5 * WIDTH + KV_LORA


def _w_in_kernel(wt_ref, o_ref):
    lo = Q_LORA + KV_LORA
    half = D_ROPE // 2

    def put(dst, rows):
        o_ref[:, dst:dst + rows.shape[0]] = rows.T.astype(BF16)

    for c in range(0, Q_LORA, LANE):
        put(c, wt_ref[c:c + LANE, :])
    put(Q_LORA, jnp.concatenate([wt_ref[lo:lo + D_ROPE, :], -wt_ref[lo + half:lo + D_ROPE, :],
                                 wt_ref[lo:lo + half, :]], axis=0))
    for c in range(0, 5 * WIDTH, LANE):
        put(Q_LORA + LANE + c, wt_ref[lo + D_ROPE + c:lo + D_ROPE + c + LANE, :])
    for c in range(0, KV_LORA, LANE):
        put(Q_LORA + LANE + 5 * WIDTH + c, wt_ref[Q_LORA + c:Q_LORA + c + LANE, :])


def _in_proj_kernel(x_ref, pos_ref, pos_next_ref, invf_ref, phase_ref, sign_ref, gmix_ref,
                    w_cqkr_ref, w_ckv_ref, w_hq_ref, w_hf_ref, w_hi_ref, w_hg_ref, w_mq_ref,
                    gqa_ref, w_uq_ref, gkva_ref, w_uk_ref, w_uv_ref,
                    gq_nope_ref, gq_r1_ref, gq_r2_ref, gk_nope_ref, gk_r1_ref, gk_r2_ref,
                    gmq_ref, kmem_ref, vmem_ref, gmo_ref,
                    q_out, k_out, vt_out, hq_out, hf_out, hi_out, hg_out, ymem_out,
                    cs_sc, tc_sc, ts_sc):
    tm = x_ref.shape[0]

    def fill_cs(p_ref):
        posi = p_ref[0]
        first = jnp.broadcast_to(posi[0:1, 0:1], posi.shape)
        step = (lax.broadcasted_iota(jnp.int32, posi.shape, 0) * LANE
                + lax.broadcasted_iota(jnp.int32, posi.shape, 1))
        gap = jnp.max(jnp.abs((posi - first - step).astype(F32)))

        @pl.when(gap == 0.0)
        def _():
            p0 = jnp.broadcast_to(posi[0:1, 0:1], (8, LANE)).astype(F32)
            a = jnp.cos(p0 * invf_ref[...] + phase_ref[...])
            b = pltpu.roll(a, D_ROPE, axis=1) * sign_ref[...]
            cs_sc[...] = a[0:1] * tc_sc[...] + b[0:1] * ts_sc[...]

        @pl.when(gap != 0.0)
        def _():
            posf = posi.astype(F32)
            pos = jnp.concatenate([jnp.broadcast_to(posf[r:r + 1, :], (LANE, LANE)).T
                                   for r in range(posf.shape[0])], axis=0)
            cs_sc[...] = jnp.cos(pos * invf_ref[...] + phase_ref[...])

    @pl.when(pl.program_id(0) == 0)
    def _():
        tf = lax.broadcasted_iota(jnp.int32, (tm, LANE), 0).astype(F32) * invf_ref[...]
        tc_sc[...] = jnp.cos(tf)
        ts_sc[...] = jnp.sin(tf)
        fill_cs(pos_ref)

    x = x_ref[...].astype(F32)
    h = _rms(x, gmix_ref[...], x.shape[-1]).astype(BF16)

    big = _dot(h, w_cqkr_ref[...])
    ckv = _dot(h, w_ckv_ref[...])
    hq = _dot(h, w_hq_ref[...])
    cqn = _rms(big[:, :Q_LORA], gqa_ref[...], Q_LORA).astype(BF16)
    ckvn = _rms(ckv, gkva_ref[...], KV_LORA).astype(BF16)
    qa = _dot(cqn, w_uq_ref[...])
    kn = _dot(ckvn, w_uk_ref[...])
    vt = _dot(ckvn, w_uv_ref[...]).T
    mq = _dot(h, w_mq_ref[...])
    hf_out[...] = _dot(h, w_hf_ref[...])

    cs = cs_sc[...]
    sc = pltpu.roll(cs, D_ROPE, axis=1)

    def rotary(tile, g1_ref, g2_ref):
        return tile * (g1_ref[...] * cs) + pltpu.roll(tile, D_ROPE, axis=1) * (g2_ref[...] * sc)

    q_scale = LOG2E * D_QK ** -0.5
    for hd in range(N_HEADS):
        nope = qa[:, hd * D_QK_PAD:hd * D_QK_PAD + D_NOPE]
        tile = qa[:, hd * D_QK_PAD + D_NOPE:(hd + 1) * D_QK_PAD]
        ss = jnp.sum(nope * nope, axis=-1, keepdims=True) + 0.5 * jnp.sum(tile * tile, axis=-1, keepdims=True)
        rinv = lax.rsqrt(ss * (1.0 / D_QK) + EPS) * q_scale
        q_out[:, hd * D_QK_PAD:hd * D_QK_PAD + D_NOPE] = (nope * gq_nope_ref[...] * rinv).astype(BF16)
        q_out[:, hd * D_QK_PAD + D_NOPE:(hd + 1) * D_QK_PAD] = (
            rotary(tile, gq_r1_ref, gq_r2_ref) * rinv).astype(BF16)

    vt_out[0, :, 0] = vt.reshape(N_HEADS, D_V, vt.shape[-1]).astype(BF16)
    ktile = big[:, Q_LORA:]
    ss_r = 0.5 * jnp.sum(ktile * ktile, axis=-1, keepdims=True)
    rot = rotary(ktile, gk_r1_ref, gk_r2_ref)
    for hd in range(N_HEADS):
        a = kn[:, hd * D_NOPE:(hd + 1) * D_NOPE]
        ss = jnp.sum(a * a, axis=-1, keepdims=True) + ss_r
        rinv = lax.rsqrt(ss * (1.0 / D_QK) + EPS)
        k_out[:, hd * D_QK_PAD:hd * D_QK_PAD + D_NOPE] = (a * gk_nope_ref[...] * rinv).astype(BF16)
        k_out[:, hd * D_QK_PAD + D_NOPE:(hd + 1) * D_QK_PAD] = (rot * rinv).astype(BF16)

    ss_ = []
    for hd in range(N_HEADS):
        sl = slice(hd * MEM_D, (hd + 1) * MEM_D)
        qh = (_rms(mq[:, sl], gmq_ref[...], MEM_D) * (MEM_D ** -0.5)).astype(BF16)
        ss_.append(_dot_nt(qh, kmem_ref[0, :, sl]))
    hi_out[...] = _dot(h, w_hi_ref[...]).astype(BF16)
    hq_out[...] = (hq * _sigmoid(hq) * (HG_D ** -0.5)).astype(BF16)
    ys = []
    for hd in range(N_HEADS):
        sl = slice(hd * MEM_D, (hd + 1) * MEM_D)
        s = ss_[hd]
        p = jnp.exp(s - jnp.max(s, axis=-1, keepdims=True))
        l = jnp.sum(p, axis=-1, keepdims=True)
        ys.append(_dot(p.astype(BF16), vmem_ref[0, :, sl]) / l)
    hg = _dot(h, w_hg_ref[...])
    hg_out[...] = (hg * _sigmoid(hg)).astype(BF16)
    y = jnp.concatenate(ys, axis=-1)
    ymem_out[...] = _rms(y, gmo_ref[...], WIDTH).astype(BF16)

    fill_cs(pos_next_ref)


def _attn_kernel(q_ref, k_ref, vt_ref, o_ref, m_sc, acc_sc, s0_sc, s1_sc, cm0_sc, cm1_sc,
                 *, bq, bk, heads):
    qi = pl.program_id(2)
    s_bufs = (s0_sc, s1_sc)
    cm_bufs = (cm0_sc, cm1_sc)
    m_sc[...] = jnp.full(m_sc.shape, -jnp.inf, F32)
    acc_sc[...] = jnp.zeros(acc_sc.shape, F32)

    def scores(t, slot, hd):
        r0 = t * bk if isinstance(t, int) else pl.multiple_of(t * bk, bk)
        q = q_ref[0, :, hd * D_QK_PAD:(hd + 1) * D_QK_PAD]
        k = k_ref[0, pl.ds(r0, bk), hd * D_QK_PAD:(hd + 1) * D_QK_PAD]
        s = _dot_nt(k, q)
        s_bufs[slot][hd] = s
        cm_bufs[slot][hd] = jnp.max(s, axis=0, keepdims=True)

    def accumulate(t, slot, hd, key_offset=None):
        s = s_bufs[slot][hd]
        if key_offset is not None:
            kv = lax.broadcasted_iota(jnp.int32, (bk, bq), 0) + key_offset
            qq = lax.broadcasted_iota(jnp.int32, (bk, bq), 1)
            s = jnp.where(kv <= qq, s, -jnp.inf)
            cm = jnp.max(s, axis=0, keepdims=True)
        else:
            cm = cm_bufs[slot][hd]
        m_prev = m_sc[hd]
        m_new = jnp.maximum(m_prev, cm)
        p = jnp.exp2(s - m_new)
        alpha = jnp.exp2(m_prev - m_new)
        vt1 = jnp.concatenate([vt_ref[0, hd, t], jnp.ones((ONES_ROWS, bk), BF16)], axis=0)
        acc_sc[hd] = alpha * acc_sc[hd] + _dot(vt1, p.astype(BF16))
        m_sc[hd] = m_new

    for hd in range(heads):
        scores(0, 0, hd)

    def advance(t, slot, key_offset=None):
        for hd in range(heads):
            scores(t + 1, 1 - slot, hd)
            accumulate(t, slot, hd, key_offset)

    def body(jj, carry):
        advance(2 * jj, 0)
        advance(2 * jj + 1, 1)
        return carry

    lax.fori_loop(0, qi, body, 0)
    advance(2 * qi, 0, key_offset=0)
    for hd in range(heads):
        accumulate(2 * qi + 1, 1, hd, key_offset=bk)

    for hd in range(heads):
        o = (acc_sc[hd, :D_V] / acc_sc[hd, D_V:D_V + 1]).T
        o_ref[0, :, hd * D_V:(hd + 1) * D_V] = o.astype(o_ref.dtype)


def _pair_reference(b, m):
    c = b.shape[0]
    n2 = c // (2 * m)
    br = b.reshape(n2, 2 * m, b.shape[1])
    last = br[:, m - 1:m, :]
    return jnp.broadcast_to(last, br.shape).reshape(b.shape)


def _hgrn_kernel(hq_ref, hf_ref, hi_ref, hg_ref, lbl_ref, gain_ref, o_ref, st_ref, b_sc, *, n_chunks, layer):
    C = HG_CHUNK

    @pl.when(pl.program_id(1) == 0)
    def _():
        st_ref[...] = jnp.zeros(st_ref.shape, F32)

    lg = lbl_ref[...].astype(F32)
    e = jnp.exp(lg - jnp.max(lg, axis=0, keepdims=True))
    lb = jnp.sum(e[:layer + 1], axis=0, keepdims=True) / jnp.sum(e, axis=0, keepdims=True)

    row = lax.broadcasted_iota(jnp.int32, (C, C), 0)
    col = lax.broadcasted_iota(jnp.int32, (C, C), 1)
    tri = (col <= row).astype(BF16)
    diag_mask = (((row ^ col) & ~(HG_SUB - 1)) | jnp.where(col <= row, 0, 1)) == 0
    sub_keep = [jnp.where((row & (HG_SUB - 1)) == s_off, 1.0, 0.0).astype(BF16) for s_off in range(HG_SUB)]
    levels = []
    m = HG_SUB
    while m < C:
        bad = ((row ^ col) & ~(2 * m - 1)) | ((row & m) ^ m) | (col & m)
        levels.append((m, bad == 0))
        m *= 2

    def chunk(c):
        r0 = c * C
        fr = hf_ref[0, pl.ds(r0, C), :]
        f = lb + (1.0 - lb) * _sigmoid(fr)
        logf = jnp.log(f) * LOG2E
        kk_all = 1.0 - f
        t0 = logf.astype(BF16)
        r1 = logf - t0.astype(F32)
        t1 = r1.astype(BF16)
        t2 = (r1 - t1.astype(F32)).astype(BF16)
        b_all = _dot(tri, t0) + _dot(tri, t1) + _dot(tri, t2)
        q_all = hq_ref[0, pl.ds(r0, C), :].astype(F32)
        v_all = hi_ref[0, pl.ds(r0, C), :]
        g_all = hg_ref[0, pl.ds(r0, C), :].astype(F32)
        b_sc[c] = b_all
        for hd in range(N_HEADS):
            sl = slice(hd * HG_D, (hd + 1) * HG_D)
            b = b_all[:, sl]
            q = q_all[:, sl]
            kk = kk_all[:, sl]
            v = v_all[:, sl]
            kb = kk.astype(BF16)

            ms = []
            ks = []
            for s_off in range(HG_SUB):
                bs = jnp.concatenate(
                    [jnp.broadcast_to(b_sc[c, i * HG_SUB + s_off:i * HG_SUB + s_off + 1, sl], (HG_SUB, HG_D))
                     for i in range(C // HG_SUB)], axis=0)
                ms.append((q * jnp.exp2(jnp.minimum(b - bs, 0.0))).astype(BF16))
                ks.append(kb * sub_keep[s_off])
            ad = _dot_nt(jnp.concatenate(ms, axis=1), jnp.concatenate(ks, axis=1))
            a = jnp.where(diag_mask, ad, 0.0)
            for m_blk, mask in levels:
                d = b - _pair_reference(b, m_blk)
                qe = (q * jnp.exp2(d)).astype(BF16)
                ke = (kk * jnp.exp2(-d)).astype(BF16)
                a = jnp.where(mask, _dot_nt(qe, ke), a)

            st = st_ref[hd]
            o = _dot(a.astype(BF16), v) + _dot_nt((q * jnp.exp2(b)).astype(BF16), st.astype(BF16))
            b_last = b[C - 1:C, :]
            kd = (kk * jnp.exp2(b_last - b)).astype(BF16)
            vt = v.astype(F32).T.astype(BF16)
            st_ref[hd] = st * jnp.exp2(b_last) + _dot(vt, kd)

            on = _rms(o, gain_ref[:, sl], HG_D)
            o_ref[0, pl.ds(r0, C), sl] = (on * g_all[:, sl]).astype(o_ref.dtype)

    for c in range(n_chunks):
        chunk(c)


def _out_ffn_kernel(x_ref, ymla_ref, yhg_ref, ymem_ref, gmla_ref, w_out_ref, gffn_ref,
                    w_gate_ref, w_up_ref, w_down_ref, o_ref):
    x = x_ref[...].astype(F32)
    ymla = _rms(ymla_ref[...].astype(F32), gmla_ref[...], WIDTH).astype(BF16)
    mix = (_dot(ymla, w_out_ref[0:WIDTH, :])
           + _dot(yhg_ref[...], w_out_ref[WIDTH:2 * WIDTH, :])
           + _dot(ymem_ref[...], w_out_ref[2 * WIDTH:3 * WIDTH, :]))
    x1 = x + mix
    h2 = _rms(x1, gffn_ref[...], x1.shape[-1]).astype(BF16)
    g = _dot(h2, w_gate_ref[...])
    u = _dot(h2, w_up_ref[...])
    act = (g * _sigmoid(g) * u).astype(BF16)
    o_ref[...] = (x1 + _dot(act, w_down_ref[...])).astype(o_ref.dtype)


def _full(shape):
    nd = len(shape)
    return pl.BlockSpec(shape, lambda *_: (0,) * nd)


def _params(sem):
    return pltpu.CompilerParams(dimension_semantics=sem, vmem_limit_bytes=VMEM_LIMIT)


def _row(v):
    return v.reshape(1, -1).astype(F32)


def _layer(x, mem, positions, layer, norm_mix, norm_mem, w_in, q_a_norm, w_uq, kv_a_norm, w_ukv,
           mla_q_norm, mla_k_norm, hg_lb_logits, hg_out_norm, w_mem_kv, mem_q_norm, mem_k_norm,
           mla_out_norm, mem_out_norm, w_out, norm_ffn, w_gate, w_up, w_down):
    B, S, D = x.shape
    M = mem.shape[1]
    T = B * S
    half = D_ROPE // 2
    H = N_HEADS

    assert w_in.shape == (D, W_ALL - D_ROPE)
    wrows = 256
    w_all = pl.pallas_call(
        _w_in_kernel,
        grid=(D // wrows,),
        in_specs=[pl.BlockSpec((w_in.shape[1], wrows), lambda i: (0, i))],
        out_specs=pl.BlockSpec((wrows, W_ALL), lambda i: (i, 0)),
        out_shape=jax.ShapeDtypeStruct((D, W_ALL), BF16),
        compiler_params=_params(("arbitrary",)),
        name="w_in_layout",
    )(w_in.T)
    uq = w_uq.reshape(Q_LORA, H, D_QK)
    uq2 = jnp.concatenate([uq, -uq[:, :, D_NOPE + half:], uq[:, :, D_NOPE:D_NOPE + half]], axis=2)
    ukv = w_ukv.reshape(KV_LORA, H, D_NOPE + D_V)
    w_uk = ukv[:, :, :D_NOPE].reshape(KV_LORA, H * D_NOPE)
    w_uv = ukv[:, :, D_NOPE:].reshape(KV_LORA, H * D_V)
    bf = lambda w: w.astype(BF16)

    pad = jnp.zeros((LANE - D_ROPE,), F32)

    def rotary_gains(g):
        g = g.astype(F32)
        g1 = jnp.concatenate([g[D_NOPE:], pad]).reshape(1, LANE)
        g2 = jnp.concatenate([g[D_NOPE + half:], g[D_NOPE:D_NOPE + half], pad]).reshape(1, LANE)
        return g[:D_NOPE].reshape(1, D_NOPE), g1, g2

    gq_nope, gq_r1, gq_r2 = rotary_gains(mla_q_norm)
    gk_nope, gk_r1, gk_r2 = rotary_gains(mla_k_norm)
    inv_freq = jnp.power(ROPE_THETA, -jnp.arange(half, dtype=F32) / half)
    invf = jnp.tile(inv_freq, LANE // half).reshape(1, LANE)
    phase = jnp.concatenate([jnp.zeros((D_ROPE,), F32), jnp.full((D_ROPE,), -jnp.pi / 2, F32)]).reshape(1, LANE)

    kmem, vmem = pl.pallas_call(
        _mem_kv_kernel,
        grid=(B,),
        in_specs=[pl.BlockSpec((1, M, D), lambda b: (b, 0, 0)),
                  _full((1, D)), _full((D, 2 * WIDTH)), _full((1, MEM_D))],
        out_specs=[pl.BlockSpec((1, M, WIDTH), lambda b: (b, 0, 0))] * 2,
        out_shape=[jax.ShapeDtypeStruct((B, M, WIDTH), BF16)] * 2,
        compiler_params=_params(("arbitrary",)),
        name="mem_kv",
    )(mem, _row(norm_mem), bf(w_mem_kv), _row(mem_k_norm))

    tm = min(512, S)
    assert S % tm == 0
    steps_per_batch = S // tm
    x2 = x.reshape(T, D)
    assert tm % LANE == 0
    pos2 = positions.reshape(T // tm, tm // LANE, LANE).astype(jnp.int32)
    pos_spec = pl.BlockSpec((1, tm // LANE, LANE), lambda i: (i, 0, 0))
    n_tok_steps = T // tm
    pos_next_spec = pl.BlockSpec((1, tm // LANE, LANE), lambda i: (jnp.minimum(i + 1, n_tok_steps - 1), 0, 0))
    sign = jnp.concatenate([jnp.full((D_ROPE,), -1.0, F32), jnp.ones((D_ROPE,), F32)]).reshape(1, LANE)
    tok = lambda w: pl.BlockSpec((tm, w), lambda i: (i, 0))
    col = lambda width, start: pl.BlockSpec((D, width), lambda i: (0, start // width))
    weight_specs = [col(WIDTH, 0), col(KV_LORA, 6 * WIDTH), col(WIDTH, WIDTH), col(WIDTH, 2 * WIDTH),
                    col(WIDTH, 3 * WIDTH), col(WIDTH, 4 * WIDTH), col(WIDTH, 5 * WIDTH)]
    weights = [w_all] * len(weight_specs)
    rest = [_row(q_a_norm), bf(uq2.reshape(Q_LORA, H * D_QK_PAD)), _row(kv_a_norm), bf(w_uk), bf(w_uv),
            gq_nope, gq_r1, gq_r2, gk_nope, gk_r1, gk_r2, _row(mem_q_norm)]
    mem_spec = pl.BlockSpec((1, M, WIDTH), lambda i: (i // steps_per_batch, 0, 0))
    vt_spec = pl.BlockSpec((1, H, 1, D_V, tm),
                           lambda i: (i // steps_per_batch, 0, i % steps_per_batch, 0, 0))
    q_all, k_all, vt_all, hq, hf, hi, hg, ymem = pl.pallas_call(
        _in_proj_kernel,
        grid=(T // tm,),
        in_specs=([tok(D), pos_spec, pos_next_spec, _full((1, LANE)), _full((1, LANE)), _full((1, LANE)),
                   _full((1, D))]
                  + weight_specs + [_full(r.shape) for r in rest]
                  + [mem_spec, mem_spec, _full((1, WIDTH))]),
        out_specs=[tok(H * D_QK_PAD), tok(H * D_QK_PAD), vt_spec] + [tok(WIDTH)] * 5,
        out_shape=[jax.ShapeDtypeStruct((T, H * D_QK_PAD), BF16)] * 2
        + [jax.ShapeDtypeStruct((B, H, steps_per_batch, D_V, tm), BF16)]
        + [jax.ShapeDtypeStruct((T, WIDTH), dt) for dt in (BF16, F32, BF16, BF16, BF16)],
        scratch_shapes=[pltpu.VMEM((tm, LANE), F32)] * 3,
        compiler_params=_params(("arbitrary",)),
        name="in_proj",
    )(x2, pos2, pos2, invf, phase, sign, _row(norm_mix), *weights, *rest, kmem, vmem, _row(mem_out_norm))

    bk = tm
    bq = 2 * bk
    assert S % bq == 0
    hpb = 2
    y_mla = pl.pallas_call(
        functools.partial(_attn_kernel, bq=bq, bk=bk, heads=hpb),
        grid=(B, H // hpb, S // bq),
        in_specs=[pl.BlockSpec((1, bq, hpb * D_QK_PAD), lambda b, h, i: (b, i, h)),
                  pl.BlockSpec((1, S, hpb * D_QK_PAD), lambda b, h, i: (b, 0, h)),
                  pl.BlockSpec((1, hpb, S // bk, D_V, bk), lambda b, h, i: (b, h, 0, 0, 0))],
        out_specs=pl.BlockSpec((1, bq, hpb * D_V), lambda b, h, i: (b, i, h)),
        out_shape=jax.ShapeDtypeStruct((B, S, H * D_V), BF16),
        scratch_shapes=[pltpu.VMEM((hpb, 1, bq), F32),
                        pltpu.VMEM((hpb, D_V + ONES_ROWS, bq), F32),
                        pltpu.VMEM((hpb, bk, bq), F32), pltpu.VMEM((hpb, bk, bq), F32),
                        pltpu.VMEM((hpb, 1, bq), F32), pltpu.VMEM((hpb, 1, bq), F32)],
        compiler_params=_params(("arbitrary", "arbitrary", "arbitrary")),
        name="mla_attn",
    )(q_all.reshape(B, S, H * D_QK_PAD), k_all.reshape(B, S, H * D_QK_PAD), vt_all)

    ts = min(1024, S)
    assert S % ts == 0 and ts % HG_CHUNK == 0
    seq = lambda: pl.BlockSpec((1, ts, WIDTH), lambda b, i: (b, i, 0))
    n_layers = hg_lb_logits.shape[0]
    y_hg = pl.pallas_call(
        functools.partial(_hgrn_kernel, n_chunks=ts // HG_CHUNK, layer=layer),
        grid=(B, S // ts),
        in_specs=[seq(), seq(), seq(), seq(), _full((n_layers, WIDTH)), _full((1, WIDTH))],
        out_specs=seq(),
        out_shape=jax.ShapeDtypeStruct((B, S, WIDTH), BF16),
        scratch_shapes=[pltpu.VMEM((N_HEADS, HG_D, HG_D), F32),
                        pltpu.VMEM((ts // HG_CHUNK, HG_CHUNK, WIDTH), F32)],
        compiler_params=_params(("arbitrary", "arbitrary")),
        name="hgrn",
    )(hq.reshape(B, S, WIDTH), hf.reshape(B, S, WIDTH), hi.reshape(B, S, WIDTH), hg.reshape(B, S, WIDTH),
      hg_lb_logits.astype(F32), _row(hg_out_norm))

    d_ff = w_gate.shape[1]
    once = lambda shape: pl.BlockSpec(shape, lambda i: (0, 0), pipeline_mode=pl.Buffered(1))
    out = pl.pallas_call(
        _out_ffn_kernel,
        grid=(T // tm,),
        in_specs=[tok(D), tok(WIDTH), tok(WIDTH), tok(WIDTH), _full((1, WIDTH)), once((3 * WIDTH, D)),
                  _full((1, D)), once((D, d_ff)), once((D, d_ff)), once((d_ff, D))],
        out_specs=tok(D),
        out_shape=jax.ShapeDtypeStruct((T, D), x.dtype),
        compiler_params=_params(("arbitrary",)),
        name="out_ffn",
    )(x2, y_mla.reshape(T, WIDTH), y_hg.reshape(T, WIDTH), ymem, _row(mla_out_norm), bf(w_out),
      _row(norm_ffn), bf(w_gate), bf(w_up), bf(w_down))
    return out.reshape(B, S, D)


def kernel(x, mem, positions, norm_mix, norm_mem, w_in, q_a_norm, w_uq, kv_a_norm, w_ukv, mla_q_norm, mla_k_norm, hg_lb_logits, hg_out_norm, w_mem_kv, mem_q_norm, mem_k_norm, mla_out_norm, mem_out_norm, w_out, norm_ffn, w_gate, w_up, w_down):
    depth = w_in.shape[0]
    for l in range(depth):
        x = _layer(x, mem, positions, l, norm_mix[l], norm_mem[l], w_in[l], q_a_norm[l], w_uq[l],
                   kv_a_norm[l], w_ukv[l], mla_q_norm[l], mla_k_norm[l], hg_lb_logits, hg_out_norm[l],
                   w_mem_kv[l], mem_q_norm[l], mem_k_norm[l], mla_out_norm[l], mem_out_norm[l],
                   w_out[l], norm_ffn[l], w_gate[l], w_up[l], w_down[l])
    return x
```

```python
import functools

import jax
import jax.numpy as jnp
from jax import lax
from jax.experimental import pallas as pl
from jax.experimental.pallas import tpu as pltpu

F32 = jnp.float32
BF16 = jnp.bfloat16

EPS = 1e-6
N_HEADS = 4
D_NOPE = 128
D_ROPE = 64
D_QK = D_NOPE + D_ROPE
D_QK_PAD = 256
D_V = 128
Q_LORA = 384
KV_LORA = 256
ROPE_THETA = 10000.0
LOG2E = 1.4426950408889634
HG_D = 128
MEM_D = 128
WIDTH = N_HEADS * 128

LANE = 128
HG_CHUNK = 128
HG_SUB = 8
ONES_ROWS = 16

VMEM_LIMIT = 56 * 1024 * 1024


def _dot(a, b):
    return jnp.dot(a, b, preferred_element_type=F32)


def _dot_nt(a, b):
    return lax.dot_general(a, b, (((1,), (1,)), ((), ())), preferred_element_type=F32)


def _rms(x, g, width):
    ss = jnp.sum(x * x, axis=-1, keepdims=True)
    return x * lax.rsqrt(ss * (1.0 / width) + EPS) * g


def _sigmoid(x):
    return 1.0 / (1.0 + jnp.exp(-x))


def _mem_kv_kernel(mem_ref, g_ref, w_ref, kn_ref, k_out, v_out):
    m = mem_ref[0].astype(F32)
    mh = _rms(m, g_ref[...], m.shape[-1]).astype(BF16)
    kv = _dot(mh, w_ref[...])
    for h in range(N_HEADS):
        kh = kv[:, h * MEM_D:(h + 1) * MEM_D]
        k_out[0, :, h * MEM_D:(h + 1) * MEM_D] = _rms(kh, kn_ref[...], MEM_D).astype(BF16)
    v_out[0] = kv[:, WIDTH:].astype(BF16)


W_ALL = Q_LORA + 2 * D_ROPE + 5 * WIDTH + KV_LORA


def _w_in_kernel(wt_ref, o_ref):
    lo = Q_LORA + KV_LORA
    half = D_ROPE // 2

    def put(dst, rows):
        o_ref[:, dst:dst + rows.shape[0]] = rows.T.astype(BF16)

    for c in range(0, Q_LORA, LANE):
        put(c, wt_ref[c:c + LANE, :])
    put(Q_LORA, jnp.concatenate([wt_ref[lo:lo + D_ROPE, :], -wt_ref[lo + half:lo + D_ROPE, :],
                                 wt_ref[lo:lo + half, :]], axis=0))
    for c in range(0, 5 * WIDTH, LANE):
        put(Q_LORA + LANE + c, wt_ref[lo + D_ROPE + c:lo + D_ROPE + c + LANE, :])
    for c in range(0, KV_LORA, LANE):
        put(Q_LORA + LANE + 5 * WIDTH + c, wt_ref[Q_LORA + c:Q_LORA + c + LANE, :])


def _in_proj_kernel(x_ref, pos_ref, pos_next_ref, invf_ref, phase_ref, sign_ref, gmix_ref,
                    w_cqkr_ref, w_ckv_ref, w_hq_ref, w_hf_ref, w_hi_ref, w_hg_ref, w_mq_ref,
                    gqa_ref, w_uq_ref, gkva_ref, w_uk_ref, w_uv_ref,
                    gq_nope_ref, gq_r1_ref, gq_r2_ref, gk_nope_ref, gk_r1_ref, gk_r2_ref,
                    gmq_ref, kmem_ref, vmem_ref, gmo_ref,
                    q_out, k_out, vt_out, hq_out, hf_out, hi_out, hg_out, ymem_out,
                    cs_sc, tc_sc, ts_sc):
    tm = x_ref.shape[0]

    def fill_cs(p_ref):
        posi = p_ref[0]
        first = jnp.broadcast_to(posi[0:1, 0:1], posi.shape)
        step = (lax.broadcasted_iota(jnp.int32, posi.shape, 0) * LANE
                + lax.broadcasted_iota(jnp.int32, posi.shape, 1))
        gap = jnp.max(jnp.abs((posi - first - step).astype(F32)))

        @pl.when(gap == 0.0)
        def _():
            p0 = jnp.broadcast_to(posi[0:1, 0:1], (8, LANE)).astype(F32)
            a = jnp.cos(p0 * invf_ref[...] + phase_ref[...])
            b = pltpu.roll(a, D_ROPE, axis=1) * sign_ref[...]
            cs_sc[...] = a[0:1] * tc_sc[...] + b[0:1] * ts_sc[...]

        @pl.when(gap != 0.0)
        def _():
            posf = posi.astype(F32)
            pos = jnp.concatenate([jnp.broadcast_to(posf[r:r + 1, :], (LANE, LANE)).T
                                   for r in range(posf.shape[0])], axis=0)
            cs_sc[...] = jnp.cos(pos * invf_ref[...] + phase_ref[...])

    @pl.when(pl.program_id(0) == 0)
    def _():
        tf = lax.broadcasted_iota(jnp.int32, (tm, LANE), 0).astype(F32) * invf_ref[...]
        tc_sc[...] = jnp.cos(tf)
        ts_sc[...] = jnp.sin(tf)
        fill_cs(pos_ref)

    x = x_ref[...].astype(F32)
    h = _rms(x, gmix_ref[...], x.shape[-1]).astype(BF16)

    big = _dot(h, w_cqkr_ref[...])
    ckv = _dot(h, w_ckv_ref[...])
    hq = _dot(h, w_hq_ref[...])
    cqn = _rms(big[:, :Q_LORA], gqa_ref[...], Q_LORA).astype(BF16)
    ckvn = _rms(ckv, gkva_ref[...], KV_LORA).astype(BF16)
    qa = _dot(cqn, w_uq_ref[...])
    kn = _dot(ckvn, w_uk_ref[...])
    vt = _dot(ckvn, w_uv_ref[...]).T
    mq = _dot(h, w_mq_ref[...])
    hf_out[...] = _dot(h, w_hf_ref[...])

    cs = cs_sc[...]
    sc = pltpu.roll(cs, D_ROPE, axis=1)

    def rotary(tile, g1_ref, g2_ref):
        return tile * (g1_ref[...] * cs) + pltpu.roll(tile, D_ROPE, axis=1) * (g2_ref[...] * sc)

    q_scale = LOG2E * D_QK ** -0.5
    for hd in range(N_HEADS):
        nope = qa[:, hd * D_QK_PAD:hd * D_QK_PAD + D_NOPE]
        tile = qa[:, hd * D_QK_PAD + D_NOPE:(hd + 1) * D_QK_PAD]
        ss = jnp.sum(nope * nope, axis=-1, keepdims=True) + 0.5 * jnp.sum(tile * tile, axis=-1, keepdims=True)
        rinv = lax.rsqrt(ss * (1.0 / D_QK) + EPS) * q_scale
        q_out[:, hd * D_QK_PAD:hd * D_QK_PAD + D_NOPE] = (nope * gq_nope_ref[...] * rinv).astype(BF16)
        q_out[:, hd * D_QK_PAD + D_NOPE:(hd + 1) * D_QK_PAD] = (
            rotary(tile, gq_r1_ref, gq_r2_ref) * rinv).astype(BF16)

    vt_out[0, :, 0] = vt.reshape(N_HEADS, D_V, vt.shape[-1]).astype(BF16)
    ktile = big[:, Q_LORA:]
    ss_r = 0.5 * jnp.sum(ktile * ktile, axis=-1, keepdims=True)
    rot = rotary(ktile, gk_r1_ref, gk_r2_ref)
    for hd in range(N_HEADS):
        a = kn[:, hd * D_NOPE:(hd + 1) * D_NOPE]
        ss = jnp.sum(a * a, axis=-1, keepdims=True) + ss_r
        rinv = lax.rsqrt(ss * (1.0 / D_QK) + EPS)
        k_out[:, hd * D_QK_PAD:hd * D_QK_PAD + D_NOPE] = (a * gk_nope_ref[...] * rinv).astype(BF16)
        k_out[:, hd * D_QK_PAD + D_NOPE:(hd + 1) * D_QK_PAD] = (rot * rinv).astype(BF16)

    ss_ = []
    for hd in range(N_HEADS):
        sl = slice(hd * MEM_D, (hd + 1) * MEM_D)
        qh = (_rms(mq[:, sl], gmq_ref[...], MEM_D) * (MEM_D ** -0.5)).astype(BF16)
        ss_.append(_dot_nt(qh, kmem_ref[0, :, sl]))
    hi_out[...] = _dot(h, w_hi_ref[...]).astype(BF16)
    hq_out[...] = (hq * _sigmoid(hq) * (HG_D ** -0.5)).astype(BF16)
    ys = []
    for hd in range(N_HEADS):
        sl = slice(hd * MEM_D, (hd + 1) * MEM_D)
        s = ss_[hd]
        p = jnp.exp(s - jnp.max(s, axis=-1, keepdims=True))
        l = jnp.sum(p, axis=-1, keepdims=True)
        ys.append(_dot(p.astype(BF16), vmem_ref[0, :, sl]) / l)
    hg = _dot(h, w_hg_ref[...])
    hg_out[...] = (hg * _sigmoid(hg)).astype(BF16)
    y = jnp.concatenate(ys, axis=-1)
    ymem_out[...] = _rms(y, gmo_ref[...], WIDTH).astype(BF16)

    fill_cs(pos_next_ref)


def _attn_kernel(q_ref, k_ref, vt_ref, o_ref, m_sc, acc_sc, s0_sc, s1_sc, cm0_sc, cm1_sc,
                 *, bq, bk, heads):
    qi = pl.program_id(2)
    s_bufs = (s0_sc, s1_sc)
    cm_bufs = (cm0_sc, cm1_sc)
    m_sc[...] = jnp.full(m_sc.shape, -jnp.inf, F32)
    acc_sc[...] = jnp.zeros(acc_sc.shape, F32)

    def scores(t, slot, hd):
        r0 = t * bk if isinstance(t, int) else pl.multiple_of(t * bk, bk)
        q = q_ref[0, :, hd * D_QK_PAD:(hd + 1) * D_QK_PAD]
        k = k_ref[0, pl.ds(r0, bk), hd * D_QK_PAD:(hd + 1) * D_QK_PAD]
        s = _dot_nt(k, q)
        s_bufs[slot][hd] = s
        cm_bufs[slot][hd] = jnp.max(s, axis=0, keepdims=True)

    def accumulate(t, slot, hd, key_offset=None):
        s = s_bufs[slot][hd]
        if key_offset is not None:
            kv = lax.broadcasted_iota(jnp.int32, (bk, bq), 0) + key_offset
            qq = lax.broadcasted_iota(jnp.int32, (bk, bq), 1)
            s = jnp.where(kv <= qq, s, -jnp.inf)
            cm = jnp.max(s, axis=0, keepdims=True)
        else:
            cm = cm_bufs[slot][hd]
        m_prev = m_sc[hd]
        m_new = jnp.maximum(m_prev, cm)
        p = jnp.exp2(s - m_new)
        alpha = jnp.exp2(m_prev - m_new)
        vt1 = jnp.concatenate([vt_ref[0, hd, t], jnp.ones((ONES_ROWS, bk), BF16)], axis=0)
        acc_sc[hd] = alpha * acc_sc[hd] + _dot(vt1, p.astype(BF16))
        m_sc[hd] = m_new

    for hd in range(heads):
        scores(0, 0, hd)

    def advance(t, slot, key_offset=None):
        for hd in range(heads):
            scores(t + 1, 1 - slot, hd)
            accumulate(t, slot, hd, key_offset)

    def body(jj, carry):
        advance(2 * jj, 0)
        advance(2 * jj + 1, 1)
        return carry

    lax.fori_loop(0, qi, body, 0)
    advance(2 * qi, 0, key_offset=0)
    for hd in range(heads):
        accumulate(2 * qi + 1, 1, hd, key_offset=bk)

    for hd in range(heads):
        o = (acc_sc[hd, :D_V] / acc_sc[hd, D_V:D_V + 1]).T
        o_ref[0, :, hd * D_V:(hd + 1) * D_V] = o.astype(o_ref.dtype)


def _pair_reference(b, m):
    c = b.shape[0]
    n2 = c // (2 * m)
    br = b.reshape(n2, 2 * m, b.shape[1])
    last = br[:, m - 1:m, :]
    return jnp.broadcast_to(last, br.shape).reshape(b.shape)


def _hgrn_kernel(hq_ref, hf_ref, hi_ref, hg_ref, lbl_ref, gain_ref, o_ref, st_ref, b_sc, *, n_chunks, layer):
    C = HG_CHUNK

    @pl.when(pl.program_id(1) == 0)
    def _():
        st_ref[...] = jnp.zeros(st_ref.shape, F32)

    lg = lbl_ref[...].astype(F32)
    e = jnp.exp(lg - jnp.max(lg, axis=0, keepdims=True))
    lb = jnp.sum(e[:layer + 1], axis=0, keepdims=True) / jnp.sum(e, axis=0, keepdims=True)

    row = lax.broadcasted_iota(jnp.int32, (C, C), 0)
    col = lax.broadcasted_iota(jnp.int32, (C, C), 1)
    tri = (col <= row).astype(BF16)
    diag_mask = (((row ^ col) & ~(HG_SUB - 1)) | jnp.where(col <= row, 0, 1)) == 0
    sub_keep = [jnp.where((row & (HG_SUB - 1)) == s_off, 1.0, 0.0).astype(BF16) for s_off in range(HG_SUB)]
    levels = []
    m = HG_SUB
    while m < C:
        bad = ((row ^ col) & ~(2 * m - 1)) | ((row & m) ^ m) | (col & m)
        levels.append((m, bad == 0))
        m *= 2

    def chunk(c):
        r0 = c * C
        fr = hf_ref[0, pl.ds(r0, C), :]
        f = lb + (1.0 - lb) * _sigmoid(fr)
        logf = jnp.log(f) * LOG2E
        kk_all = 1.0 - f
        t0 = logf.astype(BF16)
        r1 = logf - t0.astype(F32)
        t1 = r1.astype(BF16)
        t2 = (r1 - t1.astype(F32)).astype(BF16)
        b_all = _dot(tri, t0) + _dot(tri, t1) + _dot(tri, t2)
        q_all = hq_ref[0, pl.ds(r0, C), :].astype(F32)
        v_all = hi_ref[0, pl.ds(r0, C), :]
        g_all = hg_ref[0, pl.ds(r0, C), :].astype(F32)
        b_sc[c] = b_all
        zero = jnp.zeros((C, HG_D), BF16)

        def pair_nt(x0, x1):
            return jnp.concatenate([jnp.concatenate([x0, zero], axis=1),
                                    jnp.concatenate([zero, x1], axis=1)], axis=0)

        for h0 in range(0, N_HEADS, 2):
            heads = (h0, h0 + 1)
            sls = [slice(hd * HG_D, (hd + 1) * HG_D) for hd in heads]
            bs_ = [b_all[:, sl] for sl in sls]
            qs = [q_all[:, sl] for sl in sls]
            kks = [kk_all[:, sl] for sl in sls]
            vs = [v_all[:, sl] for sl in sls]

            a_ = []
            for b, q, kk, sl in zip(bs_, qs, kks, sls):
                kb = kk.astype(BF16)
                ms = []
                ks = []
                for s_off in range(HG_SUB):
                    bs = jnp.concatenate(
                        [jnp.broadcast_to(b_sc[c, i * HG_SUB + s_off:i * HG_SUB + s_off + 1, sl],
                                          (HG_SUB, HG_D)) for i in range(C // HG_SUB)], axis=0)
                    ms.append((q * jnp.exp2(jnp.minimum(b - bs, 0.0))).astype(BF16))
                    ks.append(kb * sub_keep[s_off])
                ad = _dot_nt(jnp.concatenate(ms, axis=1), jnp.concatenate(ks, axis=1))
                a_.append(jnp.where(diag_mask, ad, 0.0))
            for m_blk, mask in levels:
                qes = []
                kes = []
                for b, q, kk in zip(bs_, qs, kks):
                    d = b - _pair_reference(b, m_blk)
                    qes.append((q * jnp.exp2(d)).astype(BF16))
                    kes.append((kk * jnp.exp2(-d)).astype(BF16))
                al = _dot_nt(jnp.concatenate(qes, axis=1), pair_nt(*kes))
                a_ = [jnp.where(mask, al[:, j * C:(j + 1) * C], a_[j]) for j in range(2)]

            sts = [st_ref[hd] for hd in heads]
            qd = jnp.concatenate([(q * jnp.exp2(b)).astype(BF16) for b, q in zip(bs_, qs)], axis=1)
            a2 = jnp.concatenate([a.astype(BF16) for a in a_], axis=1)
            v2 = jnp.concatenate([jnp.concatenate([vs[0], zero], axis=1),
                                  jnp.concatenate([zero, vs[1]], axis=1)], axis=0)
            o2 = _dot(a2, v2) + _dot_nt(qd, pair_nt(*[st.astype(BF16) for st in sts]))

            for j, hd in enumerate(heads):
                b, kk, v, sl = bs_[j], kks[j], vs[j], sls[j]
                b_last = b[C - 1:C, :]
                kd = (kk * jnp.exp2(b_last - b)).astype(BF16)
                vt = v.astype(F32).T.astype(BF16)
                st_ref[hd] = sts[j] * jnp.exp2(b_last) + _dot(vt, kd)

                on = _rms(o2[:, j * HG_D:(j + 1) * HG_D], gain_ref[:, sl], HG_D)
                o_ref[0, pl.ds(r0, C), sl] = (on * g_all[:, sl]).astype(o_ref.dtype)

    for c in range(n_chunks):
        chunk(c)


def _out_ffn_kernel(x_ref, ymla_ref, yhg_ref, ymem_ref, gmla_ref, w_out_ref, gffn_ref,
                    w_gate_ref, w_up_ref, w_down_ref, o_ref):
    x = x_ref[...].astype(F32)
    ymla = _rms(ymla_ref[...].astype(F32), gmla_ref[...], WIDTH).astype(BF16)
    mix = (_dot(ymla, w_out_ref[0:WIDTH, :])
           + _dot(yhg_ref[...], w_out_ref[WIDTH:2 * WIDTH, :])
           + _dot(ymem_ref[...], w_out_ref[2 * WIDTH:3 * WIDTH, :]))
    x1 = x + mix
    h2 = _rms(x1, gffn_ref[...], x1.shape[-1]).astype(BF16)
    g = _dot(h2, w_gate_ref[...])
    u = _dot(h2, w_up_ref[...])
    act = (g * _sigmoid(g) * u).astype(BF16)
    o_ref[...] = (x1 + _dot(act, w_down_ref[...])).astype(o_ref.dtype)


def _full(shape):
    nd = len(shape)
    return pl.BlockSpec(shape, lambda *_: (0,) * nd)


def _params(sem):
    return pltpu.CompilerParams(dimension_semantics=sem, vmem_limit_bytes=VMEM_LIMIT)


def _row(v):
    return v.reshape(1, -1).astype(F32)


def _layer(x, mem, positions, layer, norm_mix, norm_mem, w_in, q_a_norm, w_uq, kv_a_norm, w_ukv,
           mla_q_norm, mla_k_norm, hg_lb_logits, hg_out_norm, w_mem_kv, mem_q_norm, mem_k_norm,
           mla_out_norm, mem_out_norm, w_out, norm_ffn, w_gate, w_up, w_down):
    B, S, D = x.shape
    M = mem.shape[1]
    T = B * S
    half = D_ROPE // 2
    H = N_HEADS

    assert w_in.shape == (D, W_ALL - D_ROPE)
    wrows = 256
    w_all = pl.pallas_call(
        _w_in_kernel,
        grid=(D // wrows,),
        in_specs=[pl.BlockSpec((w_in.shape[1], wrows), lambda i: (0, i))],
        out_specs=pl.BlockSpec((wrows, W_ALL), lambda i: (i, 0)),
        out_shape=jax.ShapeDtypeStruct((D, W_ALL), BF16),
        compiler_params=_params(("arbitrary",)),
        name="w_in_layout",
    )(w_in.T)
    uq = w_uq.reshape(Q_LORA, H, D_QK)
    uq2 = jnp.concatenate([uq, -uq[:, :, D_NOPE + half:], uq[:, :, D_NOPE:D_NOPE + half]], axis=2)
    ukv = w_ukv.reshape(KV_LORA, H, D_NOPE + D_V)
    w_uk = ukv[:, :, :D_NOPE].reshape(KV_LORA, H * D_NOPE)
    w_uv = ukv[:, :, D_NOPE:].reshape(KV_LORA, H * D_V)
    bf = lambda w: w.astype(BF16)

    pad = jnp.zeros((LANE - D_ROPE,), F32)

    def rotary_gains(g):
        g = g.astype(F32)
        g1 = jnp.concatenate([g[D_NOPE:], pad]).reshape(1, LANE)
        g2 = jnp.concatenate([g[D_NOPE + half:], g[D_NOPE:D_NOPE + half], pad]).reshape(1, LANE)
        return g[:D_NOPE].reshape(1, D_NOPE), g1, g2

    gq_nope, gq_r1, gq_r2 = rotary_gains(mla_q_norm)
    gk_nope, gk_r1, gk_r2 = rotary_gains(mla_k_norm)
    inv_freq = jnp.power(ROPE_THETA, -jnp.arange(half, dtype=F32) / half)
    invf = jnp.tile(inv_freq, LANE // half).reshape(1, LANE)
    phase = jnp.concatenate([jnp.zeros((D_ROPE,), F32), jnp.full((D_ROPE,), -jnp.pi / 2, F32)]).reshape(1, LANE)

    kmem, vmem = pl.pallas_call(
        _mem_kv_kernel,
        grid=(B,),
        in_specs=[pl.BlockSpec((1, M, D), lambda b: (b, 0, 0)),
                  _full((1, D)), _full((D, 2 * WIDTH)), _full((1, MEM_D))],
        out_specs=[pl.BlockSpec((1, M, WIDTH), lambda b: (b, 0, 0))] * 2,
        out_shape=[jax.ShapeDtypeStruct((B, M, WIDTH), BF16)] * 2,
        compiler_params=_params(("arbitrary",)),
        name="mem_kv",
    )(mem, _row(norm_mem), bf(w_mem_kv), _row(mem_k_norm))

    tm = min(512, S)
    assert S % tm == 0
    steps_per_batch = S // tm
    x2 = x.reshape(T, D)
    assert tm % LANE == 0
    pos2 = positions.reshape(T // tm, tm // LANE, LANE).astype(jnp.int32)
    pos_spec = pl.BlockSpec((1, tm // LANE, LANE), lambda i: (i, 0, 0))
    n_tok_steps = T // tm
    pos_next_spec = pl.BlockSpec((1, tm // LANE, LANE), lambda i: (jnp.minimum(i + 1, n_tok_steps - 1), 0, 0))
    sign = jnp.concatenate([jnp.full((D_ROPE,), -1.0, F32), jnp.ones((D_ROPE,), F32)]).reshape(1, LANE)
    tok = lambda w: pl.BlockSpec((tm, w), lambda i: (i, 0))
    col = lambda width, start: pl.BlockSpec((D, width), lambda i: (0, start // width))
    weight_specs = [col(WIDTH, 0), col(KV_LORA, 6 * WIDTH), col(WIDTH, WIDTH), col(WIDTH, 2 * WIDTH),
                    col(WIDTH, 3 * WIDTH), col(WIDTH, 4 * WIDTH), col(WIDTH, 5 * WIDTH)]
    weights = [w_all] * len(weight_specs)
    rest = [_row(q_a_norm), bf(uq2.reshape(Q_LORA, H * D_QK_PAD)), _row(kv_a_norm), bf(w_uk), bf(w_uv),
            gq_nope, gq_r1, gq_r2, gk_nope, gk_r1, gk_r2, _row(mem_q_norm)]
    mem_spec = pl.BlockSpec((1, M, WIDTH), lambda i: (i // steps_per_batch, 0, 0))
    vt_spec = pl.BlockSpec((1, H, 1, D_V, tm),
                           lambda i: (i // steps_per_batch, 0, i % steps_per_batch, 0, 0))
    q_all, k_all, vt_all, hq, hf, hi, hg, ymem = pl.pallas_call(
        _in_proj_kernel,
        grid=(T // tm,),
        in_specs=([tok(D), pos_spec, pos_next_spec, _full((1, LANE)), _full((1, LANE)), _full((1, LANE)),
                   _full((1, D))]
                  + weight_specs + [_full(r.shape) for r in rest]
                  + [mem_spec, mem_spec, _full((1, WIDTH))]),
        out_specs=[tok(H * D_QK_PAD), tok(H * D_QK_PAD), vt_spec] + [tok(WIDTH)] * 5,
        out_shape=[jax.ShapeDtypeStruct((T, H * D_QK_PAD), BF16)] * 2
        + [jax.ShapeDtypeStruct((B, H, steps_per_batch, D_V, tm), BF16)]
        + [jax.ShapeDtypeStruct((T, WIDTH), dt) for dt in (BF16, F32, BF16, BF16, BF16)],
        scratch_shapes=[pltpu.VMEM((tm, LANE), F32)] * 3,
        compiler_params=_params(("arbitrary",)),
        name="in_proj",
    )(x2, pos2, pos2, invf, phase, sign, _row(norm_mix), *weights, *rest, kmem, vmem, _row(mem_out_norm))

    bk = tm
    bq = 2 * bk
    assert S % bq == 0
    hpb = 2
    y_mla = pl.pallas_call(
        functools.partial(_attn_kernel, bq=bq, bk=bk, heads=hpb),
        grid=(B, H // hpb, S // bq),
        in_specs=[pl.BlockSpec((1, bq, hpb * D_QK_PAD), lambda b, h, i: (b, i, h)),
                  pl.BlockSpec((1, S, hpb * D_QK_PAD), lambda b, h, i: (b, 0, h)),
                  pl.BlockSpec((1, hpb, S // bk, D_V, bk), lambda b, h, i: (b, h, 0, 0, 0))],
        out_specs=pl.BlockSpec((1, bq, hpb * D_V), lambda b, h, i: (b, i, h)),
        out_shape=jax.ShapeDtypeStruct((B, S, H * D_V), BF16),
        scratch_shapes=[pltpu.VMEM((hpb, 1, bq), F32),
                        pltpu.VMEM((hpb, D_V + ONES_ROWS, bq), F32),
                        pltpu.VMEM((hpb, bk, bq), F32), pltpu.VMEM((hpb, bk, bq), F32),
                        pltpu.VMEM((hpb, 1, bq), F32), pltpu.VMEM((hpb, 1, bq), F32)],
        compiler_params=_params(("arbitrary", "arbitrary", "arbitrary")),
        name="mla_attn",
    )(q_all.reshape(B, S, H * D_QK_PAD), k_all.reshape(B, S, H * D_QK_PAD), vt_all)

    ts = min(1024, S)
    assert S % ts == 0 and ts % HG_CHUNK == 0
    seq = lambda: pl.BlockSpec((1, ts, WIDTH), lambda b, i: (b, i, 0))
    n_layers = hg_lb_logits.shape[0]
    y_hg = pl.pallas_call(
        functools.partial(_hgrn_kernel, n_chunks=ts // HG_CHUNK, layer=layer),
        grid=(B, S // ts),
        in_specs=[seq(), seq(), seq(), seq(), _full((n_layers, WIDTH)), _full((1, WIDTH))],
        out_specs=seq(),
        out_shape=jax.ShapeDtypeStruct((B, S, WIDTH), BF16),
        scratch_shapes=[pltpu.VMEM((N_HEADS, HG_D, HG_D), F32),
                        pltpu.VMEM((ts // HG_CHUNK, HG_CHUNK, WIDTH), F32)],
        compiler_params=_params(("arbitrary", "arbitrary")),
        name="hgrn",
    )(hq.reshape(B, S, WIDTH), hf.reshape(B, S, WIDTH), hi.reshape(B, S, WIDTH), hg.reshape(B, S, WIDTH),
      hg_lb_logits.astype(F32), _row(hg_out_norm))

    d_ff = w_gate.shape[1]
    once = lambda shape: pl.BlockSpec(shape, lambda i: (0, 0), pipeline_mode=pl.Buffered(1))
    out = pl.pallas_call(
        _out_ffn_kernel,
        grid=(T // tm,),
        in_specs=[tok(D), tok(WIDTH), tok(WIDTH), tok(WIDTH), _full((1, WIDTH)), once((3 * WIDTH, D)),
                  _full((1, D)), once((D, d_ff)), once((D, d_ff)), once((d_ff, D))],
        out_specs=tok(D),
        out_shape=jax.ShapeDtypeStruct((T, D), x.dtype),
        compiler_params=_params(("arbitrary",)),
        name="out_ffn",
    )(x2, y_mla.reshape(T, WIDTH), y_hg.reshape(T, WIDTH), ymem, _row(mla_out_norm), bf(w_out),
      _row(norm_ffn), bf(w_gate), bf(w_up), bf(w_down))
    return out.reshape(B, S, D)


def kernel(x, mem, positions, norm_mix, norm_mem, w_in, q_a_norm, w_uq, kv_a_norm, w_ukv, mla_q_norm, mla_k_norm, hg_lb_logits, hg_out_norm, w_mem_kv, mem_q_norm, mem_k_norm, mla_out_norm, mem_out_norm, w_out, norm_ffn, w_gate, w_up, w_down):
    depth = w_in.shape[0]
    for l in range(depth):
        x = _layer(x, mem, positions, l, norm_mix[l], norm_mem[l], w_in[l], q_a_norm[l], w_uq[l],
                   kv_a_norm[l], w_ukv[l], mla_q_norm[l], mla_k_norm[l], hg_lb_logits, hg_out_norm[l],
                   w_mem_kv[l], mem_q_norm[l], mem_k_norm[l], mla_out_norm[l], mem_out_norm[l],
                   w_out[l], norm_ffn[l], w_gate[l], w_up[l], w_down[l])
    return x
```

```python
import functools

import jax
import jax.numpy as jnp
from jax import lax
from jax.experimental import pallas as pl
from jax.experimental.pallas import tpu as pltpu

F32 = jnp.float32
BF16 = jnp.bfloat16

EPS = 1e-6
N_HEADS = 4
D_NOPE = 128
D_ROPE = 64
D_QK = D_NOPE + D_ROPE
D_QK_PAD = 256
D_V = 128
Q_LORA = 384
KV_LORA = 256
ROPE_THETA = 10000.0
LOG2E = 1.4426950408889634
HG_D = 128
MEM_D = 128
WIDTH = N_HEADS * 128

LANE = 128
HG_CHUNK = 128
HG_SUB = 8
ONES_ROWS = 16

VMEM_LIMIT = 56 * 1024 * 1024


def _dot(a, b):
    return jnp.dot(a, b, preferred_element_type=F32)


def _dot_nt(a, b):
    return lax.dot_general(a, b, (((1,), (1,)), ((), ())), preferred_element_type=F32)


def _rms(x, g, width):
    ss = jnp.sum(x * x, axis=-1, keepdims=True)
    return x * lax.rsqrt(ss * (1.0 / width) + EPS) * g


def _sigmoid(x):
    return 1.0 / (1.0 + jnp.exp(-x))


def _mem_kv_kernel(mem_ref, g_ref, w_ref, kn_ref, k_out, v_out):
    m = mem_ref[0].astype(F32)
    mh = _rms(m, g_ref[...], m.shape[-1]).astype(BF16)
    kv = _dot(mh, w_ref[...])
    for h in range(N_HEADS):
        kh = kv[:, h * MEM_D:(h + 1) * MEM_D]
        k_out[0, :, h * MEM_D:(h + 1) * MEM_D] = _rms(kh, kn_ref[...], MEM_D).astype(BF16)
    v_out[0] = kv[:, WIDTH:].astype(BF16)


W_ALL = Q_LORA + 2 * D_ROPE + 5 * WIDTH + KV_LORA


def _w_in_kernel(wt_ref, o_ref):
    lo = Q_LORA + KV_LORA
    half = D_ROPE // 2

    def put(dst, rows):
        o_ref[:, dst:dst + rows.shape[0]] = rows.T.astype(BF16)

    for c in range(0, Q_LORA, LANE):
        put(c, wt_ref[c:c + LANE, :])
    put(Q_LORA, jnp.concatenate([wt_ref[lo:lo + D_ROPE, :], -wt_ref[lo + half:lo + D_ROPE, :],
                                 wt_ref[lo:lo + half, :]], axis=0))
    for c in range(0, 5 * WIDTH, LANE):
        put(Q_LORA + LANE + c, wt_ref[lo + D_ROPE + c:lo + D_ROPE + c + LANE, :])
    for c in range(0, KV_LORA, LANE):
        put(Q_LORA + LANE + 5 * WIDTH + c, wt_ref[Q_LORA + c:Q_LORA + c + LANE, :])


def _in_proj_kernel(x_ref, pos_ref, pos_next_ref, invf_ref, phase_ref, sign_ref, gmix_ref,
                    w_cqkr_ref, w_ckv_ref, w_hq_ref, w_hf_ref, w_hi_ref, w_hg_ref, w_mq_ref,
                    gqa_ref, w_uq_ref, gkva_ref, w_uk_ref, w_uv_ref,
                    gq_nope_ref, gq_r1_ref, gq_r2_ref, gk_nope_ref, gk_r1_ref, gk_r2_ref,
                    gmq_ref, kmem_ref, vmem_ref, gmo_ref,
                    q_out, k_out, vt_out, hq_out, hf_out, hi_out, hg_out, ymem_out,
                    cs_sc, tc_sc, ts_sc):
    tm = x_ref.shape[0]

    def fill_cs(p_ref):
        posi = p_ref[0]
        first = jnp.broadcast_to(posi[0:1, 0:1], posi.shape)
        step = (lax.broadcasted_iota(jnp.int32, posi.shape, 0) * LANE
                + lax.broadcasted_iota(jnp.int32, posi.shape, 1))
        gap = jnp.max(jnp.abs((posi - first - step).astype(F32)))

        @pl.when(gap == 0.0)
        def _():
            p0 = jnp.broadcast_to(posi[0:1, 0:1], (8, LANE)).astype(F32)
            a = jnp.cos(p0 * invf_ref[...] + phase_ref[...])
            b = pltpu.roll(a, D_ROPE, axis=1) * sign_ref[...]
            cs_sc[...] = a[0:1] * tc_sc[...] + b[0:1] * ts_sc[...]

        @pl.when(gap != 0.0)
        def _():
            posf = posi.astype(F32)
            pos = jnp.concatenate([jnp.broadcast_to(posf[r:r + 1, :], (LANE, LANE)).T
                                   for r in range(posf.shape[0])], axis=0)
            cs_sc[...] = jnp.cos(pos * invf_ref[...] + phase_ref[...])

    @pl.when(pl.program_id(0) == 0)
    def _():
        tf = lax.broadcasted_iota(jnp.int32, (tm, LANE), 0).astype(F32) * invf_ref[...]
        tc_sc[...] = jnp.cos(tf)
        ts_sc[...] = jnp.sin(tf)
        fill_cs(pos_ref)

    x = x_ref[...].astype(F32)
    h = _rms(x, gmix_ref[...], x.shape[-1]).astype(BF16)

    big = _dot(h, w_cqkr_ref[...])
    ckv = _dot(h, w_ckv_ref[...])
    hq = _dot(h, w_hq_ref[...])
    cqn = _rms(big[:, :Q_LORA], gqa_ref[...], Q_LORA).astype(BF16)
    ckvn = _rms(ckv, gkva_ref[...], KV_LORA).astype(BF16)
    qa = _dot(cqn, w_uq_ref[...])
    kn = _dot(ckvn, w_uk_ref[...])
    vt = _dot(ckvn, w_uv_ref[...]).T
    mq = _dot(h, w_mq_ref[...])
    hf_out[...] = _dot(h, w_hf_ref[...])

    cs = cs_sc[...]
    sc = pltpu.roll(cs, D_ROPE, axis=1)

    def rotary(tile, g1_ref, g2_ref):
        return tile * (g1_ref[...] * cs) + pltpu.roll(tile, D_ROPE, axis=1) * (g2_ref[...] * sc)

    q_scale = LOG2E * D_QK ** -0.5
    for hd in range(N_HEADS):
        nope = qa[:, hd * D_QK_PAD:hd * D_QK_PAD + D_NOPE]
        tile = qa[:, hd * D_QK_PAD + D_NOPE:(hd + 1) * D_QK_PAD]
        ss = jnp.sum(nope * nope, axis=-1, keepdims=True) + 0.5 * jnp.sum(tile * tile, axis=-1, keepdims=True)
        rinv = lax.rsqrt(ss * (1.0 / D_QK) + EPS) * q_scale
        q_out[:, hd * D_QK_PAD:hd * D_QK_PAD + D_NOPE] = (nope * gq_nope_ref[...] * rinv).astype(BF16)
        q_out[:, hd * D_QK_PAD + D_NOPE:(hd + 1) * D_QK_PAD] = (
            rotary(tile, gq_r1_ref, gq_r2_ref) * rinv).astype(BF16)

    vt_out[0, :, 0] = vt.reshape(N_HEADS, D_V, vt.shape[-1]).astype(BF16)
    ktile = big[:, Q_LORA:]
    ss_r = 0.5 * jnp.sum(ktile * ktile, axis=-1, keepdims=True)
    rot = rotary(ktile, gk_r1_ref, gk_r2_ref)
    for hd in range(N_HEADS):
        a = kn[:, hd * D_NOPE:(hd + 1) * D_NOPE]
        ss = jnp.sum(a * a, axis=-1, keepdims=True) + ss_r
        rinv = lax.rsqrt(ss * (1.0 / D_QK) + EPS)
        k_out[:, hd * D_QK_PAD:hd * D_QK_PAD + D_NOPE] = (a * gk_nope_ref[...] * rinv).astype(BF16)
        k_out[:, hd * D_QK_PAD + D_NOPE:(hd + 1) * D_QK_PAD] = (rot * rinv).astype(BF16)

    ss_ = []
    for hd in range(N_HEADS):
        sl = slice(hd * MEM_D, (hd + 1) * MEM_D)
        qh = (_rms(mq[:, sl], gmq_ref[...], MEM_D) * (MEM_D ** -0.5)).astype(BF16)
        ss_.append(_dot_nt(qh, kmem_ref[0, :, sl]))
    hi_out[...] = _dot(h, w_hi_ref[...]).astype(BF16)
    hq_out[...] = (hq * _sigmoid(hq) * (HG_D ** -0.5)).astype(BF16)
    ys = []
    for hd in range(N_HEADS):
        sl = slice(hd * MEM_D, (hd + 1) * MEM_D)
        s = ss_[hd]
        p = jnp.exp(s - jnp.max(s, axis=-1, keepdims=True))
        l = jnp.sum(p, axis=-1, keepdims=True)
        ys.append(_dot(p.astype(BF16), vmem_ref[0, :, sl]) / l)
    hg = _dot(h, w_hg_ref[...])
    hg_out[...] = (hg * _sigmoid(hg)).astype(BF16)
    y = jnp.concatenate(ys, axis=-1)
    ymem_out[...] = _rms(y, gmo_ref[...], WIDTH).astype(BF16)

    fill_cs(pos_next_ref)


def _attn_kernel(q_ref, k_ref, vt_ref, o_ref, m_sc, acc_sc, s0_sc, s1_sc, cm0_sc, cm1_sc,
                 *, bq, bk, heads):
    qi = pl.program_id(2)
    s_bufs = (s0_sc, s1_sc)
    cm_bufs = (cm0_sc, cm1_sc)
    m_sc[...] = jnp.full(m_sc.shape, -jnp.inf, F32)
    acc_sc[...] = jnp.zeros(acc_sc.shape, F32)

    def scores(t, slot, hd, q0=0):
        r0 = t * bk if isinstance(t, int) else pl.multiple_of(t * bk, bk)
        q = q_ref[0, q0:, hd * D_QK_PAD:(hd + 1) * D_QK_PAD]
        k = k_ref[0, pl.ds(r0, bk), hd * D_QK_PAD:(hd + 1) * D_QK_PAD]
        s = _dot_nt(k, q)
        s_bufs[slot][hd, :, q0:] = s
        if q0 == 0:
            cm_bufs[slot][hd] = jnp.max(s, axis=0, keepdims=True)

    def accumulate(t, slot, hd, key_offset=None, q0=0):
        s = s_bufs[slot][hd, :, q0:]
        if key_offset is not None:
            kv = lax.broadcasted_iota(jnp.int32, s.shape, 0) + (key_offset - q0)
            qq = lax.broadcasted_iota(jnp.int32, s.shape, 1)
            s = jnp.where(kv <= qq, s, -jnp.inf)
            cm = jnp.max(s, axis=0, keepdims=True)
        else:
            cm = cm_bufs[slot][hd]
        m_prev = m_sc[hd, :, q0:]
        m_new = jnp.maximum(m_prev, cm)
        p = jnp.exp2(s - m_new)
        alpha = jnp.exp2(m_prev - m_new)
        vt1 = jnp.concatenate([vt_ref[0, hd, t], jnp.ones((ONES_ROWS, bk), BF16)], axis=0)
        acc_sc[hd, :, q0:] = alpha * acc_sc[hd, :, q0:] + _dot(vt1, p.astype(BF16))
        m_sc[hd, :, q0:] = m_new

    for hd in range(heads):
        scores(0, 0, hd)

    def advance(t, slot):
        for hd in range(heads):
            scores(t + 1, 1 - slot, hd)
            accumulate(t, slot, hd)

    def body(jj, carry):
        advance(2 * jj, 0)
        advance(2 * jj + 1, 1)
        return carry

    lax.fori_loop(0, qi, body, 0)
    for hd in range(heads):
        scores(2 * qi + 1, 1, hd, q0=bk)
        accumulate(2 * qi, 0, hd, key_offset=0)
    for hd in range(heads):
        accumulate(2 * qi + 1, 1, hd, key_offset=bk, q0=bk)

    for hd in range(heads):
        o = (acc_sc[hd, :D_V] / acc_sc[hd, D_V:D_V + 1]).T
        o_ref[0, :, hd * D_V:(hd + 1) * D_V] = o.astype(o_ref.dtype)


def _pair_reference(b, m):
    c = b.shape[0]
    n2 = c // (2 * m)
    br = b.reshape(n2, 2 * m, b.shape[1])
    last = br[:, m - 1:m, :]
    return jnp.broadcast_to(last, br.shape).reshape(b.shape)


def _hgrn_kernel(hq_ref, hf_ref, hi_ref, hg_ref, lbl_ref, gain_ref, o_ref, st_ref, b_sc, *, n_chunks, layer):
    C = HG_CHUNK

    @pl.when(pl.program_id(1) == 0)
    def _():
        st_ref[...] = jnp.zeros(st_ref.shape, F32)

    lg = lbl_ref[...].astype(F32)
    e = jnp.exp(lg - jnp.max(lg, axis=0, keepdims=True))
    lb = jnp.sum(e[:layer + 1], axis=0, keepdims=True) / jnp.sum(e, axis=0, keepdims=True)

    row = lax.broadcasted_iota(jnp.int32, (C, C), 0)
    col = lax.broadcasted_iota(jnp.int32, (C, C), 1)
    tri = (col <= row).astype(BF16)
    diag_mask = (((row ^ col) & ~(HG_SUB - 1)) | jnp.where(col <= row, 0, 1)) == 0
    sub_keep = [jnp.where((row & (HG_SUB - 1)) == s_off, 1.0, 0.0).astype(BF16) for s_off in range(HG_SUB)]
    levels = []
    m = HG_SUB
    while m < C:
        bad = ((row ^ col) & ~(2 * m - 1)) | ((row & m) ^ m) | (col & m)
        levels.append((m, bad == 0))
        m *= 2

    def chunk(c):
        r0 = c * C
        fr = hf_ref[0, pl.ds(r0, C), :]
        f = lb + (1.0 - lb) * _sigmoid(fr)
        logf = jnp.log(f) * LOG2E
        kk_all = 1.0 - f
        t0 = logf.astype(BF16)
        r1 = logf - t0.astype(F32)
        t1 = r1.astype(BF16)
        t2 = (r1 - t1.astype(F32)).astype(BF16)
        b_all = _dot(tri, t0) + _dot(tri, t1) + _dot(tri, t2)
        q_all = hq_ref[0, pl.ds(r0, C), :].astype(F32)
        v_all = hi_ref[0, pl.ds(r0, C), :]
        g_all = hg_ref[0, pl.ds(r0, C), :].astype(F32)
        b_sc[c] = b_all
        zero = jnp.zeros((C, HG_D), BF16)

        def pair_nt(x0, x1):
            return jnp.concatenate([jnp.concatenate([x0, zero], axis=1),
                                    jnp.concatenate([zero, x1], axis=1)], axis=0)

        for h0 in range(0, N_HEADS, 2):
            heads = (h0, h0 + 1)
            sls = [slice(hd * HG_D, (hd + 1) * HG_D) for hd in heads]
            bs_ = [b_all[:, sl] for sl in sls]
            qs = [q_all[:, sl] for sl in sls]
            kks = [kk_all[:, sl] for sl in sls]
            vs = [v_all[:, sl] for sl in sls]

            a_ = []
            for b, q, kk, sl in zip(bs_, qs, kks, sls):
                kb = kk.astype(BF16)
                ms = []
                ks = []
                for s_off in range(HG_SUB):
                    bs = jnp.concatenate(
                        [jnp.broadcast_to(b_sc[c, i * HG_SUB + s_off:i * HG_SUB + s_off + 1, sl],
                                          (HG_SUB, HG_D)) for i in range(C // HG_SUB)], axis=0)
                    ms.append((q * jnp.exp2(jnp.minimum(b - bs, 0.0))).astype(BF16))
                    ks.append(kb * sub_keep[s_off])
                ad = _dot_nt(jnp.concatenate(ms, axis=1), jnp.concatenate(ks, axis=1))
                a_.append(jnp.where(diag_mask, ad, 0.0))
            for m_blk, mask in levels:
                qes = []
                kes = []
                for b, q, kk in zip(bs_, qs, kks):
                    d = b - _pair_reference(b, m_blk)
                    qes.append((q * jnp.exp2(d)).astype(BF16))
                    kes.append((kk * jnp.exp2(-d)).astype(BF16))
                al = _dot_nt(jnp.concatenate(qes, axis=1), pair_nt(*kes))
                a_ = [jnp.where(mask, al[:, j * C:(j + 1) * C], a_[j]) for j in range(2)]

            sts = [st_ref[hd] for hd in heads]
            qd = jnp.concatenate([(q * jnp.exp2(b)).astype(BF16) for b, q in zip(bs_, qs)], axis=1)
            a2 = jnp.concatenate([a.astype(BF16) for a in a_], axis=1)
            v2 = jnp.concatenate([jnp.concatenate([vs[0], zero], axis=1),
                                  jnp.concatenate([zero, vs[1]], axis=1)], axis=0)
            o2 = _dot(a2, v2) + _dot_nt(qd, pair_nt(*[st.astype(BF16) for st in sts]))

            for j, hd in enumerate(heads):
                b, kk, v, sl = bs_[j], kks[j], vs[j], sls[j]
                b_last = b[C - 1:C, :]
                kd = (kk * jnp.exp2(b_last - b)).astype(BF16)
                vt = v.astype(F32).T.astype(BF16)
                st_ref[hd] = sts[j] * jnp.exp2(b_last) + _dot(vt, kd)

                on = _rms(o2[:, j * HG_D:(j + 1) * HG_D], gain_ref[:, sl], HG_D)
                o_ref[0, pl.ds(r0, C), sl] = (on * g_all[:, sl]).astype(o_ref.dtype)

    for c in range(n_chunks):
        chunk(c)


def _out_ffn_kernel(x_ref, ymla_ref, yhg_ref, ymem_ref, gmla_ref, w_out_ref, gffn_ref,
                    w_gate_ref, w_up_ref, w_down_ref, o_ref):
    x = x_ref[...].astype(F32)
    ymla = _rms(ymla_ref[...].astype(F32), gmla_ref[...], WIDTH).astype(BF16)
    mix = (_dot(ymla, w_out_ref[0:WIDTH, :])
           + _dot(yhg_ref[...], w_out_ref[WIDTH:2 * WIDTH, :])
           + _dot(ymem_ref[...], w_out_ref[2 * WIDTH:3 * WIDTH, :]))
    x1 = x + mix
    h2 = _rms(x1, gffn_ref[...], x1.shape[-1]).astype(BF16)
    g = _dot(h2, w_gate_ref[...])
    u = _dot(h2, w_up_ref[...])
    act = (g * _sigmoid(g) * u).astype(BF16)
    o_ref[...] = (x1 + _dot(act, w_down_ref[...])).astype(o_ref.dtype)


def _full(shape):
    nd = len(shape)
    return pl.BlockSpec(shape, lambda *_: (0,) * nd)


def _params(sem):
    return pltpu.CompilerParams(dimension_semantics=sem, vmem_limit_bytes=VMEM_LIMIT)


def _row(v):
    return v.reshape(1, -1).astype(F32)


def _layer(x, mem, positions, layer, norm_mix, norm_mem, w_in, q_a_norm, w_uq, kv_a_norm, w_ukv,
           mla_q_norm, mla_k_norm, hg_lb_logits, hg_out_norm, w_mem_kv, mem_q_norm, mem_k_norm,
           mla_out_norm, mem_out_norm, w_out, norm_ffn, w_gate, w_up, w_down):
    B, S, D = x.shape
    M = mem.shape[1]
    T = B * S
    half = D_ROPE // 2
    H = N_HEADS

    assert w_in.shape == (D, W_ALL - D_ROPE)
    wrows = 256
    w_all = pl.pallas_call(
        _w_in_kernel,
        grid=(D // wrows,),
        in_specs=[pl.BlockSpec((w_in.shape[1], wrows), lambda i: (0, i))],
        out_specs=pl.BlockSpec((wrows, W_ALL), lambda i: (i, 0)),
        out_shape=jax.ShapeDtypeStruct((D, W_ALL), BF16),
        compiler_params=_params(("arbitrary",)),
        name="w_in_layout",
    )(w_in.T)
    uq = w_uq.reshape(Q_LORA, H, D_QK)
    uq2 = jnp.concatenate([uq, -uq[:, :, D_NOPE + half:], uq[:, :, D_NOPE:D_NOPE + half]], axis=2)
    ukv = w_ukv.reshape(KV_LORA, H, D_NOPE + D_V)
    w_uk = ukv[:, :, :D_NOPE].reshape(KV_LORA, H * D_NOPE)
    w_uv = ukv[:, :, D_NOPE:].reshape(KV_LORA, H * D_V)
    bf = lambda w: w.astype(BF16)

    pad = jnp.zeros((LANE - D_ROPE,), F32)

    def rotary_gains(g):
        g = g.astype(F32)
        g1 = jnp.concatenate([g[D_NOPE:], pad]).reshape(1, LANE)
        g2 = jnp.concatenate([g[D_NOPE + half:], g[D_NOPE:D_NOPE + half], pad]).reshape(1, LANE)
        return g[:D_NOPE].reshape(1, D_NOPE), g1, g2

    gq_nope, gq_r1, gq_r2 = rotary_gains(mla_q_norm)
    gk_nope, gk_r1, gk_r2 = rotary_gains(mla_k_norm)
    inv_freq = jnp.power(ROPE_THETA, -jnp.arange(half, dtype=F32) / half)
    invf = jnp.tile(inv_freq, LANE // half).reshape(1, LANE)
    phase = jnp.concatenate([jnp.zeros((D_ROPE,), F32), jnp.full((D_ROPE,), -jnp.pi / 2, F32)]).reshape(1, LANE)

    kmem, vmem = pl.pallas_call(
        _mem_kv_kernel,
        grid=(B,),
        in_specs=[pl.BlockSpec((1, M, D), lambda b: (b, 0, 0)),
                  _full((1, D)), _full((D, 2 * WIDTH)), _full((1, MEM_D))],
        out_specs=[pl.BlockSpec((1, M, WIDTH), lambda b: (b, 0, 0))] * 2,
        out_shape=[jax.ShapeDtypeStruct((B, M, WIDTH), BF16)] * 2,
        compiler_params=_params(("arbitrary",)),
        name="mem_kv",
    )(mem, _row(norm_mem), bf(w_mem_kv), _row(mem_k_norm))

    tm = min(512, S)
    assert S % tm == 0
    steps_per_batch = S // tm
    x2 = x.reshape(T, D)
    assert tm % LANE == 0
    pos2 = positions.reshape(T // tm, tm // LANE, LANE).astype(jnp.int32)
    pos_spec = pl.BlockSpec((1, tm // LANE, LANE), lambda i: (i, 0, 0))
    n_tok_steps = T // tm
    pos_next_spec = pl.BlockSpec((1, tm // LANE, LANE), lambda i: (jnp.minimum(i + 1, n_tok_steps - 1), 0, 0))
    sign = jnp.concatenate([jnp.full((D_ROPE,), -1.0, F32), jnp.ones((D_ROPE,), F32)]).reshape(1, LANE)
    tok = lambda w: pl.BlockSpec((tm, w), lambda i: (i, 0))
    col = lambda width, start: pl.BlockSpec((D, width), lambda i: (0, start // width))
    weight_specs = [col(WIDTH, 0), col(KV_LORA, 6 * WIDTH), col(WIDTH, WIDTH), col(WIDTH, 2 * WIDTH),
                    col(WIDTH, 3 * WIDTH), col(WIDTH, 4 * WIDTH), col(WIDTH, 5 * WIDTH)]
    weights = [w_all] * len(weight_specs)
    rest = [_row(q_a_norm), bf(uq2.reshape(Q_LORA, H * D_QK_PAD)), _row(kv_a_norm), bf(w_uk), bf(w_uv),
            gq_nope, gq_r1, gq_r2, gk_nope, gk_r1, gk_r2, _row(mem_q_norm)]
    mem_spec = pl.BlockSpec((1, M, WIDTH), lambda i: (i // steps_per_batch, 0, 0))
    vt_spec = pl.BlockSpec((1, H, 1, D_V, tm),
                           lambda i: (i // steps_per_batch, 0, i % steps_per_batch, 0, 0))
    q_all, k_all, vt_all, hq, hf, hi, hg, ymem = pl.pallas_call(
        _in_proj_kernel,
        grid=(T // tm,),
        in_specs=([tok(D), pos_spec, pos_next_spec, _full((1, LANE)), _full((1, LANE)), _full((1, LANE)),
                   _full((1, D))]
                  + weight_specs + [_full(r.shape) for r in rest]
                  + [mem_spec, mem_spec, _full((1, WIDTH))]),
        out_specs=[tok(H * D_QK_PAD), tok(H * D_QK_PAD), vt_spec] + [tok(WIDTH)] * 5,
        out_shape=[jax.ShapeDtypeStruct((T, H * D_QK_PAD), BF16)] * 2
        + [jax.ShapeDtypeStruct((B, H, steps_per_batch, D_V, tm), BF16)]
        + [jax.ShapeDtypeStruct((T, WIDTH), dt) for dt in (BF16, F32, BF16, BF16, BF16)],
        scratch_shapes=[pltpu.VMEM((tm, LANE), F32)] * 3,
        compiler_params=_params(("arbitrary",)),
        name="in_proj",
    )(x2, pos2, pos2, invf, phase, sign, _row(norm_mix), *weights, *rest, kmem, vmem, _row(mem_out_norm))

    bk = tm
    bq = 2 * bk
    assert S % bq == 0
    hpb = 2
    y_mla = pl.pallas_call(
        functools.partial(_attn_kernel, bq=bq, bk=bk, heads=hpb),
        grid=(B, H // hpb, S // bq),
        in_specs=[pl.BlockSpec((1, bq, hpb * D_QK_PAD), lambda b, h, i: (b, i, h)),
                  pl.BlockSpec((1, S, hpb * D_QK_PAD), lambda b, h, i: (b, 0, h)),
                  pl.BlockSpec((1, hpb, S // bk, D_V, bk), lambda b, h, i: (b, h, 0, 0, 0))],
        out_specs=pl.BlockSpec((1, bq, hpb * D_V), lambda b, h, i: (b, i, h)),
        out_shape=jax.ShapeDtypeStruct((B, S, H * D_V), BF16),
        scratch_shapes=[pltpu.VMEM((hpb, 1, bq), F32),
                        pltpu.VMEM((hpb, D_V + ONES_ROWS, bq), F32),
                        pltpu.VMEM((hpb, bk, bq), F32), pltpu.VMEM((hpb, bk, bq), F32),
                        pltpu.VMEM((hpb, 1, bq), F32), pltpu.VMEM((hpb, 1, bq), F32)],
        compiler_params=_params(("arbitrary", "arbitrary", "arbitrary")),
        name="mla_attn",
    )(q_all.reshape(B, S, H * D_QK_PAD), k_all.reshape(B, S, H * D_QK_PAD), vt_all)

    ts = min(1024, S)
    assert S % ts == 0 and ts % HG_CHUNK == 0
    seq = lambda: pl.BlockSpec((1, ts, WIDTH), lambda b, i: (b, i, 0))
    n_layers = hg_lb_logits.shape[0]
    y_hg = pl.pallas_call(
        functools.partial(_hgrn_kernel, n_chunks=ts // HG_CHUNK, layer=layer),
        grid=(B, S // ts),
        in_specs=[seq(), seq(), seq(), seq(), _full((n_layers, WIDTH)), _full((1, WIDTH))],
        out_specs=seq(),
        out_shape=jax.ShapeDtypeStruct((B, S, WIDTH), BF16),
        scratch_shapes=[pltpu.VMEM((N_HEADS, HG_D, HG_D), F32),
                        pltpu.VMEM((ts // HG_CHUNK, HG_CHUNK, WIDTH), F32)],
        compiler_params=_params(("arbitrary", "arbitrary")),
        name="hgrn",
    )(hq.reshape(B, S, WIDTH), hf.reshape(B, S, WIDTH), hi.reshape(B, S, WIDTH), hg.reshape(B, S, WIDTH),
      hg_lb_logits.astype(F32), _row(hg_out_norm))

    d_ff = w_gate.shape[1]
    once = lambda shape: pl.BlockSpec(shape, lambda i: (0, 0), pipeline_mode=pl.Buffered(1))
    out = pl.pallas_call(
        _out_ffn_kernel,
        grid=(T // tm,),
        in_specs=[tok(D), tok(WIDTH), tok(WIDTH), tok(WIDTH), _full((1, WIDTH)), once((3 * WIDTH, D)),
                  _full((1, D)), once((D, d_ff)), once((D, d_ff)), once((d_ff, D))],
        out_specs=tok(D),
        out_shape=jax.ShapeDtypeStruct((T, D), x.dtype),
        compiler_params=_params(("arbitrary",)),
        name="out_ffn",
    )(x2, y_mla.reshape(T, WIDTH), y_hg.reshape(T, WIDTH), ymem, _row(mla_out_norm), bf(w_out),
      _row(norm_ffn), bf(w_gate), bf(w_up), bf(w_down))
    return out.reshape(B, S, D)


def kernel(x, mem, positions, norm_mix, norm_mem, w_in, q_a_norm, w_uq, kv_a_norm, w_ukv, mla_q_norm, mla_k_norm, hg_lb_logits, hg_out_norm, w_mem_kv, mem_q_norm, mem_k_norm, mla_out_norm, mem_out_norm, w_out, norm_ffn, w_gate, w_up, w_down):
    depth = w_in.shape[0]
    for l in range(depth):
        x = _layer(x, mem, positions, l, norm_mix[l], norm_mem[l], w_in[l], q_a_norm[l], w_uq[l],
                   kv_a_norm[l], w_ukv[l], mla_q_norm[l], mla_k_norm[l], hg_lb_logits, hg_out_norm[l],
                   w_mem_kv[l], mem_q_norm[l], mem_k_norm[l], mla_out_norm[l], mem_out_norm[l],
                   w_out[l], norm_ffn[l], w_gate[l], w_up[l], w_down[l])
    return x
```

```python
import functools

import jax
import jax.numpy as jnp
from jax import lax
from jax.experimental import pallas as pl
from jax.experimental.pallas import tpu as pltpu

F32 = jnp.float32
BF16 = jnp.bfloat16

EPS = 1e-6
N_HEADS = 4
D_NOPE = 128
D_ROPE = 64
D_QK = D_NOPE + D_ROPE
D_QK_PAD = 256
D_V = 128
Q_LORA = 384
KV_LORA = 256
ROPE_THETA = 10000.0
LOG2E = 1.4426950408889634
HG_D = 128
MEM_D = 128
WIDTH = N_HEADS * 128

LANE = 128
HG_CHUNK = 128
HG_SUB = 8
ONES_ROWS = 16

VMEM_LIMIT = 56 * 1024 * 1024


def _dot(a, b):
    return jnp.dot(a, b, preferred_element_type=F32)


def _dot_nt(a, b):
    return lax.dot_general(a, b, (((1,), (1,)), ((), ())), preferred_element_type=F32)


def _rms(x, g, width):
    ss = jnp.sum(x * x, axis=-1, keepdims=True)
    return x * lax.rsqrt(ss * (1.0 / width) + EPS) * g


def _sigmoid(x):
    return 1.0 / (1.0 + jnp.exp(-x))


def _mem_kv_kernel(mem_ref, g_ref, w_ref, kn_ref, k_out, v_out):
    m = mem_ref[0].astype(F32)
    mh = _rms(m, g_ref[...], m.shape[-1]).astype(BF16)
    kv = _dot(mh, w_ref[...])
    for h in range(N_HEADS):
        kh = kv[:, h * MEM_D:(h + 1) * MEM_D]
        k_out[0, :, h * MEM_D:(h + 1) * MEM_D] = _rms(kh, kn_ref[...], MEM_D).astype(BF16)
    v_out[0] = kv[:, WIDTH:].astype(BF16)


W_ALL = Q_LORA + 2 * D_ROPE + 5 * WIDTH + KV_LORA


def _w_in_kernel(wt_ref, o_ref):
    lo = Q_LORA + KV_LORA
    half = D_ROPE // 2

    def put(dst, rows):
        o_ref[:, dst:dst + rows.shape[0]] = rows.T.astype(BF16)

    for c in range(0, Q_LORA, LANE):
        put(c, wt_ref[c:c + LANE, :])
    put(Q_LORA, jnp.concatenate([wt_ref[lo:lo + D_ROPE, :], -wt_ref[lo + half:lo + D_ROPE, :],
                                 wt_ref[lo:lo + half, :]], axis=0))
    for c in range(0, 5 * WIDTH, LANE):
        put(Q_LORA + LANE + c, wt_ref[lo + D_ROPE + c:lo + D_ROPE + c + LANE, :])
    for c in range(0, KV_LORA, LANE):
        put(Q_LORA + LANE + 5 * WIDTH + c, wt_ref[Q_LORA + c:Q_LORA + c + LANE, :])


def _in_proj_kernel(x_ref, pos_ref, pos_next_ref, invf_ref, phase_ref, sign_ref, gmix_ref,
                    w_cqkr_ref, w_ckv_ref, w_hq_ref, w_hf_ref, w_hi_ref, w_hg_ref, w_mq_ref,
                    gqa_ref, w_uq_ref, gkva_ref, w_uk_ref, w_uv_ref,
                    gq_nope_ref, gq_r1_ref, gq_r2_ref, gk_nope_ref, gk_r1_ref, gk_r2_ref,
                    gmq_ref, kmem_ref, vmem_ref, gmo_ref,
                    q_out, k_out, vt_out, hq_out, hf_out, hi_out, hg_out, ymem_out,
                    cs_sc, tc_sc, ts_sc):
    tm = x_ref.shape[0]

    def fill_cs(p_ref):
        posi = p_ref[0]
        first = jnp.broadcast_to(posi[0:1, 0:1], posi.shape)
        step = (lax.broadcasted_iota(jnp.int32, posi.shape, 0) * LANE
                + lax.broadcasted_iota(jnp.int32, posi.shape, 1))
        gap = jnp.max(jnp.abs((posi - first - step).astype(F32)))

        @pl.when(gap == 0.0)
        def _():
            p0 = jnp.broadcast_to(posi[0:1, 0:1], (8, LANE)).astype(F32)
            a = jnp.cos(p0 * invf_ref[...] + phase_ref[...])
            b = pltpu.roll(a, D_ROPE, axis=1) * sign_ref[...]
            cs_sc[...] = a[0:1] * tc_sc[...] + b[0:1] * ts_sc[...]

        @pl.when(gap != 0.0)
        def _():
            posf = posi.astype(F32)
            pos = jnp.concatenate([jnp.broadcast_to(posf[r:r + 1, :], (LANE, LANE)).T
                                   for r in range(posf.shape[0])], axis=0)
            cs_sc[...] = jnp.cos(pos * invf_ref[...] + phase_ref[...])

    @pl.when(pl.program_id(0) == 0)
    def _():
        tf = lax.broadcasted_iota(jnp.int32, (tm, LANE), 0).astype(F32) * invf_ref[...]
        tc_sc[...] = jnp.cos(tf)
        ts_sc[...] = jnp.sin(tf)
        fill_cs(pos_ref)

    x = x_ref[...].astype(F32)
    h = _rms(x, gmix_ref[...], x.shape[-1]).astype(BF16)

    big = _dot(h, w_cqkr_ref[...])
    ckv = _dot(h, w_ckv_ref[...])
    hq = _dot(h, w_hq_ref[...])
    cqn = _rms(big[:, :Q_LORA], gqa_ref[...], Q_LORA).astype(BF16)
    ckvn = _rms(ckv, gkva_ref[...], KV_LORA).astype(BF16)
    qa = _dot(cqn, w_uq_ref[...])
    kn = _dot(ckvn, w_uk_ref[...])
    vt = _dot(ckvn, w_uv_ref[...]).T
    mq = _dot(h, w_mq_ref[...])
    hf_out[...] = _dot(h, w_hf_ref[...])

    cs = cs_sc[...]
    sc = pltpu.roll(cs, D_ROPE, axis=1)

    def rotary(tile, g1_ref, g2_ref):
        return tile * (g1_ref[...] * cs) + pltpu.roll(tile, D_ROPE, axis=1) * (g2_ref[...] * sc)

    q_scale = LOG2E * D_QK ** -0.5
    for hd in range(N_HEADS):
        nope = qa[:, hd * D_QK_PAD:hd * D_QK_PAD + D_NOPE]
        tile = qa[:, hd * D_QK_PAD + D_NOPE:(hd + 1) * D_QK_PAD]
        ss = jnp.sum(nope * nope, axis=-1, keepdims=True) + 0.5 * jnp.sum(tile * tile, axis=-1, keepdims=True)
        rinv = lax.rsqrt(ss * (1.0 / D_QK) + EPS) * q_scale
        q_out[:, hd * D_QK_PAD:hd * D_QK_PAD + D_NOPE] = (nope * gq_nope_ref[...] * rinv).astype(BF16)
        q_out[:, hd * D_QK_PAD + D_NOPE:(hd + 1) * D_QK_PAD] = (
            rotary(tile, gq_r1_ref, gq_r2_ref) * rinv).astype(BF16)

    vt_out[0, :, 0] = vt.reshape(N_HEADS, D_V, vt.shape[-1]).astype(BF16)
    ktile = big[:, Q_LORA:]
    ss_r = 0.5 * jnp.sum(ktile * ktile, axis=-1, keepdims=True)
    rot = rotary(ktile, gk_r1_ref, gk_r2_ref)
    for hd in range(N_HEADS):
        a = kn[:, hd * D_NOPE:(hd + 1) * D_NOPE]
        ss = jnp.sum(a * a, axis=-1, keepdims=True) + ss_r
        rinv = lax.rsqrt(ss * (1.0 / D_QK) + EPS)
        k_out[:, hd * D_QK_PAD:hd * D_QK_PAD + D_NOPE] = (a * gk_nope_ref[...] * rinv).astype(BF16)
        k_out[:, hd * D_QK_PAD + D_NOPE:(hd + 1) * D_QK_PAD] = (rot * rinv).astype(BF16)

    ss_ = []
    for hd in range(N_HEADS):
        sl = slice(hd * MEM_D, (hd + 1) * MEM_D)
        qh = (_rms(mq[:, sl], gmq_ref[...], MEM_D) * (MEM_D ** -0.5)).astype(BF16)
        ss_.append(_dot_nt(qh, kmem_ref[0, :, sl]))
    hi_out[...] = _dot(h, w_hi_ref[...]).astype(BF16)
    hq_out[...] = (hq * _sigmoid(hq) * (HG_D ** -0.5)).astype(BF16)
    ys = []
    for hd in range(N_HEADS):
        sl = slice(hd * MEM_D, (hd + 1) * MEM_D)
        s = ss_[hd]
        p = jnp.exp(s - jnp.max(s, axis=-1, keepdims=True))
        l = jnp.sum(p, axis=-1, keepdims=True)
        ys.append(_dot(p.astype(BF16), vmem_ref[0, :, sl]) / l)
    hg = _dot(h, w_hg_ref[...])
    hg_out[...] = (hg * _sigmoid(hg)).astype(BF16)
    y = jnp.concatenate(ys, axis=-1)
    ymem_out[...] = _rms(y, gmo_ref[...], WIDTH).astype(BF16)

    fill_cs(pos_next_ref)


def _attn_kernel(q_ref, k_ref, vt_ref, w0_ref, w1_ref, w2_ref, w3_ref,
                 o_ref, w0_out, w1_out, w2_out, w3_out,
                 m_sc, acc_sc, s0_sc, s1_sc, cm0_sc, cm1_sc, *, bq, bk, heads):
    qi = pl.program_id(2)
    for w_ref, w_out in ((w0_ref, w0_out), (w1_ref, w1_out), (w2_ref, w2_out), (w3_ref, w3_out)):
        w_out[...] = w_ref[...].astype(BF16)
    s_bufs = (s0_sc, s1_sc)
    cm_bufs = (cm0_sc, cm1_sc)
    m_sc[...] = jnp.full(m_sc.shape, -jnp.inf, F32)
    acc_sc[...] = jnp.zeros(acc_sc.shape, F32)

    def scores(t, slot, hd, q0=0):
        r0 = t * bk if isinstance(t, int) else pl.multiple_of(t * bk, bk)
        q = q_ref[0, q0:, hd * D_QK_PAD:(hd + 1) * D_QK_PAD]
        k = k_ref[0, pl.ds(r0, bk), hd * D_QK_PAD:(hd + 1) * D_QK_PAD]
        s = _dot_nt(k, q)
        s_bufs[slot][hd, :, q0:] = s
        if q0 == 0:
            cm_bufs[slot][hd] = jnp.max(s, axis=0, keepdims=True)

    def accumulate(t, slot, hd, key_offset=None, q0=0):
        s = s_bufs[slot][hd, :, q0:]
        if key_offset is not None:
            kv = lax.broadcasted_iota(jnp.int32, s.shape, 0) + (key_offset - q0)
            qq = lax.broadcasted_iota(jnp.int32, s.shape, 1)
            s = jnp.where(kv <= qq, s, -jnp.inf)
            cm = jnp.max(s, axis=0, keepdims=True)
        else:
            cm = cm_bufs[slot][hd]
        m_prev = m_sc[hd, :, q0:]
        m_new = jnp.maximum(m_prev, cm)
        p = jnp.exp2(s - m_new)
        alpha = jnp.exp2(m_prev - m_new)
        vt1 = jnp.concatenate([vt_ref[0, hd, t], jnp.ones((ONES_ROWS, bk), BF16)], axis=0)
        acc_sc[hd, :, q0:] = alpha * acc_sc[hd, :, q0:] + _dot(vt1, p.astype(BF16))
        m_sc[hd, :, q0:] = m_new

    for hd in range(heads):
        scores(0, 0, hd)

    def advance(t, slot):
        for hd in range(heads):
            scores(t + 1, 1 - slot, hd)
            accumulate(t, slot, hd)

    def body(jj, carry):
        advance(2 * jj, 0)
        advance(2 * jj + 1, 1)
        return carry

    lax.fori_loop(0, qi, body, 0)
    for hd in range(heads):
        scores(2 * qi + 1, 1, hd, q0=bk)
        accumulate(2 * qi, 0, hd, key_offset=0)
    for hd in range(heads):
        accumulate(2 * qi + 1, 1, hd, key_offset=bk, q0=bk)

    for hd in range(heads):
        o = (acc_sc[hd, :D_V] / acc_sc[hd, D_V:D_V + 1]).T
        o_ref[0, :, hd * D_V:(hd + 1) * D_V] = o.astype(o_ref.dtype)


def _pair_reference(b, m):
    c = b.shape[0]
    n2 = c // (2 * m)
    br = b.reshape(n2, 2 * m, b.shape[1])
    last = br[:, m - 1:m, :]
    return jnp.broadcast_to(last, br.shape).reshape(b.shape)


def _hgrn_kernel(hq_ref, hf_ref, hi_ref, hg_ref, lbl_ref, gain_ref, o_ref, st_ref, b_sc, *, n_chunks, layer):
    C = HG_CHUNK

    @pl.when(pl.program_id(1) == 0)
    def _():
        st_ref[...] = jnp.zeros(st_ref.shape, F32)

    lg = lbl_ref[...].astype(F32)
    e = jnp.exp(lg - jnp.max(lg, axis=0, keepdims=True))
    lb = jnp.sum(e[:layer + 1], axis=0, keepdims=True) / jnp.sum(e, axis=0, keepdims=True)

    row = lax.broadcasted_iota(jnp.int32, (C, C), 0)
    col = lax.broadcasted_iota(jnp.int32, (C, C), 1)
    tri = (col <= row).astype(BF16)
    diag_mask = (((row ^ col) & ~(HG_SUB - 1)) | jnp.where(col <= row, 0, 1)) == 0
    sub_keep = [jnp.where((row & (HG_SUB - 1)) == s_off, 1.0, 0.0).astype(BF16) for s_off in range(HG_SUB)]
    levels = []
    m = HG_SUB
    while m < C:
        bad = ((row ^ col) & ~(2 * m - 1)) | ((row & m) ^ m) | (col & m)
        levels.append((m, bad == 0))
        m *= 2

    def chunk(c):
        r0 = c * C
        fr = hf_ref[0, pl.ds(r0, C), :]
        f = lb + (1.0 - lb) * _sigmoid(fr)
        logf = jnp.log(f) * LOG2E
        kk_all = 1.0 - f
        t0 = logf.astype(BF16)
        r1 = logf - t0.astype(F32)
        t1 = r1.astype(BF16)
        t2 = (r1 - t1.astype(F32)).astype(BF16)
        b_all = _dot(tri, t0) + _dot(tri, t1) + _dot(tri, t2)
        q_all = hq_ref[0, pl.ds(r0, C), :].astype(F32)
        v_all = hi_ref[0, pl.ds(r0, C), :]
        g_all = hg_ref[0, pl.ds(r0, C), :].astype(F32)
        b_sc[c] = b_all
        zero = jnp.zeros((C, HG_D), BF16)

        def pair_nt(x0, x1):
            return jnp.concatenate([jnp.concatenate([x0, zero], axis=1),
                                    jnp.concatenate([zero, x1], axis=1)], axis=0)

        for h0 in range(0, N_HEADS, 2):
            heads = (h0, h0 + 1)
            sls = [slice(hd * HG_D, (hd + 1) * HG_D) for hd in heads]
            bs_ = [b_all[:, sl] for sl in sls]
            qs = [q_all[:, sl] for sl in sls]
            kks = [kk_all[:, sl] for sl in sls]
            vs = [v_all[:, sl] for sl in sls]

            a_ = []
            for b, q, kk, sl in zip(bs_, qs, kks, sls):
                kb = kk.astype(BF16)
                ms = []
                ks = []
                for s_off in range(HG_SUB):
                    bs = jnp.concatenate(
                        [jnp.broadcast_to(b_sc[c, i * HG_SUB + s_off:i * HG_SUB + s_off + 1, sl],
                                          (HG_SUB, HG_D)) for i in range(C // HG_SUB)], axis=0)
                    ms.append((q * jnp.exp2(jnp.minimum(b - bs, 0.0))).astype(BF16))
                    ks.append(kb * sub_keep[s_off])
                ad = _dot_nt(jnp.concatenate(ms, axis=1), jnp.concatenate(ks, axis=1))
                a_.append(jnp.where(diag_mask, ad, 0.0))
            for m_blk, mask in levels:
                qes = []
                kes = []
                for b, q, kk in zip(bs_, qs, kks):
                    d = b - _pair_reference(b, m_blk)
                    qes.append((q * jnp.exp2(d)).astype(BF16))
                    kes.append((kk * jnp.exp2(-d)).astype(BF16))
                al = _dot_nt(jnp.concatenate(qes, axis=1), pair_nt(*kes))
                a_ = [jnp.where(mask, al[:, j * C:(j + 1) * C], a_[j]) for j in range(2)]

            sts = [st_ref[hd] for hd in heads]
            qd = jnp.concatenate([(q * jnp.exp2(b)).astype(BF16) for b, q in zip(bs_, qs)], axis=1)
            a2 = jnp.concatenate([a.astype(BF16) for a in a_], axis=1)
            v2 = jnp.concatenate([jnp.concatenate([vs[0], zero], axis=1),
                                  jnp.concatenate([zero, vs[1]], axis=1)], axis=0)
            o2 = _dot(a2, v2) + _dot_nt(qd, pair_nt(*[st.astype(BF16) for st in sts]))

            for j, hd in enumerate(heads):
                b, kk, v, sl = bs_[j], kks[j], vs[j], sls[j]
                b_last = b[C - 1:C, :]
                kd = (kk * jnp.exp2(b_last - b)).astype(BF16)
                vt = v.astype(F32).T.astype(BF16)
                st_ref[hd] = sts[j] * jnp.exp2(b_last) + _dot(vt, kd)

                on = _rms(o2[:, j * HG_D:(j + 1) * HG_D], gain_ref[:, sl], HG_D)
                o_ref[0, pl.ds(r0, C), sl] = (on * g_all[:, sl]).astype(o_ref.dtype)

    for c in range(n_chunks):
        chunk(c)


def _out_ffn_kernel(x_ref, ymla_ref, yhg_ref, ymem_ref, gmla_ref, w_out_ref, gffn_ref,
                    w_gate_ref, w_up_ref, w_down_ref, o_ref):
    x = x_ref[...].astype(F32)
    ymla = _rms(ymla_ref[...].astype(F32), gmla_ref[...], WIDTH).astype(BF16)
    mix = (_dot(ymla, w_out_ref[0:WIDTH, :])
           + _dot(yhg_ref[...], w_out_ref[WIDTH:2 * WIDTH, :])
           + _dot(ymem_ref[...], w_out_ref[2 * WIDTH:3 * WIDTH, :]))
    x1 = x + mix
    h2 = _rms(x1, gffn_ref[...], x1.shape[-1]).astype(BF16)
    g = _dot(h2, w_gate_ref[...])
    u = _dot(h2, w_up_ref[...])
    act = (g * _sigmoid(g) * u).astype(BF16)
    o_ref[...] = (x1 + _dot(act, w_down_ref[...])).astype(o_ref.dtype)


def _full(shape):
    nd = len(shape)
    return pl.BlockSpec(shape, lambda *_: (0,) * nd)


def _params(sem):
    return pltpu.CompilerParams(dimension_semantics=sem, vmem_limit_bytes=VMEM_LIMIT)


def _row(v):
    return v.reshape(1, -1).astype(F32)


def _layer(x, mem, positions, layer, norm_mix, norm_mem, w_in, q_a_norm, w_uq, kv_a_norm, w_ukv,
           mla_q_norm, mla_k_norm, hg_lb_logits, hg_out_norm, w_mem_kv, mem_q_norm, mem_k_norm,
           mla_out_norm, mem_out_norm, w_out, norm_ffn, w_gate, w_up, w_down):
    B, S, D = x.shape
    M = mem.shape[1]
    T = B * S
    half = D_ROPE // 2
    H = N_HEADS

    assert w_in.shape == (D, W_ALL - D_ROPE)
    wrows = 256
    w_all = pl.pallas_call(
        _w_in_kernel,
        grid=(D // wrows,),
        in_specs=[pl.BlockSpec((w_in.shape[1], wrows), lambda i: (0, i))],
        out_specs=pl.BlockSpec((wrows, W_ALL), lambda i: (i, 0)),
        out_shape=jax.ShapeDtypeStruct((D, W_ALL), BF16),
        compiler_params=_params(("arbitrary",)),
        name="w_in_layout",
    )(w_in.T)
    uq = w_uq.reshape(Q_LORA, H, D_QK)
    uq2 = jnp.concatenate([uq, -uq[:, :, D_NOPE + half:], uq[:, :, D_NOPE:D_NOPE + half]], axis=2)
    ukv = w_ukv.reshape(KV_LORA, H, D_NOPE + D_V)
    w_uk = ukv[:, :, :D_NOPE].reshape(KV_LORA, H * D_NOPE)
    w_uv = ukv[:, :, D_NOPE:].reshape(KV_LORA, H * D_V)
    bf = lambda w: w.astype(BF16)

    pad = jnp.zeros((LANE - D_ROPE,), F32)

    def rotary_gains(g):
        g = g.astype(F32)
        g1 = jnp.concatenate([g[D_NOPE:], pad]).reshape(1, LANE)
        g2 = jnp.concatenate([g[D_NOPE + half:], g[D_NOPE:D_NOPE + half], pad]).reshape(1, LANE)
        return g[:D_NOPE].reshape(1, D_NOPE), g1, g2

    gq_nope, gq_r1, gq_r2 = rotary_gains(mla_q_norm)
    gk_nope, gk_r1, gk_r2 = rotary_gains(mla_k_norm)
    inv_freq = jnp.power(ROPE_THETA, -jnp.arange(half, dtype=F32) / half)
    invf = jnp.tile(inv_freq, LANE // half).reshape(1, LANE)
    phase = jnp.concatenate([jnp.zeros((D_ROPE,), F32), jnp.full((D_ROPE,), -jnp.pi / 2, F32)]).reshape(1, LANE)

    kmem, vmem = pl.pallas_call(
        _mem_kv_kernel,
        grid=(B,),
        in_specs=[pl.BlockSpec((1, M, D), lambda b: (b, 0, 0)),
                  _full((1, D)), _full((D, 2 * WIDTH)), _full((1, MEM_D))],
        out_specs=[pl.BlockSpec((1, M, WIDTH), lambda b: (b, 0, 0))] * 2,
        out_shape=[jax.ShapeDtypeStruct((B, M, WIDTH), BF16)] * 2,
        compiler_params=_params(("arbitrary",)),
        name="mem_kv",
    )(mem, _row(norm_mem), bf(w_mem_kv), _row(mem_k_norm))

    tm = min(512, S)
    assert S % tm == 0
    steps_per_batch = S // tm
    x2 = x.reshape(T, D)
    assert tm % LANE == 0
    pos2 = positions.reshape(T // tm, tm // LANE, LANE).astype(jnp.int32)
    pos_spec = pl.BlockSpec((1, tm // LANE, LANE), lambda i: (i, 0, 0))
    n_tok_steps = T // tm
    pos_next_spec = pl.BlockSpec((1, tm // LANE, LANE), lambda i: (jnp.minimum(i + 1, n_tok_steps - 1), 0, 0))
    sign = jnp.concatenate([jnp.full((D_ROPE,), -1.0, F32), jnp.ones((D_ROPE,), F32)]).reshape(1, LANE)
    tok = lambda w: pl.BlockSpec((tm, w), lambda i: (i, 0))
    col = lambda width, start: pl.BlockSpec((D, width), lambda i: (0, start // width))
    weight_specs = [col(WIDTH, 0), col(KV_LORA, 6 * WIDTH), col(WIDTH, WIDTH), col(WIDTH, 2 * WIDTH),
                    col(WIDTH, 3 * WIDTH), col(WIDTH, 4 * WIDTH), col(WIDTH, 5 * WIDTH)]
    weights = [w_all] * len(weight_specs)
    rest = [_row(q_a_norm), bf(uq2.reshape(Q_LORA, H * D_QK_PAD)), _row(kv_a_norm), bf(w_uk), bf(w_uv),
            gq_nope, gq_r1, gq_r2, gk_nope, gk_r1, gk_r2, _row(mem_q_norm)]
    mem_spec = pl.BlockSpec((1, M, WIDTH), lambda i: (i // steps_per_batch, 0, 0))
    vt_spec = pl.BlockSpec((1, H, 1, D_V, tm),
                           lambda i: (i // steps_per_batch, 0, i % steps_per_batch, 0, 0))
    q_all, k_all, vt_all, hq, hf, hi, hg, ymem = pl.pallas_call(
        _in_proj_kernel,
        grid=(T // tm,),
        in_specs=([tok(D), pos_spec, pos_next_spec, _full((1, LANE)), _full((1, LANE)), _full((1, LANE)),
                   _full((1, D))]
                  + weight_specs + [_full(r.shape) for r in rest]
                  + [mem_spec, mem_spec, _full((1, WIDTH))]),
        out_specs=[tok(H * D_QK_PAD), tok(H * D_QK_PAD), vt_spec] + [tok(WIDTH)] * 5,
        out_shape=[jax.ShapeDtypeStruct((T, H * D_QK_PAD), BF16)] * 2
        + [jax.ShapeDtypeStruct((B, H, steps_per_batch, D_V, tm), BF16)]
        + [jax.ShapeDtypeStruct((T, WIDTH), dt) for dt in (BF16, F32, BF16, BF16, BF16)],
        scratch_shapes=[pltpu.VMEM((tm, LANE), F32)] * 3,
        compiler_params=_params(("arbitrary",)),
        name="in_proj",
    )(x2, pos2, pos2, invf, phase, sign, _row(norm_mix), *weights, *rest, kmem, vmem, _row(mem_out_norm))

    bk = tm
    bq = 2 * bk
    assert S % bq == 0
    hpb = 2
    att_grid = (B, H // hpb, S // bq)
    n_att = att_grid[0] * att_grid[1] * att_grid[2]

    def cast_spec(w):
        rows = w.shape[0]
        rb = next(r for r in range(16 * -(-rows // (16 * n_att)), rows + 1, 16) if rows % r == 0)
        last = rows // rb - 1
        return pl.BlockSpec((rb, w.shape[1]),
                            lambda b, h, i: (jnp.minimum((b * att_grid[1] + h) * att_grid[2] + i, last), 0))

    ffn_weights = [w_out, w_gate, w_up, w_down]
    cast_specs = [cast_spec(w) for w in ffn_weights]
    y_mla, w_out_b, w_gate_b, w_up_b, w_down_b = pl.pallas_call(
        functools.partial(_attn_kernel, bq=bq, bk=bk, heads=hpb),
        grid=att_grid,
        in_specs=[pl.BlockSpec((1, bq, hpb * D_QK_PAD), lambda b, h, i: (b, i, h)),
                  pl.BlockSpec((1, S, hpb * D_QK_PAD), lambda b, h, i: (b, 0, h)),
                  pl.BlockSpec((1, hpb, S // bk, D_V, bk), lambda b, h, i: (b, h, 0, 0, 0))] + cast_specs,
        out_specs=[pl.BlockSpec((1, bq, hpb * D_V), lambda b, h, i: (b, i, h))] + cast_specs,
        out_shape=[jax.ShapeDtypeStruct((B, S, H * D_V), BF16)]
        + [jax.ShapeDtypeStruct(w.shape, BF16) for w in ffn_weights],
        scratch_shapes=[pltpu.VMEM((hpb, 1, bq), F32),
                        pltpu.VMEM((hpb, D_V + ONES_ROWS, bq), F32),
                        pltpu.VMEM((hpb, bk, bq), F32), pltpu.VMEM((hpb, bk, bq), F32),
                        pltpu.VMEM((hpb, 1, bq), F32), pltpu.VMEM((hpb, 1, bq), F32)],
        compiler_params=_params(("arbitrary", "arbitrary", "arbitrary")),
        name="mla_attn",
    )(q_all.reshape(B, S, H * D_QK_PAD), k_all.reshape(B, S, H * D_QK_PAD), vt_all, *ffn_weights)

    ts = min(1024, S)
    assert S % ts == 0 and ts % HG_CHUNK == 0
    seq = lambda: pl.BlockSpec((1, ts, WIDTH), lambda b, i: (b, i, 0))
    n_layers = hg_lb_logits.shape[0]
    y_hg = pl.pallas_call(
        functools.partial(_hgrn_kernel, n_chunks=ts // HG_CHUNK, layer=layer),
        grid=(B, S // ts),
        in_specs=[seq(), seq(), seq(), seq(), _full((n_layers, WIDTH)), _full((1, WIDTH))],
        out_specs=seq(),
        out_shape=jax.ShapeDtypeStruct((B, S, WIDTH), BF16),
        scratch_shapes=[pltpu.VMEM((N_HEADS, HG_D, HG_D), F32),
                        pltpu.VMEM((ts // HG_CHUNK, HG_CHUNK, WIDTH), F32)],
        compiler_params=_params(("arbitrary", "arbitrary")),
        name="hgrn",
    )(hq.reshape(B, S, WIDTH), hf.reshape(B, S, WIDTH), hi.reshape(B, S, WIDTH), hg.reshape(B, S, WIDTH),
      hg_lb_logits.astype(F32), _row(hg_out_norm))

    d_ff = w_gate.shape[1]
    once = lambda shape: pl.BlockSpec(shape, lambda i: (0, 0), pipeline_mode=pl.Buffered(1))
    out = pl.pallas_call(
        _out_ffn_kernel,
        grid=(T // tm,),
        in_specs=[tok(D), tok(WIDTH), tok(WIDTH), tok(WIDTH), _full((1, WIDTH)), once((3 * WIDTH, D)),
                  _full((1, D)), once((D, d_ff)), once((D, d_ff)), once((d_ff, D))],
        out_specs=tok(D),
        out_shape=jax.ShapeDtypeStruct((T, D), x.dtype),
        compiler_params=_params(("arbitrary",)),
        name="out_ffn",
    )(x2, y_mla.reshape(T, WIDTH), y_hg.reshape(T, WIDTH), ymem, _row(mla_out_norm), w_out_b,
      _row(norm_ffn), w_gate_b, w_up_b, w_down_b)
    return out.reshape(B, S, D)


def kernel(x, mem, positions, norm_mix, norm_mem, w_in, q_a_norm, w_uq, kv_a_norm, w_ukv, mla_q_norm, mla_k_norm, hg_lb_logits, hg_out_norm, w_mem_kv, mem_q_norm, mem_k_norm, mla_out_norm, mem_out_norm, w_out, norm_ffn, w_gate, w_up, w_down):
    depth = w_in.shape[0]
    for l in range(depth):
        x = _layer(x, mem, positions, l, norm_mix[l], norm_mem[l], w_in[l], q_a_norm[l], w_uq[l],
                   kv_a_norm[l], w_ukv[l], mla_q_norm[l], mla_k_norm[l], hg_lb_logits, hg_out_norm[l],
                   w_mem_kv[l], mem_q_norm[l], mem_k_norm[l], mla_out_norm[l], mem_out_norm[l],
                   w_out[l], norm_ffn[l], w_gate[l], w_up[l], w_down[l])
    return x
```

```python
import functools

import jax
import jax.numpy as jnp
from jax import lax
from jax.experimental import pallas as pl
from jax.experimental.pallas import tpu as pltpu

F32 = jnp.float32
BF16 = jnp.bfloat16

EPS = 1e-6
N_HEADS = 4
D_NOPE = 128
D_ROPE = 64
D_QK = D_NOPE + D_ROPE
D_QK_PAD = 256
D_V = 128
Q_LORA = 384
KV_LORA = 256
ROPE_THETA = 10000.0
LOG2E = 1.4426950408889634
HG_D = 128
MEM_D = 128
WIDTH = N_HEADS * 128

LANE = 128
HG_CHUNK = 128
HG_SUB = 8
ONES_ROWS = 16

VMEM_LIMIT = 56 * 1024 * 1024


def _dot(a, b):
    return jnp.dot(a, b, preferred_element_type=F32)


def _dot_nt(a, b):
    return lax.dot_general(a, b, (((1,), (1,)), ((), ())), preferred_element_type=F32)


def _rms(x, g, width):
    ss = jnp.sum(x * x, axis=-1, keepdims=True)
    return x * lax.rsqrt(ss * (1.0 / width) + EPS) * g


def _sigmoid(x):
    return 1.0 / (1.0 + jnp.exp(-x))


def _mem_kv_kernel(mem_ref, g_ref, w_ref, kn_ref, k_out, v_out):
    m = mem_ref[0].astype(F32)
    mh = _rms(m, g_ref[...], m.shape[-1]).astype(BF16)
    kv = _dot(mh, w_ref[...])
    for h in range(N_HEADS):
        kh = kv[:, h * MEM_D:(h + 1) * MEM_D]
        k_out[0, :, h * MEM_D:(h + 1) * MEM_D] = _rms(kh, kn_ref[...], MEM_D).astype(BF16)
    v_out[0] = kv[:, WIDTH:].astype(BF16)


W_ALL = Q_LORA + 2 * D_ROPE + 5 * WIDTH + KV_LORA


def _w_in_kernel(wt_ref, o_ref):
    lo = Q_LORA + KV_LORA
    half = D_ROPE // 2

    def put(dst, rows):
        o_ref[:, dst:dst + rows.shape[0]] = rows.T.astype(BF16)

    for c in range(0, Q_LORA, LANE):
        put(c, wt_ref[c:c + LANE, :])
    put(Q_LORA, jnp.concatenate([wt_ref[lo:lo + D_ROPE, :], -wt_ref[lo + half:lo + D_ROPE, :],
                                 wt_ref[lo:lo + half, :]], axis=0))
    for c in range(0, 5 * WIDTH, LANE):
        put(Q_LORA + LANE + c, wt_ref[lo + D_ROPE + c:lo + D_ROPE + c + LANE, :])
    for c in range(0, KV_LORA, LANE):
        put(Q_LORA + LANE + 5 * WIDTH + c, wt_ref[Q_LORA + c:Q_LORA + c + LANE, :])


def _in_proj_kernel(x_ref, pos_ref, pos_next_ref, invf_ref, phase_ref, sign_ref, gmix_ref,
                    w_cqkr_ref, w_ckv_ref, w_hq_ref, w_hf_ref, w_hi_ref, w_hg_ref, w_mq_ref,
                    gqa_ref, w_uq_ref, gkva_ref, w_uk_ref, w_uv_ref,
                    gq_nope_ref, gq_r1_ref, gq_r2_ref, gk_nope_ref, gk_r1_ref, gk_r2_ref,
                    gmq_ref, kmem_ref, vmem_ref, gmo_ref,
                    q_out, k_out, vt_out, hq_out, hf_out, hi_out, hg_out, ymem_out,
                    cs_sc, tc_sc, ts_sc):
    tm = x_ref.shape[0]

    def fill_cs(p_ref):
        posi = p_ref[0]
        first = jnp.broadcast_to(posi[0:1, 0:1], posi.shape)
        step = (lax.broadcasted_iota(jnp.int32, posi.shape, 0) * LANE
                + lax.broadcasted_iota(jnp.int32, posi.shape, 1))
        gap = jnp.max(jnp.abs((posi - first - step).astype(F32)))

        @pl.when(gap == 0.0)
        def _():
            p0 = jnp.broadcast_to(posi[0:1, 0:1], (8, LANE)).astype(F32)
            a = jnp.cos(p0 * invf_ref[...] + phase_ref[...])
            b = pltpu.roll(a, D_ROPE, axis=1) * sign_ref[...]
            cs_sc[...] = a[0:1] * tc_sc[...] + b[0:1] * ts_sc[...]

        @pl.when(gap != 0.0)
        def _():
            posf = posi.astype(F32)
            pos = jnp.concatenate([jnp.broadcast_to(posf[r:r + 1, :], (LANE, LANE)).T
                                   for r in range(posf.shape[0])], axis=0)
            cs_sc[...] = jnp.cos(pos * invf_ref[...] + phase_ref[...])

    @pl.when(pl.program_id(0) == 0)
    def _():
        tf = lax.broadcasted_iota(jnp.int32, (tm, LANE), 0).astype(F32) * invf_ref[...]
        tc_sc[...] = jnp.cos(tf)
        ts_sc[...] = jnp.sin(tf)
        fill_cs(pos_ref)

    x = x_ref[...].astype(F32)
    h = _rms(x, gmix_ref[...], x.shape[-1]).astype(BF16)

    big = _dot(h, w_cqkr_ref[...])
    ckv = _dot(h, w_ckv_ref[...])
    hq = _dot(h, w_hq_ref[...])
    cqn = _rms(big[:, :Q_LORA], gqa_ref[...], Q_LORA).astype(BF16)
    ckvn = _rms(ckv, gkva_ref[...], KV_LORA).astype(BF16)
    qa = _dot(cqn, w_uq_ref[...])
    kn = _dot(ckvn, w_uk_ref[...])
    vt = _dot(ckvn, w_uv_ref[...]).T
    mq = _dot(h, w_mq_ref[...])
    hf_out[...] = _dot(h, w_hf_ref[...])

    cs = cs_sc[...]
    sc = pltpu.roll(cs, D_ROPE, axis=1)

    def rotary(tile, g1_ref, g2_ref):
        return tile * (g1_ref[...] * cs) + pltpu.roll(tile, D_ROPE, axis=1) * (g2_ref[...] * sc)

    q_scale = LOG2E * D_QK ** -0.5
    for hd in range(N_HEADS):
        nope = qa[:, hd * D_QK_PAD:hd * D_QK_PAD + D_NOPE]
        tile = qa[:, hd * D_QK_PAD + D_NOPE:(hd + 1) * D_QK_PAD]
        ss = jnp.sum(nope * nope, axis=-1, keepdims=True) + 0.5 * jnp.sum(tile * tile, axis=-1, keepdims=True)
        rinv = lax.rsqrt(ss * (1.0 / D_QK) + EPS) * q_scale
        q_out[:, hd * D_QK_PAD:hd * D_QK_PAD + D_NOPE] = (nope * gq_nope_ref[...] * rinv).astype(BF16)
        q_out[:, hd * D_QK_PAD + D_NOPE:(hd + 1) * D_QK_PAD] = (
            rotary(tile, gq_r1_ref, gq_r2_ref) * rinv).astype(BF16)

    vt_out[0, :, 0] = vt.reshape(N_HEADS, D_V, vt.shape[-1]).astype(BF16)
    ktile = big[:, Q_LORA:]
    ss_r = 0.5 * jnp.sum(ktile * ktile, axis=-1, keepdims=True)
    rot = rotary(ktile, gk_r1_ref, gk_r2_ref)
    for hd in range(N_HEADS):
        a = kn[:, hd * D_NOPE:(hd + 1) * D_NOPE]
        ss = jnp.sum(a * a, axis=-1, keepdims=True) + ss_r
        rinv = lax.rsqrt(ss * (1.0 / D_QK) + EPS)
        k_out[:, hd * D_QK_PAD:hd * D_QK_PAD + D_NOPE] = (a * gk_nope_ref[...] * rinv).astype(BF16)
        k_out[:, hd * D_QK_PAD + D_NOPE:(hd + 1) * D_QK_PAD] = (rot * rinv).astype(BF16)

    ss_ = []
    for hd in range(N_HEADS):
        sl = slice(hd * MEM_D, (hd + 1) * MEM_D)
        qh = (_rms(mq[:, sl], gmq_ref[...], MEM_D) * (MEM_D ** -0.5)).astype(BF16)
        ss_.append(_dot_nt(qh, kmem_ref[0, :, sl]))
    hi_out[...] = _dot(h, w_hi_ref[...]).astype(BF16)
    hq_out[...] = (hq * _sigmoid(hq) * (HG_D ** -0.5)).astype(BF16)
    ys = []
    for hd in range(N_HEADS):
        sl = slice(hd * MEM_D, (hd + 1) * MEM_D)
        s = ss_[hd]
        p = jnp.exp(s - jnp.max(s, axis=-1, keepdims=True))
        l = jnp.sum(p, axis=-1, keepdims=True)
        ys.append(_dot(p.astype(BF16), vmem_ref[0, :, sl]) / l)
    hg = _dot(h, w_hg_ref[...])
    hg_out[...] = (hg * _sigmoid(hg)).astype(BF16)
    y = jnp.concatenate(ys, axis=-1)
    ymem_out[...] = _rms(y, gmo_ref[...], WIDTH).astype(BF16)

    fill_cs(pos_next_ref)


def _attn_kernel(q_ref, k_ref, vt_ref, w0_ref, w1_ref, w2_ref, w3_ref,
                 o_ref, w0_out, w1_out, w2_out, w3_out,
                 m_sc, acc_sc, s0_sc, s1_sc, cm0_sc, cm1_sc, *, bq, bk, heads):
    qi = pl.program_id(2)
    for w_ref, w_out in ((w0_ref, w0_out), (w1_ref, w1_out), (w2_ref, w2_out), (w3_ref, w3_out)):
        w_out[...] = w_ref[...].astype(BF16)
    s_bufs = (s0_sc, s1_sc)
    cm_bufs = (cm0_sc, cm1_sc)
    m_sc[...] = jnp.full(m_sc.shape, -jnp.inf, F32)
    acc_sc[...] = jnp.zeros(acc_sc.shape, F32)

    def scores(t, slot, hd, q0=0):
        r0 = t * bk if isinstance(t, int) else pl.multiple_of(t * bk, bk)
        q = q_ref[0, q0:, hd * D_QK_PAD:(hd + 1) * D_QK_PAD]
        k = k_ref[0, pl.ds(r0, bk), hd * D_QK_PAD:(hd + 1) * D_QK_PAD]
        s = _dot_nt(k, q)
        s_bufs[slot][hd, :, q0:] = s
        if q0 == 0:
            cm_bufs[slot][hd] = jnp.max(s, axis=0, keepdims=True)

    def accumulate(t, slot, hd, key_offset=None, q0=0):
        s = s_bufs[slot][hd, :, q0:]
        if key_offset is not None:
            kv = lax.broadcasted_iota(jnp.int32, s.shape, 0) + (key_offset - q0)
            qq = lax.broadcasted_iota(jnp.int32, s.shape, 1)
            s = jnp.where(kv <= qq, s, -jnp.inf)
            cm = jnp.max(s, axis=0, keepdims=True)
        else:
            cm = cm_bufs[slot][hd]
        m_prev = m_sc[hd, :, q0:]
        m_new = jnp.maximum(m_prev, cm)
        p = jnp.exp2(s - m_new)
        alpha = jnp.exp2(m_prev - m_new)
        vt1 = jnp.concatenate([vt_ref[0, hd, t], jnp.ones((ONES_ROWS, bk), BF16)], axis=0)
        acc_sc[hd, :, q0:] = alpha * acc_sc[hd, :, q0:] + _dot(vt1, p.astype(BF16))
        m_sc[hd, :, q0:] = m_new

    for hd in range(heads):
        scores(0, 0, hd)

    def advance(t, slot):
        for hd in range(heads):
            scores(t + 1, 1 - slot, hd)
            accumulate(t, slot, hd)

    def pair(jj):
        advance(2 * jj, 0)
        advance(2 * jj + 1, 1)

    def body(j4, carry):
        pair(2 * j4)
        pair(2 * j4 + 1)
        return carry

    lax.fori_loop(0, jnp.right_shift(qi, 1), body, 0)

    @pl.when((qi & 1) == 1)
    def _():
        pair(qi - 1)
    for hd in range(heads):
        scores(2 * qi + 1, 1, hd, q0=bk)
        accumulate(2 * qi, 0, hd, key_offset=0)
    for hd in range(heads):
        accumulate(2 * qi + 1, 1, hd, key_offset=bk, q0=bk)

    for hd in range(heads):
        o = (acc_sc[hd, :D_V] / acc_sc[hd, D_V:D_V + 1]).T
        o_ref[0, :, hd * D_V:(hd + 1) * D_V] = o.astype(o_ref.dtype)


def _pair_reference(b, m):
    c = b.shape[0]
    n2 = c // (2 * m)
    br = b.reshape(n2, 2 * m, b.shape[1])
    last = br[:, m - 1:m, :]
    return jnp.broadcast_to(last, br.shape).reshape(b.shape)


def _hgrn_kernel(hq_ref, hf_ref, hi_ref, hg_ref, lbl_ref, gain_ref, o_ref, st_ref, b_sc, *, n_chunks, layer):
    C = HG_CHUNK

    @pl.when(pl.program_id(1) == 0)
    def _():
        st_ref[...] = jnp.zeros(st_ref.shape, F32)

    lg = lbl_ref[...].astype(F32)
    e = jnp.exp(lg - jnp.max(lg, axis=0, keepdims=True))
    lb = jnp.sum(e[:layer + 1], axis=0, keepdims=True) / jnp.sum(e, axis=0, keepdims=True)

    row = lax.broadcasted_iota(jnp.int32, (C, C), 0)
    col = lax.broadcasted_iota(jnp.int32, (C, C), 1)
    tri = (col <= row).astype(BF16)
    diag_mask = (((row ^ col) & ~(HG_SUB - 1)) | jnp.where(col <= row, 0, 1)) == 0
    sub_keep = [jnp.where((row & (HG_SUB - 1)) == s_off, 1.0, 0.0).astype(BF16) for s_off in range(HG_SUB)]
    levels = []
    m = HG_SUB
    while m < C:
        bad = ((row ^ col) & ~(2 * m - 1)) | ((row & m) ^ m) | (col & m)
        levels.append((m, bad == 0))
        m *= 2

    def chunk(c):
        r0 = c * C
        fr = hf_ref[0, pl.ds(r0, C), :]
        f = lb + (1.0 - lb) * _sigmoid(fr)
        logf = jnp.log(f) * LOG2E
        kk_all = 1.0 - f
        t0 = logf.astype(BF16)
        r1 = logf - t0.astype(F32)
        t1 = r1.astype(BF16)
        t2 = (r1 - t1.astype(F32)).astype(BF16)
        b_all = _dot(tri, t0) + _dot(tri, t1) + _dot(tri, t2)
        q_all = hq_ref[0, pl.ds(r0, C), :].astype(F32)
        v_all = hi_ref[0, pl.ds(r0, C), :]
        g_all = hg_ref[0, pl.ds(r0, C), :].astype(F32)
        b_sc[c] = b_all
        zero = jnp.zeros((C, HG_D), BF16)

        def pair_nt(x0, x1):
            return jnp.concatenate([jnp.concatenate([x0, zero], axis=1),
                                    jnp.concatenate([zero, x1], axis=1)], axis=0)

        for h0 in range(0, N_HEADS, 2):
            heads = (h0, h0 + 1)
            sls = [slice(hd * HG_D, (hd + 1) * HG_D) for hd in heads]
            bs_ = [b_all[:, sl] for sl in sls]
            qs = [q_all[:, sl] for sl in sls]
            kks = [kk_all[:, sl] for sl in sls]
            vs = [v_all[:, sl] for sl in sls]

            a_ = []
            for b, q, kk, sl in zip(bs_, qs, kks, sls):
                kb = kk.astype(BF16)
                ms = []
                ks = []
                for s_off in range(HG_SUB):
                    bs = jnp.concatenate(
                        [jnp.broadcast_to(b_sc[c, i * HG_SUB + s_off:i * HG_SUB + s_off + 1, sl],
                                          (HG_SUB, HG_D)) for i in range(C // HG_SUB)], axis=0)
                    ms.append((q * jnp.exp2(jnp.minimum(b - bs, 0.0))).astype(BF16))
                    ks.append(kb * sub_keep[s_off])
                ad = _dot_nt(jnp.concatenate(ms, axis=1), jnp.concatenate(ks, axis=1))
                a_.append(jnp.where(diag_mask, ad, 0.0))
            for m_blk, mask in levels:
                qes = []
                kes = []
                for b, q, kk in zip(bs_, qs, kks):
                    d = b - _pair_reference(b, m_blk)
                    qes.append((q * jnp.exp2(d)).astype(BF16))
                    kes.append((kk * jnp.exp2(-d)).astype(BF16))
                al = _dot_nt(jnp.concatenate(qes, axis=1), pair_nt(*kes))
                a_ = [jnp.where(mask, al[:, j * C:(j + 1) * C], a_[j]) for j in range(2)]

            sts = [st_ref[hd] for hd in heads]
            qd = jnp.concatenate([(q * jnp.exp2(b)).astype(BF16) for b, q in zip(bs_, qs)], axis=1)
            a2 = jnp.concatenate([a.astype(BF16) for a in a_], axis=1)
            v2 = jnp.concatenate([jnp.concatenate([vs[0], zero], axis=1),
                                  jnp.concatenate([zero, vs[1]], axis=1)], axis=0)
            o2 = _dot(a2, v2) + _dot_nt(qd, pair_nt(*[st.astype(BF16) for st in sts]))

            for j, hd in enumerate(heads):
                b, kk, v, sl = bs_[j], kks[j], vs[j], sls[j]
                b_last = b[C - 1:C, :]
                kd = (kk * jnp.exp2(b_last - b)).astype(BF16)
                vt = v.astype(F32).T.astype(BF16)
                st_ref[hd] = sts[j] * jnp.exp2(b_last) + _dot(vt, kd)

                on = _rms(o2[:, j * HG_D:(j + 1) * HG_D], gain_ref[:, sl], HG_D)
                o_ref[0, pl.ds(r0, C), sl] = (on * g_all[:, sl]).astype(o_ref.dtype)

    for c in range(n_chunks):
        chunk(c)


def _out_ffn_kernel(x_ref, ymla_ref, yhg_ref, ymem_ref, gmla_ref, w_out_ref, gffn_ref,
                    w_gate_ref, w_up_ref, w_down_ref, o_ref):
    x = x_ref[...].astype(F32)
    ymla = _rms(ymla_ref[...].astype(F32), gmla_ref[...], WIDTH).astype(BF16)
    mix = (_dot(ymla, w_out_ref[0:WIDTH, :])
           + _dot(yhg_ref[...], w_out_ref[WIDTH:2 * WIDTH, :])
           + _dot(ymem_ref[...], w_out_ref[2 * WIDTH:3 * WIDTH, :]))
    x1 = x + mix
    h2 = _rms(x1, gffn_ref[...], x1.shape[-1]).astype(BF16)
    g = _dot(h2, w_gate_ref[...])
    u = _dot(h2, w_up_ref[...])
    act = (g * _sigmoid(g) * u).astype(BF16)
    o_ref[...] = (x1 + _dot(act, w_down_ref[...])).astype(o_ref.dtype)


def _full(shape):
    nd = len(shape)
    return pl.BlockSpec(shape, lambda *_: (0,) * nd)


def _params(sem):
    return pltpu.CompilerParams(dimension_semantics=sem, vmem_limit_bytes=VMEM_LIMIT)


def _row(v):
    return v.reshape(1, -1).astype(F32)


def _layer(x, mem, positions, layer, norm_mix, norm_mem, w_in, q_a_norm, w_uq, kv_a_norm, w_ukv,
           mla_q_norm, mla_k_norm, hg_lb_logits, hg_out_norm, w_mem_kv, mem_q_norm, mem_k_norm,
           mla_out_norm, mem_out_norm, w_out, norm_ffn, w_gate, w_up, w_down):
    B, S, D = x.shape
    M = mem.shape[1]
    T = B * S
    half = D_ROPE // 2
    H = N_HEADS

    assert w_in.shape == (D, W_ALL - D_ROPE)
    wrows = 256
    w_all = pl.pallas_call(
        _w_in_kernel,
        grid=(D // wrows,),
        in_specs=[pl.BlockSpec((w_in.shape[1], wrows), lambda i: (0, i))],
        out_specs=pl.BlockSpec((wrows, W_ALL), lambda i: (i, 0)),
        out_shape=jax.ShapeDtypeStruct((D, W_ALL), BF16),
        compiler_params=_params(("arbitrary",)),
        name="w_in_layout",
    )(w_in.T)
    uq = w_uq.reshape(Q_LORA, H, D_QK)
    uq2 = jnp.concatenate([uq, -uq[:, :, D_NOPE + half:], uq[:, :, D_NOPE:D_NOPE + half]], axis=2)
    ukv = w_ukv.reshape(KV_LORA, H, D_NOPE + D_V)
    w_uk = ukv[:, :, :D_NOPE].reshape(KV_LORA, H * D_NOPE)
    w_uv = ukv[:, :, D_NOPE:].reshape(KV_LORA, H * D_V)
    bf = lambda w: w.astype(BF16)

    pad = jnp.zeros((LANE - D_ROPE,), F32)

    def rotary_gains(g):
        g = g.astype(F32)
        g1 = jnp.concatenate([g[D_NOPE:], pad]).reshape(1, LANE)
        g2 = jnp.concatenate([g[D_NOPE + half:], g[D_NOPE:D_NOPE + half], pad]).reshape(1, LANE)
        return g[:D_NOPE].reshape(1, D_NOPE), g1, g2

    gq_nope, gq_r1, gq_r2 = rotary_gains(mla_q_norm)
    gk_nope, gk_r1, gk_r2 = rotary_gains(mla_k_norm)
    inv_freq = jnp.power(ROPE_THETA, -jnp.arange(half, dtype=F32) / half)
    invf = jnp.tile(inv_freq, LANE // half).reshape(1, LANE)
    phase = jnp.concatenate([jnp.zeros((D_ROPE,), F32), jnp.full((D_ROPE,), -jnp.pi / 2, F32)]).reshape(1, LANE)

    kmem, vmem = pl.pallas_call(
        _mem_kv_kernel,
        grid=(B,),
        in_specs=[pl.BlockSpec((1, M, D), lambda b: (b, 0, 0)),
                  _full((1, D)), _full((D, 2 * WIDTH)), _full((1, MEM_D))],
        out_specs=[pl.BlockSpec((1, M, WIDTH), lambda b: (b, 0, 0))] * 2,
        out_shape=[jax.ShapeDtypeStruct((B, M, WIDTH), BF16)] * 2,
        compiler_params=_params(("arbitrary",)),
        name="mem_kv",
    )(mem, _row(norm_mem), bf(w_mem_kv), _row(mem_k_norm))

    tm = min(512, S)
    assert S % tm == 0
    steps_per_batch = S // tm
    x2 = x.reshape(T, D)
    assert tm % LANE == 0
    pos2 = positions.reshape(T // tm, tm // LANE, LANE).astype(jnp.int32)
    pos_spec = pl.BlockSpec((1, tm // LANE, LANE), lambda i: (i, 0, 0))
    n_tok_steps = T // tm
    pos_next_spec = pl.BlockSpec((1, tm // LANE, LANE), lambda i: (jnp.minimum(i + 1, n_tok_steps - 1), 0, 0))
    sign = jnp.concatenate([jnp.full((D_ROPE,), -1.0, F32), jnp.ones((D_ROPE,), F32)]).reshape(1, LANE)
    tok = lambda w: pl.BlockSpec((tm, w), lambda i: (i, 0))
    col = lambda width, start: pl.BlockSpec((D, width), lambda i: (0, start // width))
    weight_specs = [col(WIDTH, 0), col(KV_LORA, 6 * WIDTH), col(WIDTH, WIDTH), col(WIDTH, 2 * WIDTH),
                    col(WIDTH, 3 * WIDTH), col(WIDTH, 4 * WIDTH), col(WIDTH, 5 * WIDTH)]
    weights = [w_all] * len(weight_specs)
    rest = [_row(q_a_norm), bf(uq2.reshape(Q_LORA, H * D_QK_PAD)), _row(kv_a_norm), bf(w_uk), bf(w_uv),
            gq_nope, gq_r1, gq_r2, gk_nope, gk_r1, gk_r2, _row(mem_q_norm)]
    mem_spec = pl.BlockSpec((1, M, WIDTH), lambda i: (i // steps_per_batch, 0, 0))
    vt_spec = pl.BlockSpec((1, H, 1, D_V, tm),
                           lambda i: (i // steps_per_batch, 0, i % steps_per_batch, 0, 0))
    q_all, k_all, vt_all, hq, hf, hi, hg, ymem = pl.pallas_call(
        _in_proj_kernel,
        grid=(T // tm,),
        in_specs=([tok(D), pos_spec, pos_next_spec, _full((1, LANE)), _full((1, LANE)), _full((1, LANE)),
                   _full((1, D))]
                  + weight_specs + [_full(r.shape) for r in rest]
                  + [mem_spec, mem_spec, _full((1, WIDTH))]),
        out_specs=[tok(H * D_QK_PAD), tok(H * D_QK_PAD), vt_spec] + [tok(WIDTH)] * 5,
        out_shape=[jax.ShapeDtypeStruct((T, H * D_QK_PAD), BF16)] * 2
        + [jax.ShapeDtypeStruct((B, H, steps_per_batch, D_V, tm), BF16)]
        + [jax.ShapeDtypeStruct((T, WIDTH), dt) for dt in (BF16, F32, BF16, BF16, BF16)],
        scratch_shapes=[pltpu.VMEM((tm, LANE), F32)] * 3,
        compiler_params=_params(("arbitrary",)),
        name="in_proj",
    )(x2, pos2, pos2, invf, phase, sign, _row(norm_mix), *weights, *rest, kmem, vmem, _row(mem_out_norm))

    bk = tm
    bq = 2 * bk
    assert S % bq == 0
    hpb = 2
    att_grid = (B, H // hpb, S // bq)
    n_att = att_grid[0] * att_grid[1] * att_grid[2]

    def cast_spec(w):
        rows = w.shape[0]
        rb = next(r for r in range(16 * -(-rows // (16 * n_att)), rows + 1, 16) if rows % r == 0)
        last = rows // rb - 1
        return pl.BlockSpec((rb, w.shape[1]),
                            lambda b, h, i: (jnp.minimum((b * att_grid[1] + h) * att_grid[2] + i, last), 0))

    ffn_weights = [w_out, w_gate, w_up, w_down]
    cast_specs = [cast_spec(w) for w in ffn_weights]
    y_mla, w_out_b, w_gate_b, w_up_b, w_down_b = pl.pallas_call(
        functools.partial(_attn_kernel, bq=bq, bk=bk, heads=hpb),
        grid=att_grid,
        in_specs=[pl.BlockSpec((1, bq, hpb * D_QK_PAD), lambda b, h, i: (b, i, h)),
                  pl.BlockSpec((1, S, hpb * D_QK_PAD), lambda b, h, i: (b, 0, h)),
                  pl.BlockSpec((1, hpb, S // bk, D_V, bk), lambda b, h, i: (b, h, 0, 0, 0))] + cast_specs,
        out_specs=[pl.BlockSpec((1, bq, hpb * D_V), lambda b, h, i: (b, i, h))] + cast_specs,
        out_shape=[jax.ShapeDtypeStruct((B, S, H * D_V), BF16)]
        + [jax.ShapeDtypeStruct(w.shape, BF16) for w in ffn_weights],
        scratch_shapes=[pltpu.VMEM((hpb, 1, bq), F32),
                        pltpu.VMEM((hpb, D_V + ONES_ROWS, bq), F32),
                        pltpu.VMEM((hpb, bk, bq), F32), pltpu.VMEM((hpb, bk, bq), F32),
                        pltpu.VMEM((hpb, 1, bq), F32), pltpu.VMEM((hpb, 1, bq), F32)],
        compiler_params=_params(("arbitrary", "arbitrary", "arbitrary")),
        name="mla_attn",
    )(q_all.reshape(B, S, H * D_QK_PAD), k_all.reshape(B, S, H * D_QK_PAD), vt_all, *ffn_weights)

    ts = min(1024, S)
    assert S % ts == 0 and ts % HG_CHUNK == 0
    seq = lambda: pl.BlockSpec((1, ts, WIDTH), lambda b, i: (b, i, 0))
    n_layers = hg_lb_logits.shape[0]
    y_hg = pl.pallas_call(
        functools.partial(_hgrn_kernel, n_chunks=ts // HG_CHUNK, layer=layer),
        grid=(B, S // ts),
        in_specs=[seq(), seq(), seq(), seq(), _full((n_layers, WIDTH)), _full((1, WIDTH))],
        out_specs=seq(),
        out_shape=jax.ShapeDtypeStruct((B, S, WIDTH), BF16),
        scratch_shapes=[pltpu.VMEM((N_HEADS, HG_D, HG_D), F32),
                        pltpu.VMEM((ts // HG_CHUNK, HG_CHUNK, WIDTH), F32)],
        compiler_params=_params(("arbitrary", "arbitrary")),
        name="hgrn",
    )(hq.reshape(B, S, WIDTH), hf.reshape(B, S, WIDTH), hi.reshape(B, S, WIDTH), hg.reshape(B, S, WIDTH),
      hg_lb_logits.astype(F32), _row(hg_out_norm))

    d_ff = w_gate.shape[1]
    once = lambda shape: pl.BlockSpec(shape, lambda i: (0, 0), pipeline_mode=pl.Buffered(1))
    out = pl.pallas_call(
        _out_ffn_kernel,
        grid=(T // tm,),
        in_specs=[tok(D), tok(WIDTH), tok(WIDTH), tok(WIDTH), _full((1, WIDTH)), once((3 * WIDTH, D)),
                  _full((1, D)), once((D, d_ff)), once((D, d_ff)), once((d_ff, D))],
        out_specs=tok(D),
        out_shape=jax.ShapeDtypeStruct((T, D), x.dtype),
        compiler_params=_params(("arbitrary",)),
        name="out_ffn",
    )(x2, y_mla.reshape(T, WIDTH), y_hg.reshape(T, WIDTH), ymem, _row(mla_out_norm), w_out_b,
      _row(norm_ffn), w_gate_b, w_up_b, w_down_b)
    return out.reshape(B, S, D)


def kernel(x, mem, positions, norm_mix, norm_mem, w_in, q_a_norm, w_uq, kv_a_norm, w_ukv, mla_q_norm, mla_k_norm, hg_lb_logits, hg_out_norm, w_mem_kv, mem_q_norm, mem_k_norm, mla_out_norm, mem_out_norm, w_out, norm_ffn, w_gate, w_up, w_down):
    depth = w_in.shape[0]
    for l in range(depth):
        x = _layer(x, mem, positions, l, norm_mix[l], norm_mem[l], w_in[l], q_a_norm[l], w_uq[l],
                   kv_a_norm[l], w_ukv[l], mla_q_norm[l], mla_k_norm[l], hg_lb_logits, hg_out_norm[l],
                   w_mem_kv[l], mem_q_norm[l], mem_k_norm[l], mla_out_norm[l], mem_out_norm[l],
                   w_out[l], norm_ffn[l], w_gate[l], w_up[l], w_down[l])
    return x
```

```python
import functools

import jax
import jax.numpy as jnp
from jax import lax
from jax.experimental import pallas as pl
from jax.experimental.pallas import tpu as pltpu

F32 = jnp.float32
BF16 = jnp.bfloat16

EPS = 1e-6
N_HEADS = 4
D_NOPE = 128
D_ROPE = 64
D_QK = D_NOPE + D_ROPE
D_QK_PAD = 256
D_V = 128
Q_LORA = 384
KV_LORA = 256
ROPE_THETA = 10000.0
LOG2E = 1.4426950408889634
HG_D = 128
MEM_D = 128
WIDTH = N_HEADS * 128

LANE = 128
HG_CHUNK = 128
HG_SUB = 8
ONES_ROWS = 16

VMEM_LIMIT = 56 * 1024 * 1024


def _dot(a, b):
    return jnp.dot(a, b, preferred_element_type=F32)


def _dot_nt(a, b):
    return lax.dot_general(a, b, (((1,), (1,)), ((), ())), preferred_element_type=F32)


def _rms(x, g, width):
    ss = jnp.sum(x * x, axis=-1, keepdims=True)
    return x * lax.rsqrt(ss * (1.0 / width) + EPS) * g


def _sigmoid(x):
    return 1.0 / (1.0 + jnp.exp(-x))


def _mem_kv_kernel(mem_ref, g_ref, w_ref, kn_ref, k_out, v_out):
    m = mem_ref[0].astype(F32)
    mh = _rms(m, g_ref[...], m.shape[-1]).astype(BF16)
    kv = _dot(mh, w_ref[...])
    for h in range(N_HEADS):
        kh = kv[:, h * MEM_D:(h + 1) * MEM_D]
        k_out[0, :, h * MEM_D:(h + 1) * MEM_D] = _rms(kh, kn_ref[...], MEM_D).astype(BF16)
    v_out[0] = kv[:, WIDTH:].astype(BF16)


W_ALL = Q_LORA + 2 * D_ROPE + 5 * WIDTH + KV_LORA


def _w_in_kernel(wt_ref, uqt_ref, ukv_ref, wmem_ref, o_ref, uq_out, uk_out, uv_out, wmem_out):
    lo = Q_LORA + KV_LORA
    half = D_ROPE // 2

    @pl.when(pl.program_id(0) == 0)
    def _():
        for hd in range(N_HEADS):
            r = hd * D_QK
            uq_out[:, hd * D_QK_PAD:hd * D_QK_PAD + D_NOPE] = uqt_ref[r:r + D_NOPE, :].T.astype(BF16)
            x1 = uqt_ref[r + D_NOPE:r + D_NOPE + half, :]
            x2 = uqt_ref[r + D_NOPE + half:r + D_QK, :]
            uq_out[:, hd * D_QK_PAD + D_NOPE:(hd + 1) * D_QK_PAD] = (
                jnp.concatenate([x1, x2, -x2, x1], axis=0).T.astype(BF16))
            c = hd * (D_NOPE + D_V)
            uk_out[:, hd * D_NOPE:(hd + 1) * D_NOPE] = ukv_ref[:, c:c + D_NOPE].astype(BF16)
            uv_out[:, hd * D_V:(hd + 1) * D_V] = ukv_ref[:, c + D_NOPE:c + D_NOPE + D_V].astype(BF16)
        wmem_out[...] = wmem_ref[...].astype(BF16)

    def put(dst, rows):
        o_ref[:, dst:dst + rows.shape[0]] = rows.T.astype(BF16)

    for c in range(0, Q_LORA, LANE):
        put(c, wt_ref[c:c + LANE, :])
    put(Q_LORA, jnp.concatenate([wt_ref[lo:lo + D_ROPE, :], -wt_ref[lo + half:lo + D_ROPE, :],
                                 wt_ref[lo:lo + half, :]], axis=0))
    for c in range(0, 5 * WIDTH, LANE):
        put(Q_LORA + LANE + c, wt_ref[lo + D_ROPE + c:lo + D_ROPE + c + LANE, :])
    for c in range(0, KV_LORA, LANE):
        put(Q_LORA + LANE + 5 * WIDTH + c, wt_ref[Q_LORA + c:Q_LORA + c + LANE, :])


def _in_proj_kernel(x_ref, pos_ref, pos_next_ref, invf_ref, phase_ref, sign_ref, gmix_ref,
                    w_cqkr_ref, w_ckv_ref, w_hq_ref, w_hf_ref, w_hi_ref, w_hg_ref, w_mq_ref,
                    gqa_ref, w_uq_ref, gkva_ref, w_uk_ref, w_uv_ref,
                    gq_nope_ref, gq_r1_ref, gq_r2_ref, gk_nope_ref, gk_r1_ref, gk_r2_ref,
                    gmq_ref, kmem_ref, vmem_ref, gmo_ref,
                    q_out, k_out, vt_out, hq_out, hf_out, hi_out, hg_out, ymem_out,
                    cs_sc, tc_sc, ts_sc):
    tm = x_ref.shape[0]

    def fill_cs(p_ref):
        posi = p_ref[0]
        first = jnp.broadcast_to(posi[0:1, 0:1], posi.shape)
        step = (lax.broadcasted_iota(jnp.int32, posi.shape, 0) * LANE
                + lax.broadcasted_iota(jnp.int32, posi.shape, 1))
        gap = jnp.max(jnp.abs((posi - first - step).astype(F32)))

        @pl.when(gap == 0.0)
        def _():
            p0 = jnp.broadcast_to(posi[0:1, 0:1], (8, LANE)).astype(F32)
            a = jnp.cos(p0 * invf_ref[...] + phase_ref[...])
            b = pltpu.roll(a, D_ROPE, axis=1) * sign_ref[...]
            cs_sc[...] = a[0:1] * tc_sc[...] + b[0:1] * ts_sc[...]

        @pl.when(gap != 0.0)
        def _():
            posf = posi.astype(F32)
            pos = jnp.concatenate([jnp.broadcast_to(posf[r:r + 1, :], (LANE, LANE)).T
                                   for r in range(posf.shape[0])], axis=0)
            cs_sc[...] = jnp.cos(pos * invf_ref[...] + phase_ref[...])

    @pl.when(pl.program_id(0) == 0)
    def _():
        tf = lax.broadcasted_iota(jnp.int32, (tm, LANE), 0).astype(F32) * invf_ref[...]
        tc_sc[...] = jnp.cos(tf)
        ts_sc[...] = jnp.sin(tf)
        fill_cs(pos_ref)

    x = x_ref[...].astype(F32)
    h = _rms(x, gmix_ref[...], x.shape[-1]).astype(BF16)

    big = _dot(h, w_cqkr_ref[...])
    ckv = _dot(h, w_ckv_ref[...])
    hq = _dot(h, w_hq_ref[...])
    cqn = _rms(big[:, :Q_LORA], gqa_ref[...], Q_LORA).astype(BF16)
    ckvn = _rms(ckv, gkva_ref[...], KV_LORA).astype(BF16)
    qa = _dot(cqn, w_uq_ref[...])
    kn = _dot(ckvn, w_uk_ref[...])
    vt = _dot(ckvn, w_uv_ref[...]).T
    mq = _dot(h, w_mq_ref[...])
    hf_out[...] = _dot(h, w_hf_ref[...])

    cs = cs_sc[...]
    sc = pltpu.roll(cs, D_ROPE, axis=1)

    def rotary(tile, g1_ref, g2_ref):
        return tile * (g1_ref[...] * cs) + pltpu.roll(tile, D_ROPE, axis=1) * (g2_ref[...] * sc)

    q_scale = LOG2E * D_QK ** -0.5
    for hd in range(N_HEADS):
        nope = qa[:, hd * D_QK_PAD:hd * D_QK_PAD + D_NOPE]
        tile = qa[:, hd * D_QK_PAD + D_NOPE:(hd + 1) * D_QK_PAD]
        ss = jnp.sum(nope * nope, axis=-1, keepdims=True) + 0.5 * jnp.sum(tile * tile, axis=-1, keepdims=True)
        rinv = lax.rsqrt(ss * (1.0 / D_QK) + EPS) * q_scale
        q_out[:, hd * D_QK_PAD:hd * D_QK_PAD + D_NOPE] = (nope * gq_nope_ref[...] * rinv).astype(BF16)
        q_out[:, hd * D_QK_PAD + D_NOPE:(hd + 1) * D_QK_PAD] = (
            rotary(tile, gq_r1_ref, gq_r2_ref) * rinv).astype(BF16)

    vt_out[0, :, 0] = vt.reshape(N_HEADS, D_V, vt.shape[-1]).astype(BF16)
    ktile = big[:, Q_LORA:]
    ss_r = 0.5 * jnp.sum(ktile * ktile, axis=-1, keepdims=True)
    rot = rotary(ktile, gk_r1_ref, gk_r2_ref)
    for hd in range(N_HEADS):
        a = kn[:, hd * D_NOPE:(hd + 1) * D_NOPE]
        ss = jnp.sum(a * a, axis=-1, keepdims=True) + ss_r
        rinv = lax.rsqrt(ss * (1.0 / D_QK) + EPS)
        k_out[:, hd * D_QK_PAD:hd * D_QK_PAD + D_NOPE] = (a * gk_nope_ref[...] * rinv).astype(BF16)
        k_out[:, hd * D_QK_PAD + D_NOPE:(hd + 1) * D_QK_PAD] = (rot * rinv).astype(BF16)

    ss_ = []
    for hd in range(N_HEADS):
        sl = slice(hd * MEM_D, (hd + 1) * MEM_D)
        qh = (_rms(mq[:, sl], gmq_ref[...], MEM_D) * (MEM_D ** -0.5)).astype(BF16)
        ss_.append(_dot_nt(qh, kmem_ref[0, :, sl]))
    hi_out[...] = _dot(h, w_hi_ref[...]).astype(BF16)
    hq_out[...] = (hq * _sigmoid(hq) * (HG_D ** -0.5)).astype(BF16)
    ys = []
    for hd in range(N_HEADS):
        sl = slice(hd * MEM_D, (hd + 1) * MEM_D)
        s = ss_[hd]
        p = jnp.exp(s - jnp.max(s, axis=-1, keepdims=True))
        l = jnp.sum(p, axis=-1, keepdims=True)
        ys.append(_dot(p.astype(BF16), vmem_ref[0, :, sl]) / l)
    hg = _dot(h, w_hg_ref[...])
    hg_out[...] = (hg * _sigmoid(hg)).astype(BF16)
    y = jnp.concatenate(ys, axis=-1)
    ymem_out[...] = _rms(y, gmo_ref[...], WIDTH).astype(BF16)

    fill_cs(pos_next_ref)


def _attn_kernel(q_ref, k_ref, vt_ref, w0_ref, w1_ref, w2_ref, w3_ref,
                 o_ref, w0_out, w1_out, w2_out, w3_out,
                 m_sc, acc_sc, s0_sc, s1_sc, cm0_sc, cm1_sc, *, bq, bk, heads):
    qi = pl.program_id(2)
    for w_ref, w_out in ((w0_ref, w0_out), (w1_ref, w1_out), (w2_ref, w2_out), (w3_ref, w3_out)):
        w_out[...] = w_ref[...].astype(BF16)
    s_bufs = (s0_sc, s1_sc)
    cm_bufs = (cm0_sc, cm1_sc)
    m_sc[...] = jnp.full(m_sc.shape, -jnp.inf, F32)
    acc_sc[...] = jnp.zeros(acc_sc.shape, F32)

    def scores(t, slot, hd, q0=0):
        r0 = t * bk if isinstance(t, int) else pl.multiple_of(t * bk, bk)
        q = q_ref[0, q0:, hd * D_QK_PAD:(hd + 1) * D_QK_PAD]
        k = k_ref[0, pl.ds(r0, bk), hd * D_QK_PAD:(hd + 1) * D_QK_PAD]
        s = _dot_nt(k, q)
        s_bufs[slot][hd, :, q0:] = s
        if q0 == 0:
            cm_bufs[slot][hd] = jnp.max(s, axis=0, keepdims=True)

    def accumulate(t, slot, hd, key_offset=None, q0=0):
        s = s_bufs[slot][hd, :, q0:]
        if key_offset is not None:
            kv = lax.broadcasted_iota(jnp.int32, s.shape, 0) + (key_offset - q0)
            qq = lax.broadcasted_iota(jnp.int32, s.shape, 1)
            s = jnp.where(kv <= qq, s, -jnp.inf)
            cm = jnp.max(s, axis=0, keepdims=True)
        else:
            cm = cm_bufs[slot][hd]
        m_prev = m_sc[hd, :, q0:]
        m_new = jnp.maximum(m_prev, cm)
        p = jnp.exp2(s - m_new)
        alpha = jnp.exp2(m_prev - m_new)
        vt1 = jnp.concatenate([vt_ref[0, hd, t], jnp.ones((ONES_ROWS, bk), BF16)], axis=0)
        acc_sc[hd, :, q0:] = alpha * acc_sc[hd, :, q0:] + _dot(vt1, p.astype(BF16))
        m_sc[hd, :, q0:] = m_new

    for hd in range(heads):
        scores(0, 0, hd)

    def advance(t, slot):
        for hd in range(heads):
            scores(t + 1, 1 - slot, hd)
            accumulate(t, slot, hd)

    def pair(jj):
        advance(2 * jj, 0)
        advance(2 * jj + 1, 1)

    def body(j4, carry):
        pair(2 * j4)
        pair(2 * j4 + 1)
        return carry

    lax.fori_loop(0, jnp.right_shift(qi, 1), body, 0)

    @pl.when((qi & 1) == 1)
    def _():
        pair(qi - 1)
    for hd in range(heads):
        scores(2 * qi + 1, 1, hd, q0=bk)
        accumulate(2 * qi, 0, hd, key_offset=0)
    for hd in range(heads):
        accumulate(2 * qi + 1, 1, hd, key_offset=bk, q0=bk)

    for hd in range(heads):
        o = (acc_sc[hd, :D_V] / acc_sc[hd, D_V:D_V + 1]).T
        o_ref[0, :, hd * D_V:(hd + 1) * D_V] = o.astype(o_ref.dtype)


def _pair_reference(b, m):
    c = b.shape[0]
    n2 = c // (2 * m)
    br = b.reshape(n2, 2 * m, b.shape[1])
    last = br[:, m - 1:m, :]
    return jnp.broadcast_to(last, br.shape).reshape(b.shape)


def _hgrn_kernel(hq_ref, hf_ref, hi_ref, hg_ref, lbl_ref, gain_ref, o_ref, st_ref, b_sc, *, n_chunks, layer):
    C = HG_CHUNK

    @pl.when(pl.program_id(1) == 0)
    def _():
        st_ref[...] = jnp.zeros(st_ref.shape, F32)

    lg = lbl_ref[...].astype(F32)
    e = jnp.exp(lg - jnp.max(lg, axis=0, keepdims=True))
    lb = jnp.sum(e[:layer + 1], axis=0, keepdims=True) / jnp.sum(e, axis=0, keepdims=True)

    row = lax.broadcasted_iota(jnp.int32, (C, C), 0)
    col = lax.broadcasted_iota(jnp.int32, (C, C), 1)
    tri = (col <= row).astype(BF16)
    diag_mask = (((row ^ col) & ~(HG_SUB - 1)) | jnp.where(col <= row, 0, 1)) == 0
    sub_keep = [jnp.where((row & (HG_SUB - 1)) == s_off, 1.0, 0.0).astype(BF16) for s_off in range(HG_SUB)]
    levels = []
    m = HG_SUB
    while m < C:
        bad = ((row ^ col) & ~(2 * m - 1)) | ((row & m) ^ m) | (col & m)
        levels.append((m, bad == 0))
        m *= 2

    def chunk(c):
        r0 = c * C
        fr = hf_ref[0, pl.ds(r0, C), :]
        f = lb + (1.0 - lb) * _sigmoid(fr)
        logf = jnp.log(f) * LOG2E
        kk_all = 1.0 - f
        t0 = logf.astype(BF16)
        r1 = logf - t0.astype(F32)
        t1 = r1.astype(BF16)
        t2 = (r1 - t1.astype(F32)).astype(BF16)
        b_all = _dot(tri, t0) + _dot(tri, t1) + _dot(tri, t2)
        q_all = hq_ref[0, pl.ds(r0, C), :].astype(F32)
        v_all = hi_ref[0, pl.ds(r0, C), :]
        g_all = hg_ref[0, pl.ds(r0, C), :].astype(F32)
        b_sc[c] = b_all
        zero = jnp.zeros((C, HG_D), BF16)

        def pair_nt(x0, x1):
            return jnp.concatenate([jnp.concatenate([x0, zero], axis=1),
                                    jnp.concatenate([zero, x1], axis=1)], axis=0)

        for h0 in range(0, N_HEADS, 2):
            heads = (h0, h0 + 1)
            sls = [slice(hd * HG_D, (hd + 1) * HG_D) for hd in heads]
            bs_ = [b_all[:, sl] for sl in sls]
            qs = [q_all[:, sl] for sl in sls]
            kks = [kk_all[:, sl] for sl in sls]
            vs = [v_all[:, sl] for sl in sls]

            a_ = []
            for b, q, kk, sl in zip(bs_, qs, kks, sls):
                kb = kk.astype(BF16)
                ms = []
                ks = []
                for s_off in range(HG_SUB):
                    bs = jnp.concatenate(
                        [jnp.broadcast_to(b_sc[c, i * HG_SUB + s_off:i * HG_SUB + s_off + 1, sl],
                                          (HG_SUB, HG_D)) for i in range(C // HG_SUB)], axis=0)
                    ms.append((q * jnp.exp2(jnp.minimum(b - bs, 0.0))).astype(BF16))
                    ks.append(kb * sub_keep[s_off])
                ad = _dot_nt(jnp.concatenate(ms, axis=1), jnp.concatenate(ks, axis=1))
                a_.append(jnp.where(diag_mask, ad, 0.0))
            for m_blk, mask in levels:
                qes = []
                kes = []
                for b, q, kk in zip(bs_, qs, kks):
                    d = b - _pair_reference(b, m_blk)
                    qes.append((q * jnp.exp2(d)).astype(BF16))
                    kes.append((kk * jnp.exp2(-d)).astype(BF16))
                al = _dot_nt(jnp.concatenate(qes, axis=1), pair_nt(*kes))
                a_ = [jnp.where(mask, al[:, j * C:(j + 1) * C], a_[j]) for j in range(2)]

            sts = [st_ref[hd] for hd in heads]
            qd = jnp.concatenate([(q * jnp.exp2(b)).astype(BF16) for b, q in zip(bs_, qs)], axis=1)
            a2 = jnp.concatenate([a.astype(BF16) for a in a_], axis=1)
            v2 = jnp.concatenate([jnp.concatenate([vs[0], zero], axis=1),
                                  jnp.concatenate([zero, vs[1]], axis=1)], axis=0)
            o2 = _dot(a2, v2) + _dot_nt(qd, pair_nt(*[st.astype(BF16) for st in sts]))

            for j, hd in enumerate(heads):
                b, kk, v, sl = bs_[j], kks[j], vs[j], sls[j]
                b_last = b[C - 1:C, :]
                kd = (kk * jnp.exp2(b_last - b)).astype(BF16)
                vt = v.astype(F32).T.astype(BF16)
                st_ref[hd] = sts[j] * jnp.exp2(b_last) + _dot(vt, kd)

                on = _rms(o2[:, j * HG_D:(j + 1) * HG_D], gain_ref[:, sl], HG_D)
                o_ref[0, pl.ds(r0, C), sl] = (on * g_all[:, sl]).astype(o_ref.dtype)

    for c in range(n_chunks):
        chunk(c)


def _out_ffn_kernel(x_ref, ymla_ref, yhg_ref, ymem_ref, gmla_ref, w_out_ref, gffn_ref,
                    w_gate_ref, w_up_ref, w_down_ref, o_ref):
    x = x_ref[...].astype(F32)
    ymla = _rms(ymla_ref[...].astype(F32), gmla_ref[...], WIDTH).astype(BF16)
    mix = (_dot(ymla, w_out_ref[0:WIDTH, :])
           + _dot(yhg_ref[...], w_out_ref[WIDTH:2 * WIDTH, :])
           + _dot(ymem_ref[...], w_out_ref[2 * WIDTH:3 * WIDTH, :]))
    x1 = x + mix
    h2 = _rms(x1, gffn_ref[...], x1.shape[-1]).astype(BF16)
    g = _dot(h2, w_gate_ref[...])
    u = _dot(h2, w_up_ref[...])
    act = (g * _sigmoid(g) * u).astype(BF16)
    o_ref[...] = (x1 + _dot(act, w_down_ref[...])).astype(o_ref.dtype)


def _full(shape):
    nd = len(shape)
    return pl.BlockSpec(shape, lambda *_: (0,) * nd)


def _params(sem):
    return pltpu.CompilerParams(dimension_semantics=sem, vmem_limit_bytes=VMEM_LIMIT)


def _row(v):
    return v.reshape(1, -1).astype(F32)


def _layer(x, mem, positions, layer, norm_mix, norm_mem, w_in, q_a_norm, w_uq, kv_a_norm, w_ukv,
           mla_q_norm, mla_k_norm, hg_lb_logits, hg_out_norm, w_mem_kv, mem_q_norm, mem_k_norm,
           mla_out_norm, mem_out_norm, w_out, norm_ffn, w_gate, w_up, w_down):
    B, S, D = x.shape
    M = mem.shape[1]
    T = B * S
    half = D_ROPE // 2
    H = N_HEADS

    assert w_in.shape == (D, W_ALL - D_ROPE)
    wrows = 256
    assert w_uq.shape == (Q_LORA, H * D_QK) and w_ukv.shape == (KV_LORA, H * (D_NOPE + D_V))
    small = [(Q_LORA, H * D_QK_PAD), (KV_LORA, H * D_NOPE), (KV_LORA, H * D_V), w_mem_kv.shape]
    w_all, uq2, w_uk, w_uv, w_mem_b = pl.pallas_call(
        _w_in_kernel,
        grid=(D // wrows,),
        in_specs=[pl.BlockSpec((w_in.shape[1], wrows), lambda i: (0, i)),
                  _full((H * D_QK, Q_LORA)), _full(w_ukv.shape), _full(w_mem_kv.shape)],
        out_specs=[pl.BlockSpec((wrows, W_ALL), lambda i: (i, 0))] + [_full(s) for s in small],
        out_shape=[jax.ShapeDtypeStruct((D, W_ALL), BF16)] + [jax.ShapeDtypeStruct(s, BF16) for s in small],
        compiler_params=_params(("arbitrary",)),
        name="w_in_layout",
    )(w_in.T, w_uq.T, w_ukv, w_mem_kv)

    pad = jnp.zeros((LANE - D_ROPE,), F32)

    def rotary_gains(g):
        g = g.astype(F32)
        g1 = jnp.concatenate([g[D_NOPE:], pad]).reshape(1, LANE)
        g2 = jnp.concatenate([g[D_NOPE + half:], g[D_NOPE:D_NOPE + half], pad]).reshape(1, LANE)
        return g[:D_NOPE].reshape(1, D_NOPE), g1, g2

    gq_nope, gq_r1, gq_r2 = rotary_gains(mla_q_norm)
    gk_nope, gk_r1, gk_r2 = rotary_gains(mla_k_norm)
    inv_freq = jnp.power(ROPE_THETA, -jnp.arange(half, dtype=F32) / half)
    invf = jnp.tile(inv_freq, LANE // half).reshape(1, LANE)
    phase = jnp.concatenate([jnp.zeros((D_ROPE,), F32), jnp.full((D_ROPE,), -jnp.pi / 2, F32)]).reshape(1, LANE)

    kmem, vmem = pl.pallas_call(
        _mem_kv_kernel,
        grid=(B,),
        in_specs=[pl.BlockSpec((1, M, D), lambda b: (b, 0, 0)),
                  _full((1, D)), _full((D, 2 * WIDTH)), _full((1, MEM_D))],
        out_specs=[pl.BlockSpec((1, M, WIDTH), lambda b: (b, 0, 0))] * 2,
        out_shape=[jax.ShapeDtypeStruct((B, M, WIDTH), BF16)] * 2,
        compiler_params=_params(("arbitrary",)),
        name="mem_kv",
    )(mem, _row(norm_mem), w_mem_b, _row(mem_k_norm))

    tm = min(512, S)
    assert S % tm == 0
    steps_per_batch = S // tm
    x2 = x.reshape(T, D)
    assert tm % LANE == 0
    pos2 = positions.reshape(T // tm, tm // LANE, LANE).astype(jnp.int32)
    pos_spec = pl.BlockSpec((1, tm // LANE, LANE), lambda i: (i, 0, 0))
    n_tok_steps = T // tm
    pos_next_spec = pl.BlockSpec((1, tm // LANE, LANE), lambda i: (jnp.minimum(i + 1, n_tok_steps - 1), 0, 0))
    sign = jnp.concatenate([jnp.full((D_ROPE,), -1.0, F32), jnp.ones((D_ROPE,), F32)]).reshape(1, LANE)
    tok = lambda w: pl.BlockSpec((tm, w), lambda i: (i, 0))
    col = lambda width, start: pl.BlockSpec((D, width), lambda i: (0, start // width))
    weight_specs = [col(WIDTH, 0), col(KV_LORA, 6 * WIDTH), col(WIDTH, WIDTH), col(WIDTH, 2 * WIDTH),
                    col(WIDTH, 3 * WIDTH), col(WIDTH, 4 * WIDTH), col(WIDTH, 5 * WIDTH)]
    weights = [w_all] * len(weight_specs)
    rest = [_row(q_a_norm), uq2, _row(kv_a_norm), w_uk, w_uv,
            gq_nope, gq_r1, gq_r2, gk_nope, gk_r1, gk_r2, _row(mem_q_norm)]
    mem_spec = pl.BlockSpec((1, M, WIDTH), lambda i: (i // steps_per_batch, 0, 0))
    vt_spec = pl.BlockSpec((1, H, 1, D_V, tm),
                           lambda i: (i // steps_per_batch, 0, i % steps_per_batch, 0, 0))
    q_all, k_all, vt_all, hq, hf, hi, hg, ymem = pl.pallas_call(
        _in_proj_kernel,
        grid=(T // tm,),
        in_specs=([tok(D), pos_spec, pos_next_spec, _full((1, LANE)), _full((1, LANE)), _full((1, LANE)),
                   _full((1, D))]
                  + weight_specs + [_full(r.shape) for r in rest]
                  + [mem_spec, mem_spec, _full((1, WIDTH))]),
        out_specs=[tok(H * D_QK_PAD), tok(H * D_QK_PAD), vt_spec] + [tok(WIDTH)] * 5,
        out_shape=[jax.ShapeDtypeStruct((T, H * D_QK_PAD), BF16)] * 2
        + [jax.ShapeDtypeStruct((B, H, steps_per_batch, D_V, tm), BF16)]
        + [jax.ShapeDtypeStruct((T, WIDTH), dt) for dt in (BF16, F32, BF16, BF16, BF16)],
        scratch_shapes=[pltpu.VMEM((tm, LANE), F32)] * 3,
        compiler_params=_params(("arbitrary",)),
        name="in_proj",
    )(x2, pos2, pos2, invf, phase, sign, _row(norm_mix), *weights, *rest, kmem, vmem, _row(mem_out_norm))

    bk = tm
    bq = 2 * bk
    assert S % bq == 0
    hpb = 2
    att_grid = (B, H // hpb, S // bq)
    n_att = att_grid[0] * att_grid[1] * att_grid[2]

    def cast_spec(w):
        rows = w.shape[0]
        rb = next(r for r in range(16 * -(-rows // (16 * n_att)), rows + 1, 16) if rows % r == 0)
        last = rows // rb - 1
        return pl.BlockSpec((rb, w.shape[1]),
                            lambda b, h, i: (jnp.minimum((b * att_grid[1] + h) * att_grid[2] + i, last), 0))

    ffn_weights = [w_out, w_gate, w_up, w_down]
    cast_specs = [cast_spec(w) for w in ffn_weights]
    y_mla, w_out_b, w_gate_b, w_up_b, w_down_b = pl.pallas_call(
        functools.partial(_attn_kernel, bq=bq, bk=bk, heads=hpb),
        grid=att_grid,
        in_specs=[pl.BlockSpec((1, bq, hpb * D_QK_PAD), lambda b, h, i: (b, i, h)),
                  pl.BlockSpec((1, S, hpb * D_QK_PAD), lambda b, h, i: (b, 0, h)),
                  pl.BlockSpec((1, hpb, S // bk, D_V, bk), lambda b, h, i: (b, h, 0, 0, 0))] + cast_specs,
        out_specs=[pl.BlockSpec((1, bq, hpb * D_V), lambda b, h, i: (b, i, h))] + cast_specs,
        out_shape=[jax.ShapeDtypeStruct((B, S, H * D_V), BF16)]
        + [jax.ShapeDtypeStruct(w.shape, BF16) for w in ffn_weights],
        scratch_shapes=[pltpu.VMEM((hpb, 1, bq), F32),
                        pltpu.VMEM((hpb, D_V + ONES_ROWS, bq), F32),
                        pltpu.VMEM((hpb, bk, bq), F32), pltpu.VMEM((hpb, bk, bq), F32),
                        pltpu.VMEM((hpb, 1, bq), F32), pltpu.VMEM((hpb, 1, bq), F32)],
        compiler_params=_params(("arbitrary", "arbitrary", "arbitrary")),
        name="mla_attn",
    )(q_all.reshape(B, S, H * D_QK_PAD), k_all.reshape(B, S, H * D_QK_PAD), vt_all, *ffn_weights)

    ts = min(1024, S)
    assert S % ts == 0 and ts % HG_CHUNK == 0
    seq = lambda: pl.BlockSpec((1, ts, WIDTH), lambda b, i: (b, i, 0))
    n_layers = hg_lb_logits.shape[0]
    y_hg = pl.pallas_call(
        functools.partial(_hgrn_kernel, n_chunks=ts // HG_CHUNK, layer=layer),
        grid=(B, S // ts),
        in_specs=[seq(), seq(), seq(), seq(), _full((n_layers, WIDTH)), _full((1, WIDTH))],
        out_specs=seq(),
        out_shape=jax.ShapeDtypeStruct((B, S, WIDTH), BF16),
        scratch_shapes=[pltpu.VMEM((N_HEADS, HG_D, HG_D), F32),
                        pltpu.VMEM((ts // HG_CHUNK, HG_CHUNK, WIDTH), F32)],
        compiler_params=_params(("arbitrary", "arbitrary")),
        name="hgrn",
    )(hq.reshape(B, S, WIDTH), hf.reshape(B, S, WIDTH), hi.reshape(B, S, WIDTH), hg.reshape(B, S, WIDTH),
      hg_lb_logits.astype(F32), _row(hg_out_norm))

    d_ff = w_gate.shape[1]
    once = lambda shape: pl.BlockSpec(shape, lambda i: (0, 0), pipeline_mode=pl.Buffered(1))
    out = pl.pallas_call(
        _out_ffn_kernel,
        grid=(T // tm,),
        in_specs=[tok(D), tok(WIDTH), tok(WIDTH), tok(WIDTH), _full((1, WIDTH)), once((3 * WIDTH, D)),
                  _full((1, D)), once((D, d_ff)), once((D, d_ff)), once((d_ff, D))],
        out_specs=tok(D),
        out_shape=jax.ShapeDtypeStruct((T, D), x.dtype),
        compiler_params=_params(("arbitrary",)),
        name="out_ffn",
    )(x2, y_mla.reshape(T, WIDTH), y_hg.reshape(T, WIDTH), ymem, _row(mla_out_norm), w_out_b,
      _row(norm_ffn), w_gate_b, w_up_b, w_down_b)
    return out.reshape(B, S, D)


def kernel(x, mem, positions, norm_mix, norm_mem, w_in, q_a_norm, w_uq, kv_a_norm, w_ukv, mla_q_norm, mla_k_norm, hg_lb_logits, hg_out_norm, w_mem_kv, mem_q_norm, mem_k_norm, mla_out_norm, mem_out_norm, w_out, norm_ffn, w_gate, w_up, w_down):
    depth = w_in.shape[0]
    for l in range(depth):
        x = _layer(x, mem, positions, l, norm_mix[l], norm_mem[l], w_in[l], q_a_norm[l], w_uq[l],
                   kv_a_norm[l], w_ukv[l], mla_q_norm[l], mla_k_norm[l], hg_lb_logits, hg_out_norm[l],
                   w_mem_kv[l], mem_q_norm[l], mem_k_norm[l], mla_out_norm[l], mem_out_norm[l],
                   w_out[l], norm_ffn[l], w_gate[l], w_up[l], w_down[l])
    return x
```

```python
import functools

import jax
import jax.numpy as jnp
from jax import lax
from jax.experimental import pallas as pl
from jax.experimental.pallas import tpu as pltpu

F32 = jnp.float32
BF16 = jnp.bfloat16

EPS = 1e-6
N_HEADS = 4
D_NOPE = 128
D_ROPE = 64
D_QK = D_NOPE + D_ROPE
D_QK_PAD = 256
D_V = 128
Q_LORA = 384
KV_LORA = 256
ROPE_THETA = 10000.0
LOG2E = 1.4426950408889634
HG_D = 128
MEM_D = 128
WIDTH = N_HEADS * 128

LANE = 128
HG_CHUNK = 128
HG_SUB = 8
ONES_ROWS = 16

VMEM_LIMIT = 56 * 1024 * 1024


def _dot(a, b):
    return jnp.dot(a, b, preferred_element_type=F32)


def _dot_nt(a, b):
    return lax.dot_general(a, b, (((1,), (1,)), ((), ())), preferred_element_type=F32)


def _rms(x, g, width):
    ss = jnp.sum(x * x, axis=-1, keepdims=True)
    return x * lax.rsqrt(ss * (1.0 / width) + EPS) * g


def _sigmoid(x):
    return 1.0 / (1.0 + jnp.exp(-x))


W_ALL = Q_LORA + 2 * D_ROPE + 5 * WIDTH + KV_LORA


def _w_in_kernel(wt_ref, uq_ref, ukv_ref, wmem_ref, mem_ref, gmem_ref, gk_ref,
                 o_ref, uq_out, uk_out, uv_out, kmem_out, vmem_out):
    lo = Q_LORA + KV_LORA
    half = D_ROPE // 2

    @pl.when(pl.program_id(0) == 0)
    def _():
        uq = uq_ref[...]
        for hd in range(N_HEADS):
            r = hd * D_QK
            uq_out[:, hd * D_QK_PAD:hd * D_QK_PAD + D_NOPE] = uq[:, r:r + D_NOPE].astype(BF16)
            x1 = uq[:, r + D_NOPE:r + D_NOPE + half]
            x2 = uq[:, r + D_NOPE + half:r + D_QK]
            uq_out[:, hd * D_QK_PAD + D_NOPE:(hd + 1) * D_QK_PAD] = (
                jnp.concatenate([x1, x2, -x2, x1], axis=1).astype(BF16))
            c = hd * (D_NOPE + D_V)
            uk_out[:, hd * D_NOPE:(hd + 1) * D_NOPE] = ukv_ref[:, c:c + D_NOPE].astype(BF16)
            uv_out[:, hd * D_V:(hd + 1) * D_V] = ukv_ref[:, c + D_NOPE:c + D_NOPE + D_V].astype(BF16)
        w_mem = wmem_ref[...].astype(BF16)
        for b in range(mem_ref.shape[0]):
            m = mem_ref[b].astype(F32)
            kv = _dot(_rms(m, gmem_ref[...], m.shape[-1]).astype(BF16), w_mem)
            for hd in range(N_HEADS):
                kh = kv[:, hd * MEM_D:(hd + 1) * MEM_D]
                kmem_out[b, :, hd * MEM_D:(hd + 1) * MEM_D] = _rms(kh, gk_ref[...], MEM_D).astype(BF16)
            vmem_out[b] = kv[:, WIDTH:].astype(BF16)

    def put(dst, rows):
        o_ref[:, dst:dst + rows.shape[0]] = rows.T.astype(BF16)

    for c in range(0, Q_LORA, LANE):
        put(c, wt_ref[c:c + LANE, :])
    put(Q_LORA, jnp.concatenate([wt_ref[lo:lo + D_ROPE, :], -wt_ref[lo + half:lo + D_ROPE, :],
                                 wt_ref[lo:lo + half, :]], axis=0))
    for c in range(0, 5 * WIDTH, LANE):
        put(Q_LORA + LANE + c, wt_ref[lo + D_ROPE + c:lo + D_ROPE + c + LANE, :])
    for c in range(0, KV_LORA, LANE):
        put(Q_LORA + LANE + 5 * WIDTH + c, wt_ref[Q_LORA + c:Q_LORA + c + LANE, :])


def _in_proj_kernel(x_ref, pos_ref, pos_next_ref, invf_ref, phase_ref, sign_ref, gmix_ref,
                    w_cqkr_ref, w_ckv_ref, w_hq_ref, w_hf_ref, w_hi_ref, w_hg_ref, w_mq_ref,
                    gqa_ref, w_uq_ref, gkva_ref, w_uk_ref, w_uv_ref,
                    gq_nope_ref, gq_r1_ref, gq_r2_ref, gk_nope_ref, gk_r1_ref, gk_r2_ref,
                    gmq_ref, kmem_ref, vmem_ref, gmo_ref,
                    q_out, k_out, vt_out, hq_out, hf_out, hi_out, hg_out, ymem_out,
                    cs_sc, tc_sc, ts_sc):
    tm = x_ref.shape[0]

    def fill_cs(p_ref):
        posi = p_ref[0]
        first = jnp.broadcast_to(posi[0:1, 0:1], posi.shape)
        step = (lax.broadcasted_iota(jnp.int32, posi.shape, 0) * LANE
                + lax.broadcasted_iota(jnp.int32, posi.shape, 1))
        gap = jnp.max(jnp.abs((posi - first - step).astype(F32)))

        @pl.when(gap == 0.0)
        def _():
            p0 = jnp.broadcast_to(posi[0:1, 0:1], (8, LANE)).astype(F32)
            a = jnp.cos(p0 * invf_ref[...] + phase_ref[...])
            b = pltpu.roll(a, D_ROPE, axis=1) * sign_ref[...]
            cs_sc[...] = a[0:1] * tc_sc[...] + b[0:1] * ts_sc[...]

        @pl.when(gap != 0.0)
        def _():
            posf = posi.astype(F32)
            pos = jnp.concatenate([jnp.broadcast_to(posf[r:r + 1, :], (LANE, LANE)).T
                                   for r in range(posf.shape[0])], axis=0)
            cs_sc[...] = jnp.cos(pos * invf_ref[...] + phase_ref[...])

    @pl.when(pl.program_id(0) == 0)
    def _():
        tf = lax.broadcasted_iota(jnp.int32, (tm, LANE), 0).astype(F32) * invf_ref[...]
        tc_sc[...] = jnp.cos(tf)
        ts_sc[...] = jnp.sin(tf)
        fill_cs(pos_ref)

    x = x_ref[...].astype(F32)
    h = _rms(x, gmix_ref[...], x.shape[-1]).astype(BF16)

    big = _dot(h, w_cqkr_ref[...])
    ckv = _dot(h, w_ckv_ref[...])
    hq = _dot(h, w_hq_ref[...])
    cqn = _rms(big[:, :Q_LORA], gqa_ref[...], Q_LORA).astype(BF16)
    ckvn = _rms(ckv, gkva_ref[...], KV_LORA).astype(BF16)
    qa = _dot(cqn, w_uq_ref[...])
    kn = _dot(ckvn, w_uk_ref[...])
    vt = _dot(ckvn, w_uv_ref[...]).T
    mq = _dot(h, w_mq_ref[...])
    hf_out[...] = _dot(h, w_hf_ref[...])

    cs = cs_sc[...]
    sc = pltpu.roll(cs, D_ROPE, axis=1)

    def rotary(tile, g1_ref, g2_ref):
        return tile * (g1_ref[...] * cs) + pltpu.roll(tile, D_ROPE, axis=1) * (g2_ref[...] * sc)

    q_scale = LOG2E * D_QK ** -0.5
    for hd in range(N_HEADS):
        nope = qa[:, hd * D_QK_PAD:hd * D_QK_PAD + D_NOPE]
        tile = qa[:, hd * D_QK_PAD + D_NOPE:(hd + 1) * D_QK_PAD]
        ss = jnp.sum(nope * nope, axis=-1, keepdims=True) + 0.5 * jnp.sum(tile * tile, axis=-1, keepdims=True)
        rinv = lax.rsqrt(ss * (1.0 / D_QK) + EPS) * q_scale
        q_out[:, hd * D_QK_PAD:hd * D_QK_PAD + D_NOPE] = (nope * gq_nope_ref[...] * rinv).astype(BF16)
        q_out[:, hd * D_QK_PAD + D_NOPE:(hd + 1) * D_QK_PAD] = (
            rotary(tile, gq_r1_ref, gq_r2_ref) * rinv).astype(BF16)

    vt_out[0, :, 0] = vt.reshape(N_HEADS, D_V, vt.shape[-1]).astype(BF16)
    ktile = big[:, Q_LORA:]
    ss_r = 0.5 * jnp.sum(ktile * ktile, axis=-1, keepdims=True)
    rot = rotary(ktile, gk_r1_ref, gk_r2_ref)
    for hd in range(N_HEADS):
        a = kn[:, hd * D_NOPE:(hd + 1) * D_NOPE]
        ss = jnp.sum(a * a, axis=-1, keepdims=True) + ss_r
        rinv = lax.rsqrt(ss * (1.0 / D_QK) + EPS)
        k_out[:, hd * D_QK_PAD:hd * D_QK_PAD + D_NOPE] = (a * gk_nope_ref[...] * rinv).astype(BF16)
        k_out[:, hd * D_QK_PAD + D_NOPE:(hd + 1) * D_QK_PAD] = (rot * rinv).astype(BF16)

    ss_ = []
    for hd in range(N_HEADS):
        sl = slice(hd * MEM_D, (hd + 1) * MEM_D)
        qh = (_rms(mq[:, sl], gmq_ref[...], MEM_D) * (MEM_D ** -0.5)).astype(BF16)
        ss_.append(_dot_nt(qh, kmem_ref[0, :, sl]))
    hi_out[...] = _dot(h, w_hi_ref[...]).astype(BF16)
    hq_out[...] = (hq * _sigmoid(hq) * (HG_D ** -0.5)).astype(BF16)
    ys = []
    for hd in range(N_HEADS):
        sl = slice(hd * MEM_D, (hd + 1) * MEM_D)
        s = ss_[hd]
        p = jnp.exp(s - jnp.max(s, axis=-1, keepdims=True))
        l = jnp.sum(p, axis=-1, keepdims=True)
        ys.append(_dot(p.astype(BF16), vmem_ref[0, :, sl]) / l)
    hg = _dot(h, w_hg_ref[...])
    hg_out[...] = (hg * _sigmoid(hg)).astype(BF16)
    y = jnp.concatenate(ys, axis=-1)
    ymem_out[...] = _rms(y, gmo_ref[...], WIDTH).astype(BF16)

    fill_cs(pos_next_ref)


def _attn_kernel(q_ref, k_ref, vt_ref, w0_ref, w1_ref, w2_ref, w3_ref,
                 o_ref, w0_out, w1_out, w2_out, w3_out,
                 m_sc, acc_sc, s0_sc, s1_sc, cm0_sc, cm1_sc, *, bq, bk, heads):
    qi = pl.program_id(2)
    for w_ref, w_out in ((w0_ref, w0_out), (w1_ref, w1_out), (w2_ref, w2_out), (w3_ref, w3_out)):
        w_out[...] = w_ref[...].astype(BF16)
    s_bufs = (s0_sc, s1_sc)
    cm_bufs = (cm0_sc, cm1_sc)
    m_sc[...] = jnp.full(m_sc.shape, -jnp.inf, F32)
    acc_sc[...] = jnp.zeros(acc_sc.shape, F32)

    def scores(t, slot, hd, q0=0):
        r0 = t * bk if isinstance(t, int) else pl.multiple_of(t * bk, bk)
        q = q_ref[0, q0:, hd * D_QK_PAD:(hd + 1) * D_QK_PAD]
        k = k_ref[0, pl.ds(r0, bk), hd * D_QK_PAD:(hd + 1) * D_QK_PAD]
        s = _dot_nt(k, q)
        s_bufs[slot][hd, :, q0:] = s
        if q0 == 0:
            cm_bufs[slot][hd] = jnp.max(s, axis=0, keepdims=True)

    def accumulate(t, slot, hd, key_offset=None, q0=0):
        s = s_bufs[slot][hd, :, q0:]
        if key_offset is not None:
            kv = lax.broadcasted_iota(jnp.int32, s.shape, 0) + (key_offset - q0)
            qq = lax.broadcasted_iota(jnp.int32, s.shape, 1)
            s = jnp.where(kv <= qq, s, -jnp.inf)
            cm = jnp.max(s, axis=0, keepdims=True)
        else:
            cm = cm_bufs[slot][hd]
        m_prev = m_sc[hd, :, q0:]
        m_new = jnp.maximum(m_prev, cm)
        p = jnp.exp2(s - m_new)
        alpha = jnp.exp2(m_prev - m_new)
        vt1 = jnp.concatenate([vt_ref[0, hd, t], jnp.ones((ONES_ROWS, bk), BF16)], axis=0)
        acc_sc[hd, :, q0:] = alpha * acc_sc[hd, :, q0:] + _dot(vt1, p.astype(BF16))
        m_sc[hd, :, q0:] = m_new

    for hd in range(heads):
        scores(0, 0, hd)

    def advance(t, slot):
        for hd in range(heads):
            scores(t + 1, 1 - slot, hd)
            accumulate(t, slot, hd)

    def pair(jj):
        advance(2 * jj, 0)
        advance(2 * jj + 1, 1)

    def body(j4, carry):
        pair(2 * j4)
        pair(2 * j4 + 1)
        return carry

    lax.fori_loop(0, jnp.right_shift(qi, 1), body, 0)

    @pl.when((qi & 1) == 1)
    def _():
        pair(qi - 1)
    for hd in range(heads):
        scores(2 * qi + 1, 1, hd, q0=bk)
        accumulate(2 * qi, 0, hd, key_offset=0)
    for hd in range(heads):
        accumulate(2 * qi + 1, 1, hd, key_offset=bk, q0=bk)

    for hd in range(heads):
        o = (acc_sc[hd, :D_V] / acc_sc[hd, D_V:D_V + 1]).T
        o_ref[0, :, hd * D_V:(hd + 1) * D_V] = o.astype(o_ref.dtype)


def _pair_reference(b, m):
    c = b.shape[0]
    n2 = c // (2 * m)
    br = b.reshape(n2, 2 * m, b.shape[1])
    last = br[:, m - 1:m, :]
    return jnp.broadcast_to(last, br.shape).reshape(b.shape)


def _hgrn_kernel(hq_ref, hf_ref, hi_ref, hg_ref, lbl_ref, gain_ref, o_ref, st_ref, b_sc, *, n_chunks, layer):
    C = HG_CHUNK

    @pl.when(pl.program_id(1) == 0)
    def _():
        st_ref[...] = jnp.zeros(st_ref.shape, F32)

    lg = lbl_ref[...].astype(F32)
    e = jnp.exp(lg - jnp.max(lg, axis=0, keepdims=True))
    lb = jnp.sum(e[:layer + 1], axis=0, keepdims=True) / jnp.sum(e, axis=0, keepdims=True)

    row = lax.broadcasted_iota(jnp.int32, (C, C), 0)
    col = lax.broadcasted_iota(jnp.int32, (C, C), 1)
    tri = (col <= row).astype(BF16)
    diag_mask = (((row ^ col) & ~(HG_SUB - 1)) | jnp.where(col <= row, 0, 1)) == 0
    sub_keep = [jnp.where((row & (HG_SUB - 1)) == s_off, 1.0, 0.0).astype(BF16) for s_off in range(HG_SUB)]
    levels = []
    m = HG_SUB
    while m < C:
        bad = ((row ^ col) & ~(2 * m - 1)) | ((row & m) ^ m) | (col & m)
        levels.append((m, bad == 0))
        m *= 2

    def chunk(c):
        r0 = c * C
        fr = hf_ref[0, pl.ds(r0, C), :]
        f = lb + (1.0 - lb) * _sigmoid(fr)
        logf = jnp.log(f) * LOG2E
        kk_all = 1.0 - f
        t0 = logf.astype(BF16)
        r1 = logf - t0.astype(F32)
        t1 = r1.astype(BF16)
        t2 = (r1 - t1.astype(F32)).astype(BF16)
        b_all = _dot(tri, t0) + _dot(tri, t1) + _dot(tri, t2)
        q_all = hq_ref[0, pl.ds(r0, C), :].astype(F32)
        v_all = hi_ref[0, pl.ds(r0, C), :]
        g_all = hg_ref[0, pl.ds(r0, C), :].astype(F32)
        b_sc[c] = b_all
        zero = jnp.zeros((C, HG_D), BF16)

        def pair_nt(x0, x1):
            return jnp.concatenate([jnp.concatenate([x0, zero], axis=1),
                                    jnp.concatenate([zero, x1], axis=1)], axis=0)

        for h0 in range(0, N_HEADS, 2):
            heads = (h0, h0 + 1)
            sls = [slice(hd * HG_D, (hd + 1) * HG_D) for hd in heads]
            bs_ = [b_all[:, sl] for sl in sls]
            qs = [q_all[:, sl] for sl in sls]
            kks = [kk_all[:, sl] for sl in sls]
            vs = [v_all[:, sl] for sl in sls]

            a_ = []
            for b, q, kk, sl in zip(bs_, qs, kks, sls):
                kb = kk.astype(BF16)
                ms = []
                ks = []
                for s_off in range(HG_SUB):
                    bs = jnp.concatenate(
                        [jnp.broadcast_to(b_sc[c, i * HG_SUB + s_off:i * HG_SUB + s_off + 1, sl],
                                          (HG_SUB, HG_D)) for i in range(C // HG_SUB)], axis=0)
                    ms.append((q * jnp.exp2(jnp.minimum(b - bs, 0.0))).astype(BF16))
                    ks.append(kb * sub_keep[s_off])
                ad = _dot_nt(jnp.concatenate(ms, axis=1), jnp.concatenate(ks, axis=1))
                a_.append(jnp.where(diag_mask, ad, 0.0))
            for m_blk, mask in levels:
                qes = []
                kes = []
                for b, q, kk in zip(bs_, qs, kks):
                    d = b - _pair_reference(b, m_blk)
                    qes.append((q * jnp.exp2(d)).astype(BF16))
                    kes.append((kk * jnp.exp2(-d)).astype(BF16))
                al = _dot_nt(jnp.concatenate(qes, axis=1), pair_nt(*kes))
                a_ = [jnp.where(mask, al[:, j * C:(j + 1) * C], a_[j]) for j in range(2)]

            sts = [st_ref[hd] for hd in heads]
            qd = jnp.concatenate([(q * jnp.exp2(b)).astype(BF16) for b, q in zip(bs_, qs)], axis=1)
            a2 = jnp.concatenate([a.astype(BF16) for a in a_], axis=1)
            v2 = jnp.concatenate([jnp.concatenate([vs[0], zero], axis=1),
                                  jnp.concatenate([zero, vs[1]], axis=1)], axis=0)
            o2 = _dot(a2, v2) + _dot_nt(qd, pair_nt(*[st.astype(BF16) for st in sts]))

            for j, hd in enumerate(heads):
                b, kk, v, sl = bs_[j], kks[j], vs[j], sls[j]
                b_last = b[C - 1:C, :]
                kd = (kk * jnp.exp2(b_last - b)).astype(BF16)
                vt = v.astype(F32).T.astype(BF16)
                st_ref[hd] = sts[j] * jnp.exp2(b_last) + _dot(vt, kd)

                on = _rms(o2[:, j * HG_D:(j + 1) * HG_D], gain_ref[:, sl], HG_D)
                o_ref[0, pl.ds(r0, C), sl] = (on * g_all[:, sl]).astype(o_ref.dtype)

    for c in range(n_chunks):
        chunk(c)


def _out_ffn_kernel(x_ref, ymla_ref, yhg_ref, ymem_ref, gmla_ref, w_out_ref, gffn_ref,
                    w_gate_ref, w_up_ref, w_down_ref, o_ref):
    x = x_ref[...].astype(F32)
    ymla = _rms(ymla_ref[...].astype(F32), gmla_ref[...], WIDTH).astype(BF16)
    mix = (_dot(ymla, w_out_ref[0:WIDTH, :])
           + _dot(yhg_ref[...], w_out_ref[WIDTH:2 * WIDTH, :])
           + _dot(ymem_ref[...], w_out_ref[2 * WIDTH:3 * WIDTH, :]))
    x1 = x + mix
    h2 = _rms(x1, gffn_ref[...], x1.shape[-1]).astype(BF16)
    g = _dot(h2, w_gate_ref[...])
    u = _dot(h2, w_up_ref[...])
    act = (g * _sigmoid(g) * u).astype(BF16)
    o_ref[...] = (x1 + _dot(act, w_down_ref[...])).astype(o_ref.dtype)


def _full(shape):
    nd = len(shape)
    return pl.BlockSpec(shape, lambda *_: (0,) * nd)


def _params(sem):
    return pltpu.CompilerParams(dimension_semantics=sem, vmem_limit_bytes=VMEM_LIMIT)


def _row(v):
    return v.reshape(1, -1).astype(F32)


def _layer(x, mem, positions, layer, norm_mix, norm_mem, w_in, q_a_norm, w_uq, kv_a_norm, w_ukv,
           mla_q_norm, mla_k_norm, hg_lb_logits, hg_out_norm, w_mem_kv, mem_q_norm, mem_k_norm,
           mla_out_norm, mem_out_norm, w_out, norm_ffn, w_gate, w_up, w_down):
    B, S, D = x.shape
    M = mem.shape[1]
    T = B * S
    half = D_ROPE // 2
    H = N_HEADS

    assert w_in.shape == (D, W_ALL - D_ROPE)
    wrows = 256
    assert w_uq.shape == (Q_LORA, H * D_QK) and w_ukv.shape == (KV_LORA, H * (D_NOPE + D_V))
    small = [(Q_LORA, H * D_QK_PAD), (KV_LORA, H * D_NOPE), (KV_LORA, H * D_V), (B, M, WIDTH), (B, M, WIDTH)]
    w_all, uq2, w_uk, w_uv, kmem, vmem = pl.pallas_call(
        _w_in_kernel,
        grid=(D // wrows,),
        in_specs=[pl.BlockSpec((w_in.shape[1], wrows), lambda i: (0, i)),
                  _full(w_uq.shape), _full(w_ukv.shape), _full(w_mem_kv.shape), _full(mem.shape),
                  _full((1, D)), _full((1, MEM_D))],
        out_specs=[pl.BlockSpec((wrows, W_ALL), lambda i: (i, 0))] + [_full(s) for s in small],
        out_shape=[jax.ShapeDtypeStruct((D, W_ALL), BF16)] + [jax.ShapeDtypeStruct(s, BF16) for s in small],
        compiler_params=_params(("arbitrary",)),
        name="w_in_layout",
    )(w_in.T, w_uq, w_ukv, w_mem_kv, mem, _row(norm_mem), _row(mem_k_norm))

    pad = jnp.zeros((LANE - D_ROPE,), F32)

    def rotary_gains(g):
        g = g.astype(F32)
        g1 = jnp.concatenate([g[D_NOPE:], pad]).reshape(1, LANE)
        g2 = jnp.concatenate([g[D_NOPE + half:], g[D_NOPE:D_NOPE + half], pad]).reshape(1, LANE)
        return g[:D_NOPE].reshape(1, D_NOPE), g1, g2

    gq_nope, gq_r1, gq_r2 = rotary_gains(mla_q_norm)
    gk_nope, gk_r1, gk_r2 = rotary_gains(mla_k_norm)
    inv_freq = jnp.power(ROPE_THETA, -jnp.arange(half, dtype=F32) / half)
    invf = jnp.tile(inv_freq, LANE // half).reshape(1, LANE)
    phase = jnp.concatenate([jnp.zeros((D_ROPE,), F32), jnp.full((D_ROPE,), -jnp.pi / 2, F32)]).reshape(1, LANE)

    tm = min(512, S)
    assert S % tm == 0
    steps_per_batch = S // tm
    x2 = x.reshape(T, D)
    assert tm % LANE == 0
    pos2 = positions.reshape(T // tm, tm // LANE, LANE).astype(jnp.int32)
    pos_spec = pl.BlockSpec((1, tm // LANE, LANE), lambda i: (i, 0, 0))
    n_tok_steps = T // tm
    pos_next_spec = pl.BlockSpec((1, tm // LANE, LANE), lambda i: (jnp.minimum(i + 1, n_tok_steps - 1), 0, 0))
    sign = jnp.concatenate([jnp.full((D_ROPE,), -1.0, F32), jnp.ones((D_ROPE,), F32)]).reshape(1, LANE)
    tok = lambda w: pl.BlockSpec((tm, w), lambda i: (i, 0))
    col = lambda width, start: pl.BlockSpec((D, width), lambda i: (0, start // width))
    weight_specs = [col(WIDTH, 0), col(KV_LORA, 6 * WIDTH), col(WIDTH, WIDTH), col(WIDTH, 2 * WIDTH),
                    col(WIDTH, 3 * WIDTH), col(WIDTH, 4 * WIDTH), col(WIDTH, 5 * WIDTH)]
    weights = [w_all] * len(weight_specs)
    rest = [_row(q_a_norm), uq2, _row(kv_a_norm), w_uk, w_uv,
            gq_nope, gq_r1, gq_r2, gk_nope, gk_r1, gk_r2, _row(mem_q_norm)]
    mem_spec = pl.BlockSpec((1, M, WIDTH), lambda i: (i // steps_per_batch, 0, 0))
    vt_spec = pl.BlockSpec((1, H, 1, D_V, tm),
                           lambda i: (i // steps_per_batch, 0, i % steps_per_batch, 0, 0))
    q_all, k_all, vt_all, hq, hf, hi, hg, ymem = pl.pallas_call(
        _in_proj_kernel,
        grid=(T // tm,),
        in_specs=([tok(D), pos_spec, pos_next_spec, _full((1, LANE)), _full((1, LANE)), _full((1, LANE)),
                   _full((1, D))]
                  + weight_specs + [_full(r.shape) for r in rest]
                  + [mem_spec, mem_spec, _full((1, WIDTH))]),
        out_specs=[tok(H * D_QK_PAD), tok(H * D_QK_PAD), vt_spec] + [tok(WIDTH)] * 5,
        out_shape=[jax.ShapeDtypeStruct((T, H * D_QK_PAD), BF16)] * 2
        + [jax.ShapeDtypeStruct((B, H, steps_per_batch, D_V, tm), BF16)]
        + [jax.ShapeDtypeStruct((T, WIDTH), dt) for dt in (BF16, F32, BF16, BF16, BF16)],
        scratch_shapes=[pltpu.VMEM((tm, LANE), F32)] * 3,
        compiler_params=_params(("arbitrary",)),
        name="in_proj",
    )(x2, pos2, pos2, invf, phase, sign, _row(norm_mix), *weights, *rest, kmem, vmem, _row(mem_out_norm))

    bk = tm
    bq = 2 * bk
    assert S % bq == 0
    hpb = 2
    att_grid = (B, H // hpb, S // bq)
    n_att = att_grid[0] * att_grid[1] * att_grid[2]

    def cast_spec(w):
        rows = w.shape[0]
        rb = next(r for r in range(16 * -(-rows // (16 * n_att)), rows + 1, 16) if rows % r == 0)
        last = rows // rb - 1
        return pl.BlockSpec((rb, w.shape[1]),
                            lambda b, h, i: (jnp.minimum((b * att_grid[1] + h) * att_grid[2] + i, last), 0))

    ffn_weights = [w_out, w_gate, w_up, w_down]
    cast_specs = [cast_spec(w) for w in ffn_weights]
    y_mla, w_out_b, w_gate_b, w_up_b, w_down_b = pl.pallas_call(
        functools.partial(_attn_kernel, bq=bq, bk=bk, heads=hpb),
        grid=att_grid,
        in_specs=[pl.BlockSpec((1, bq, hpb * D_QK_PAD), lambda b, h, i: (b, i, h)),
                  pl.BlockSpec((1, S, hpb * D_QK_PAD), lambda b, h, i: (b, 0, h)),
                  pl.BlockSpec((1, hpb, S // bk, D_V, bk), lambda b, h, i: (b, h, 0, 0, 0))] + cast_specs,
        out_specs=[pl.BlockSpec((1, bq, hpb * D_V), lambda b, h, i: (b, i, h))] + cast_specs,
        out_shape=[jax.ShapeDtypeStruct((B, S, H * D_V), BF16)]
        + [jax.ShapeDtypeStruct(w.shape, BF16) for w in ffn_weights],
        scratch_shapes=[pltpu.VMEM((hpb, 1, bq), F32),
                        pltpu.VMEM((hpb, D_V + ONES_ROWS, bq), F32),
                        pltpu.VMEM((hpb, bk, bq), F32), pltpu.VMEM((hpb, bk, bq), F32),
                        pltpu.VMEM((hpb, 1, bq), F32), pltpu.VMEM((hpb, 1, bq), F32)],
        compiler_params=_params(("arbitrary", "arbitrary", "arbitrary")),
        name="mla_attn",
    )(q_all.reshape(B, S, H * D_QK_PAD), k_all.reshape(B, S, H * D_QK_PAD), vt_all, *ffn_weights)

    ts = min(1024, S)
    assert S % ts == 0 and ts % HG_CHUNK == 0
    seq = lambda: pl.BlockSpec((1, ts, WIDTH), lambda b, i: (b, i, 0))
    n_layers = hg_lb_logits.shape[0]
    y_hg = pl.pallas_call(
        functools.partial(_hgrn_kernel, n_chunks=ts // HG_CHUNK, layer=layer),
        grid=(B, S // ts),
        in_specs=[seq(), seq(), seq(), seq(), _full((n_layers, WIDTH)), _full((1, WIDTH))],
        out_specs=seq(),
        out_shape=jax.ShapeDtypeStruct((B, S, WIDTH), BF16),
        scratch_shapes=[pltpu.VMEM((N_HEADS, HG_D, HG_D), F32),
                        pltpu.VMEM((ts // HG_CHUNK, HG_CHUNK, WIDTH), F32)],
        compiler_params=_params(("arbitrary", "arbitrary")),
        name="hgrn",
    )(hq.reshape(B, S, WIDTH), hf.reshape(B, S, WIDTH), hi.reshape(B, S, WIDTH), hg.reshape(B, S, WIDTH),
      hg_lb_logits.astype(F32), _row(hg_out_norm))

    d_ff = w_gate.shape[1]
    once = lambda shape: pl.BlockSpec(shape, lambda i: (0, 0), pipeline_mode=pl.Buffered(1))
    out = pl.pallas_call(
        _out_ffn_kernel,
        grid=(T // tm,),
        in_specs=[tok(D), tok(WIDTH), tok(WIDTH), tok(WIDTH), _full((1, WIDTH)), once((3 * WIDTH, D)),
                  _full((1, D)), once((D, d_ff)), once((D, d_ff)), once((d_ff, D))],
        out_specs=tok(D),
        out_shape=jax.ShapeDtypeStruct((T, D), x.dtype),
        compiler_params=_params(("arbitrary",)),
        name="out_ffn",
    )(x2, y_mla.reshape(T, WIDTH), y_hg.reshape(T, WIDTH), ymem, _row(mla_out_norm), w_out_b,
      _row(norm_ffn), w_gate_b, w_up_b, w_down_b)
    return out.reshape(B, S, D)


def kernel(x, mem, positions, norm_mix, norm_mem, w_in, q_a_norm, w_uq, kv_a_norm, w_ukv, mla_q_norm, mla_k_norm, hg_lb_logits, hg_out_norm, w_mem_kv, mem_q_norm, mem_k_norm, mla_out_norm, mem_out_norm, w_out, norm_ffn, w_gate, w_up, w_down):
    depth = w_in.shape[0]
    for l in range(depth):
        x = _layer(x, mem, positions, l, norm_mix[l], norm_mem[l], w_in[l], q_a_norm[l], w_uq[l],
                   kv_a_norm[l], w_ukv[l], mla_q_norm[l], mla_k_norm[l], hg_lb_logits, hg_out_norm[l],
                   w_mem_kv[l], mem_q_norm[l], mem_k_norm[l], mla_out_norm[l], mem_out_norm[l],
                   w_out[l], norm_ffn[l], w_gate[l], w_up[l], w_down[l])
    return x
```

```python
import functools

import jax
import jax.numpy as jnp
from jax import lax
from jax.experimental import pallas as pl
from jax.experimental.pallas import tpu as pltpu

F32 = jnp.float32
BF16 = jnp.bfloat16

EPS = 1e-6
N_HEADS = 4
D_NOPE = 128
D_ROPE = 64
D_QK = D_NOPE + D_ROPE
D_QK_PAD = 256
D_V = 128
Q_LORA = 384
KV_LORA = 256
ROPE_THETA = 10000.0
LOG2E = 1.4426950408889634
HG_D = 128
MEM_D = 128
WIDTH = N_HEADS * 128

LANE = 128
HG_CHUNK = 128
HG_SUB = 8
ONES_ROWS = 16

VMEM_LIMIT = 56 * 1024 * 1024


def _dot(a, b):
    return jnp.dot(a, b, preferred_element_type=F32)


def _dot_nt(a, b):
    return lax.dot_general(a, b, (((1,), (1,)), ((), ())), preferred_element_type=F32)


def _rms(x, g, width):
    ss = jnp.sum(x * x, axis=-1, keepdims=True)
    return x * lax.rsqrt(ss * (1.0 / width) + EPS) * g


def _sigmoid(x):
    return 1.0 / (1.0 + jnp.exp(-x))


W_ALL = Q_LORA + 2 * D_ROPE + 5 * WIDTH + KV_LORA


def _w_in_kernel(wt_ref, uq_ref, ukv_ref, wmem_ref, mem_ref, gmem_ref, gk_ref,
                 o_ref, uq_out, uk_out, uv_out, kmem_out, vmem_out):
    lo = Q_LORA + KV_LORA
    half = D_ROPE // 2

    @pl.when(pl.program_id(0) == 0)
    def _():
        uq = uq_ref[...]
        for hd in range(N_HEADS):
            r = hd * D_QK
            uq_out[:, hd * D_QK_PAD:hd * D_QK_PAD + D_NOPE] = uq[:, r:r + D_NOPE].astype(BF16)
            x1 = uq[:, r + D_NOPE:r + D_NOPE + half]
            x2 = uq[:, r + D_NOPE + half:r + D_QK]
            uq_out[:, hd * D_QK_PAD + D_NOPE:(hd + 1) * D_QK_PAD] = (
                jnp.concatenate([x1, x2, -x2, x1], axis=1).astype(BF16))
            c = hd * (D_NOPE + D_V)
            uk_out[:, hd * D_NOPE:(hd + 1) * D_NOPE] = ukv_ref[:, c:c + D_NOPE].astype(BF16)
            uv_out[:, hd * D_V:(hd + 1) * D_V] = ukv_ref[:, c + D_NOPE:c + D_NOPE + D_V].astype(BF16)
        w_mem = wmem_ref[...].astype(BF16)
        for b in range(mem_ref.shape[0]):
            m = mem_ref[b].astype(F32)
            kv = _dot(_rms(m, gmem_ref[...], m.shape[-1]).astype(BF16), w_mem)
            for hd in range(N_HEADS):
                kh = kv[:, hd * MEM_D:(hd + 1) * MEM_D]
                kmem_out[b, :, hd * MEM_D:(hd + 1) * MEM_D] = _rms(kh, gk_ref[...], MEM_D).astype(BF16)
            vmem_out[b] = kv[:, WIDTH:].astype(BF16)

    def put(dst, rows):
        o_ref[:, dst:dst + rows.shape[0]] = rows.T.astype(BF16)

    for c in range(0, Q_LORA, LANE):
        put(c, wt_ref[c:c + LANE, :])
    put(Q_LORA, jnp.concatenate([wt_ref[lo:lo + D_ROPE, :], -wt_ref[lo + half:lo + D_ROPE, :],
                                 wt_ref[lo:lo + half, :]], axis=0))
    for c in range(0, 5 * WIDTH, LANE):
        put(Q_LORA + LANE + c, wt_ref[lo + D_ROPE + c:lo + D_ROPE + c + LANE, :])
    for c in range(0, KV_LORA, LANE):
        put(Q_LORA + LANE + 5 * WIDTH + c, wt_ref[Q_LORA + c:Q_LORA + c + LANE, :])


def _in_proj_kernel(x_ref, pos_ref, pos_next_ref, invf_ref, phase_ref, sign_ref, gmix_ref,
                    w_cqkr_ref, w_ckv_ref, w_hq_ref, w_hf_ref, w_hi_ref, w_hg_ref, w_mq_ref,
                    gqa_ref, w_uq_ref, gkva_ref, w_uk_ref, w_uv_ref,
                    gq_nope_ref, gq_r1_ref, gq_r2_ref, gk_nope_ref, gk_r1_ref, gk_r2_ref,
                    gmq_ref, kmem_ref, vmem_ref, gmo_ref,
                    q_out, k_out, vt_out, hq_out, hf_out, hi_out, hg_out, ymem_out,
                    cs_sc, tc_sc, ts_sc):
    tm = x_ref.shape[0]

    def fill_cs(p_ref):
        posi = p_ref[0]
        first = jnp.broadcast_to(posi[0:1, 0:1], posi.shape)
        step = (lax.broadcasted_iota(jnp.int32, posi.shape, 0) * LANE
                + lax.broadcasted_iota(jnp.int32, posi.shape, 1))
        gap = jnp.max(jnp.abs((posi - first - step).astype(F32)))

        @pl.when(gap == 0.0)
        def _():
            p0 = jnp.broadcast_to(posi[0:1, 0:1], (8, LANE)).astype(F32)
            a = jnp.cos(p0 * invf_ref[...] + phase_ref[...])
            b = pltpu.roll(a, D_ROPE, axis=1) * sign_ref[...]
            cs_sc[...] = a[0:1] * tc_sc[...] + b[0:1] * ts_sc[...]

        @pl.when(gap != 0.0)
        def _():
            posf = posi.astype(F32)
            pos = jnp.concatenate([jnp.broadcast_to(posf[r:r + 1, :], (LANE, LANE)).T
                                   for r in range(posf.shape[0])], axis=0)
            cs_sc[...] = jnp.cos(pos * invf_ref[...] + phase_ref[...])

    @pl.when(pl.program_id(0) == 0)
    def _():
        tf = lax.broadcasted_iota(jnp.int32, (tm, LANE), 0).astype(F32) * invf_ref[...]
        tc_sc[...] = jnp.cos(tf)
        ts_sc[...] = jnp.sin(tf)
        fill_cs(pos_ref)

    x = x_ref[...].astype(F32)
    h = _rms(x, gmix_ref[...], x.shape[-1]).astype(BF16)

    big = _dot(h, w_cqkr_ref[...])
    ckv = _dot(h, w_ckv_ref[...])
    hq = _dot(h, w_hq_ref[...])
    cqn = _rms(big[:, :Q_LORA], gqa_ref[...], Q_LORA).astype(BF16)
    ckvn = _rms(ckv, gkva_ref[...], KV_LORA).astype(BF16)
    qa = _dot(cqn, w_uq_ref[...])
    kn = _dot(ckvn, w_uk_ref[...])
    vt = _dot(ckvn, w_uv_ref[...]).T
    mq = _dot(h, w_mq_ref[...])
    hf_out[...] = _dot(h, w_hf_ref[...])

    cs = cs_sc[...]
    sc = pltpu.roll(cs, D_ROPE, axis=1)

    def rotary(tile, g1_ref, g2_ref):
        return tile * (g1_ref[...] * cs) + pltpu.roll(tile, D_ROPE, axis=1) * (g2_ref[...] * sc)

    q_scale = LOG2E * D_QK ** -0.5
    for hd in range(N_HEADS):
        nope = qa[:, hd * D_QK_PAD:hd * D_QK_PAD + D_NOPE]
        tile = qa[:, hd * D_QK_PAD + D_NOPE:(hd + 1) * D_QK_PAD]
        ss = jnp.sum(nope * nope, axis=-1, keepdims=True) + 0.5 * jnp.sum(tile * tile, axis=-1, keepdims=True)
        rinv = lax.rsqrt(ss * (1.0 / D_QK) + EPS) * q_scale
        q_out[:, hd * D_QK_PAD:hd * D_QK_PAD + D_NOPE] = (nope * gq_nope_ref[...] * rinv).astype(BF16)
        q_out[:, hd * D_QK_PAD + D_NOPE:(hd + 1) * D_QK_PAD] = (
            rotary(tile, gq_r1_ref, gq_r2_ref) * rinv).astype(BF16)

    vt_out[0, :, 0] = vt.reshape(N_HEADS, D_V, vt.shape[-1]).astype(BF16)
    ktile = big[:, Q_LORA:]
    ss_r = 0.5 * jnp.sum(ktile * ktile, axis=-1, keepdims=True)
    rot = rotary(ktile, gk_r1_ref, gk_r2_ref)
    for hd in range(N_HEADS):
        a = kn[:, hd * D_NOPE:(hd + 1) * D_NOPE]
        ss = jnp.sum(a * a, axis=-1, keepdims=True) + ss_r
        rinv = lax.rsqrt(ss * (1.0 / D_QK) + EPS)
        k_out[:, hd * D_QK_PAD:hd * D_QK_PAD + D_NOPE] = (a * gk_nope_ref[...] * rinv).astype(BF16)
        k_out[:, hd * D_QK_PAD + D_NOPE:(hd + 1) * D_QK_PAD] = (rot * rinv).astype(BF16)

    ss_ = []
    for hd in range(N_HEADS):
        sl = slice(hd * MEM_D, (hd + 1) * MEM_D)
        qh = (_rms(mq[:, sl], gmq_ref[...], MEM_D) * (MEM_D ** -0.5)).astype(BF16)
        ss_.append(_dot_nt(qh, kmem_ref[0, :, sl]))
    hi_out[...] = _dot(h, w_hi_ref[...]).astype(BF16)
    hq_out[...] = (hq * _sigmoid(hq) * (HG_D ** -0.5)).astype(BF16)
    ys = []
    for hd in range(N_HEADS):
        sl = slice(hd * MEM_D, (hd + 1) * MEM_D)
        s = ss_[hd]
        p = jnp.exp(s - jnp.max(s, axis=-1, keepdims=True))
        l = jnp.sum(p, axis=-1, keepdims=True)
        ys.append(_dot(p.astype(BF16), vmem_ref[0, :, sl]) / l)
    hg = _dot(h, w_hg_ref[...])
    hg_out[...] = (hg * _sigmoid(hg)).astype(BF16)
    y = jnp.concatenate(ys, axis=-1)
    ymem_out[...] = _rms(y, gmo_ref[...], WIDTH).astype(BF16)

    fill_cs(pos_next_ref)


def _attn_kernel(q_ref, k_ref, vt_ref, w0_ref, w1_ref, w2_ref, w3_ref,
                 o_ref, w0_out, w1_out, w2_out, w3_out,
                 m_sc, acc_sc, s0_sc, s1_sc, cm0_sc, cm1_sc, *, bq, bk, heads):
    qi = pl.program_id(2)
    for w_ref, w_out in ((w0_ref, w0_out), (w1_ref, w1_out), (w2_ref, w2_out), (w3_ref, w3_out)):
        w_out[...] = w_ref[...].astype(BF16)
    s_bufs = (s0_sc, s1_sc)
    cm_bufs = (cm0_sc, cm1_sc)
    m_sc[...] = jnp.full(m_sc.shape, -jnp.inf, F32)
    acc_sc[...] = jnp.zeros(acc_sc.shape, F32)

    def scores(t, slot, hd, q0=0):
        r0 = t * bk if isinstance(t, int) else pl.multiple_of(t * bk, bk)
        q = q_ref[0, q0:, hd * D_QK_PAD:(hd + 1) * D_QK_PAD]
        k = k_ref[0, pl.ds(r0, bk), hd * D_QK_PAD:(hd + 1) * D_QK_PAD]
        s = _dot_nt(k, q)
        s_bufs[slot][hd, :, q0:] = s
        if q0 == 0:
            cm_bufs[slot][hd] = jnp.max(s, axis=0, keepdims=True)

    def accumulate(t, slot, hd, key_offset=None, q0=0):
        s = s_bufs[slot][hd, :, q0:]
        if key_offset is not None:
            kv = lax.broadcasted_iota(jnp.int32, s.shape, 0) + (key_offset - q0)
            qq = lax.broadcasted_iota(jnp.int32, s.shape, 1)
            s = jnp.where(kv <= qq, s, -jnp.inf)
            cm = jnp.max(s, axis=0, keepdims=True)
        else:
            cm = cm_bufs[slot][hd]
        m_prev = m_sc[hd, :, q0:]
        m_new = jnp.maximum(m_prev, cm)
        p = jnp.exp2(s - m_new)
        alpha = jnp.exp2(m_prev - m_new)
        vt1 = jnp.concatenate([vt_ref[0, hd, t], jnp.ones((ONES_ROWS, bk), BF16)], axis=0)
        acc_sc[hd, :, q0:] = alpha * acc_sc[hd, :, q0:] + _dot(vt1, p.astype(BF16))
        m_sc[hd, :, q0:] = m_new

    for hd in range(heads):
        scores(0, 0, hd)

    def advance(t, slot):
        for hd in range(heads):
            scores(t + 1, 1 - slot, hd)
            accumulate(t, slot, hd)

    def pair(jj):
        advance(2 * jj, 0)
        advance(2 * jj + 1, 1)

    def body(j4, carry):
        pair(2 * j4)
        pair(2 * j4 + 1)
        return carry

    lax.fori_loop(0, jnp.right_shift(qi, 1), body, 0)

    @pl.when((qi & 1) == 1)
    def _():
        pair(qi - 1)
    for hd in range(heads):
        scores(2 * qi + 1, 1, hd, q0=bk)
        accumulate(2 * qi, 0, hd, key_offset=0)
    for hd in range(heads):
        accumulate(2 * qi + 1, 1, hd, key_offset=bk, q0=bk)

    for hd in range(heads):
        o = (acc_sc[hd, :D_V] / acc_sc[hd, D_V:D_V + 1]).T
        o_ref[0, :, hd * D_V:(hd + 1) * D_V] = o.astype(o_ref.dtype)


def _pair_reference(b, m):
    c = b.shape[0]
    n2 = c // (2 * m)
    br = b.reshape(n2, 2 * m, b.shape[1])
    last = br[:, m - 1:m, :]
    return jnp.broadcast_to(last, br.shape).reshape(b.shape)


def _hgrn_kernel(hq_ref, hf_ref, hi_ref, hg_ref, lbl_ref, gain_ref, o_ref, st_ref, b_sc, *, n_chunks, layer):
    C = HG_CHUNK

    @pl.when(pl.program_id(1) == 0)
    def _():
        st_ref[...] = jnp.zeros(st_ref.shape, F32)

    lg = lbl_ref[...].astype(F32)
    e = jnp.exp(lg - jnp.max(lg, axis=0, keepdims=True))
    lb = jnp.sum(e[:layer + 1], axis=0, keepdims=True) / jnp.sum(e, axis=0, keepdims=True)

    row = lax.broadcasted_iota(jnp.int32, (C, C), 0)
    col = lax.broadcasted_iota(jnp.int32, (C, C), 1)
    tri = (col <= row).astype(BF16)
    diag_mask = (((row ^ col) & ~(HG_SUB - 1)) | jnp.where(col <= row, 0, 1)) == 0
    sub_keep = [jnp.where((row & (HG_SUB - 1)) == s_off, 1.0, 0.0).astype(BF16) for s_off in range(HG_SUB)]
    levels = []
    m = HG_SUB
    while m < C:
        bad = ((row ^ col) & ~(2 * m - 1)) | ((row & m) ^ m) | (col & m)
        levels.append((m, bad == 0))
        m *= 2

    def chunk(c):
        r0 = c * C
        fr = hf_ref[0, pl.ds(r0, C), :]
        f = lb + (1.0 - lb) * _sigmoid(fr)
        logf = jnp.log(f) * LOG2E
        kk_all = 1.0 - f
        t0 = logf.astype(BF16)
        r1 = logf - t0.astype(F32)
        t1 = r1.astype(BF16)
        t2 = (r1 - t1.astype(F32)).astype(BF16)
        b_all = _dot(tri, t0) + _dot(tri, t1) + _dot(tri, t2)
        q_all = hq_ref[0, pl.ds(r0, C), :].astype(F32)
        v_all = hi_ref[0, pl.ds(r0, C), :]
        g_all = hg_ref[0, pl.ds(r0, C), :].astype(F32)
        b_sc[c] = b_all
        zero = jnp.zeros((C, HG_D), BF16)

        def pair_nt(x0, x1):
            return jnp.concatenate([jnp.concatenate([x0, zero], axis=1),
                                    jnp.concatenate([zero, x1], axis=1)], axis=0)

        for h0 in range(0, N_HEADS, 2):
            heads = (h0, h0 + 1)
            sls = [slice(hd * HG_D, (hd + 1) * HG_D) for hd in heads]
            bs_ = [b_all[:, sl] for sl in sls]
            qs = [q_all[:, sl] for sl in sls]
            kks = [kk_all[:, sl] for sl in sls]
            vs = [v_all[:, sl] for sl in sls]

            a_ = []
            for b, q, kk, sl in zip(bs_, qs, kks, sls):
                kb = kk.astype(BF16)
                ms = []
                ks = []
                for s_off in range(HG_SUB):
                    bs = jnp.concatenate(
                        [jnp.broadcast_to(b_sc[c, i * HG_SUB + s_off:i * HG_SUB + s_off + 1, sl],
                                          (HG_SUB, HG_D)) for i in range(C // HG_SUB)], axis=0)
                    ms.append((q * jnp.exp2(jnp.minimum(b - bs, 0.0))).astype(BF16))
                    ks.append(kb * sub_keep[s_off])
                ad = _dot_nt(jnp.concatenate(ms, axis=1), jnp.concatenate(ks, axis=1))
                a_.append(jnp.where(diag_mask, ad, 0.0))
            for m_blk, mask in levels:
                qes = []
                kes = []
                for b, q, kk in zip(bs_, qs, kks):
                    d = b - _pair_reference(b, m_blk)
                    qes.append((q * jnp.exp2(d)).astype(BF16))
                    kes.append((kk * jnp.exp2(-d)).astype(BF16))
                al = _dot_nt(jnp.concatenate(qes, axis=1), pair_nt(*kes))
                a_ = [jnp.where(mask, al[:, j * C:(j + 1) * C], a_[j]) for j in range(2)]

            sts = [st_ref[hd] for hd in heads]
            qd = jnp.concatenate([(q * jnp.exp2(b)).astype(BF16) for b, q in zip(bs_, qs)], axis=1)
            a2 = jnp.concatenate([a.astype(BF16) for a in a_], axis=1)
            v2 = jnp.concatenate([jnp.concatenate([vs[0], zero], axis=1),
                                  jnp.concatenate([zero, vs[1]], axis=1)], axis=0)
            o2 = _dot(a2, v2) + _dot_nt(qd, pair_nt(*[st.astype(BF16) for st in sts]))

            for j, hd in enumerate(heads):
                b, kk, v, sl = bs_[j], kks[j], vs[j], sls[j]
                b_last = b[C - 1:C, :]
                kd = (kk * jnp.exp2(b_last - b)).astype(BF16)
                vt = v.astype(F32).T.astype(BF16)
                st_ref[hd] = sts[j] * jnp.exp2(b_last) + _dot(vt, kd)

                on = _rms(o2[:, j * HG_D:(j + 1) * HG_D], gain_ref[:, sl], HG_D)
                o_ref[0, pl.ds(r0, C), sl] = (on * g_all[:, sl]).astype(o_ref.dtype)

    for c in range(n_chunks):
        chunk(c)


def _out_ffn_kernel(x_ref, ymla_ref, yhg_ref, ymem_ref, gmla_ref, w_out_ref, gffn_ref,
                    w_gate_ref, w_up_ref, w_down_ref, o_ref):
    x = x_ref[...].astype(F32)
    ymla = _rms(ymla_ref[...].astype(F32), gmla_ref[...], WIDTH).astype(BF16)
    mix = (_dot(ymla, w_out_ref[0:WIDTH, :])
           + _dot(yhg_ref[...], w_out_ref[WIDTH:2 * WIDTH, :])
           + _dot(ymem_ref[...], w_out_ref[2 * WIDTH:3 * WIDTH, :]))
    x1 = x + mix
    h2 = _rms(x1, gffn_ref[...], x1.shape[-1]).astype(BF16)
    g = _dot(h2, w_gate_ref[...])
    u = _dot(h2, w_up_ref[...])
    act = (g * _sigmoid(g) * u).astype(BF16)
    o_ref[...] = (x1 + _dot(act, w_down_ref[...])).astype(o_ref.dtype)


def _full(shape):
    nd = len(shape)
    return pl.BlockSpec(shape, lambda *_: (0,) * nd)


def _params(sem):
    return pltpu.CompilerParams(dimension_semantics=sem, vmem_limit_bytes=VMEM_LIMIT)


def _row(v):
    return v.reshape(1, -1).astype(F32)


def _layer(x, mem, positions, layer, norm_mix, norm_mem, w_in, q_a_norm, w_uq, kv_a_norm, w_ukv,
           mla_q_norm, mla_k_norm, hg_lb_logits, hg_out_norm, w_mem_kv, mem_q_norm, mem_k_norm,
           mla_out_norm, mem_out_norm, w_out, norm_ffn, w_gate, w_up, w_down):
    B, S, D = x.shape
    M = mem.shape[1]
    T = B * S
    half = D_ROPE // 2
    H = N_HEADS

    assert w_in.shape == (D, W_ALL - D_ROPE)
    wrows = 256
    assert w_uq.shape == (Q_LORA, H * D_QK) and w_ukv.shape == (KV_LORA, H * (D_NOPE + D_V))
    small = [(Q_LORA, H * D_QK_PAD), (KV_LORA, H * D_NOPE), (KV_LORA, H * D_V), (B, M, WIDTH), (B, M, WIDTH)]
    w_all, uq2, w_uk, w_uv, kmem, vmem = pl.pallas_call(
        _w_in_kernel,
        grid=(D // wrows,),
        in_specs=[pl.BlockSpec((w_in.shape[1], wrows), lambda i: (0, i)),
                  _full(w_uq.shape), _full(w_ukv.shape), _full(w_mem_kv.shape), _full(mem.shape),
                  _full((1, D)), _full((1, MEM_D))],
        out_specs=[pl.BlockSpec((wrows, W_ALL), lambda i: (i, 0))] + [_full(s) for s in small],
        out_shape=[jax.ShapeDtypeStruct((D, W_ALL), BF16)] + [jax.ShapeDtypeStruct(s, BF16) for s in small],
        compiler_params=_params(("arbitrary",)),
        name="w_in_layout",
    )(w_in.T, w_uq, w_ukv, w_mem_kv, mem, _row(norm_mem), _row(mem_k_norm))

    pad = jnp.zeros((LANE - D_ROPE,), F32)

    def rotary_gains(g):
        g = g.astype(F32)
        g1 = jnp.concatenate([g[D_NOPE:], pad]).reshape(1, LANE)
        g2 = jnp.concatenate([g[D_NOPE + half:], g[D_NOPE:D_NOPE + half], pad]).reshape(1, LANE)
        return g[:D_NOPE].reshape(1, D_NOPE), g1, g2

    gq_nope, gq_r1, gq_r2 = rotary_gains(mla_q_norm)
    gk_nope, gk_r1, gk_r2 = rotary_gains(mla_k_norm)
    inv_freq = jnp.power(ROPE_THETA, -jnp.arange(half, dtype=F32) / half)
    invf = jnp.tile(inv_freq, LANE // half).reshape(1, LANE)
    phase = jnp.concatenate([jnp.zeros((D_ROPE,), F32), jnp.full((D_ROPE,), -jnp.pi / 2, F32)]).reshape(1, LANE)

    tm = min(512, S)
    assert S % tm == 0
    steps_per_batch = S // tm
    x2 = x.reshape(T, D)
    assert tm % LANE == 0
    pos2 = positions.reshape(T // tm, tm // LANE, LANE).astype(jnp.int32)
    pos_spec = pl.BlockSpec((1, tm // LANE, LANE), lambda i: (i, 0, 0))
    n_tok_steps = T // tm
    pos_next_spec = pl.BlockSpec((1, tm // LANE, LANE), lambda i: (jnp.minimum(i + 1, n_tok_steps - 1), 0, 0))
    sign = jnp.concatenate([jnp.full((D_ROPE,), -1.0, F32), jnp.ones((D_ROPE,), F32)]).reshape(1, LANE)
    tok = lambda w: pl.BlockSpec((tm, w), lambda i: (i, 0))
    col = lambda width, start: pl.BlockSpec((D, width), lambda i: (0, start // width))
    weight_specs = [col(WIDTH, 0), col(KV_LORA, 6 * WIDTH), col(WIDTH, WIDTH), col(WIDTH, 2 * WIDTH),
                    col(WIDTH, 3 * WIDTH), col(WIDTH, 4 * WIDTH), col(WIDTH, 5 * WIDTH)]
    weights = [w_all] * len(weight_specs)
    rest = [_row(q_a_norm), uq2, _row(kv_a_norm), w_uk, w_uv,
            gq_nope, gq_r1, gq_r2, gk_nope, gk_r1, gk_r2, _row(mem_q_norm)]
    mem_spec = pl.BlockSpec((1, M, WIDTH), lambda i: (i // steps_per_batch, 0, 0))
    vt_spec = pl.BlockSpec((1, H, 1, D_V, tm),
                           lambda i: (i // steps_per_batch, 0, i % steps_per_batch, 0, 0))
    q_all, k_all, vt_all, hq, hf, hi, hg, ymem = pl.pallas_call(
        _in_proj_kernel,
        grid=(T // tm,),
        in_specs=([tok(D), pos_spec, pos_next_spec, _full((1, LANE)), _full((1, LANE)), _full((1, LANE)),
                   _full((1, D))]
                  + weight_specs + [_full(r.shape) for r in rest]
                  + [mem_spec, mem_spec, _full((1, WIDTH))]),
        out_specs=[tok(H * D_QK_PAD), tok(H * D_QK_PAD), vt_spec] + [tok(WIDTH)] * 5,
        out_shape=[jax.ShapeDtypeStruct((T, H * D_QK_PAD), BF16)] * 2
        + [jax.ShapeDtypeStruct((B, H, steps_per_batch, D_V, tm), BF16)]
        + [jax.ShapeDtypeStruct((T, WIDTH), dt) for dt in (BF16, F32, BF16, BF16, BF16)],
        scratch_shapes=[pltpu.VMEM((tm, LANE), F32)] * 3,
        compiler_params=_params(("arbitrary",)),
        name="in_proj",
    )(x2, pos2, pos2, invf, phase, sign, _row(norm_mix), *weights, *rest, kmem, vmem, _row(mem_out_norm))

    bk = tm
    bq = 2 * bk
    assert S % bq == 0
    hpb = 2
    att_grid = (B, H // hpb, S // bq)
    n_att = att_grid[0] * att_grid[1] * att_grid[2]

    def cast_spec(w):
        rows = w.shape[0]
        rb = next(r for r in range(16 * -(-rows // (16 * n_att)), rows + 1, 16) if rows % r == 0)
        last = rows // rb - 1
        return pl.BlockSpec((rb, w.shape[1]),
                            lambda b, h, i: (jnp.minimum((b * att_grid[1] + h) * att_grid[2] + i, last), 0))

    ffn_weights = [w_out, w_gate, w_up, w_down]
    cast_specs = [cast_spec(w) for w in ffn_weights]
    y_mla, w_out_b, w_gate_b, w_up_b, w_down_b = pl.pallas_call(
        functools.partial(_attn_kernel, bq=bq, bk=bk, heads=hpb),
        grid=att_grid,
        in_specs=[pl.BlockSpec((1, bq, hpb * D_QK_PAD), lambda b, h, i: (b, i, h)),
                  pl.BlockSpec((1, S, hpb * D_QK_PAD), lambda b, h, i: (b, 0, h)),
                  pl.BlockSpec((1, hpb, S // bk, D_V, bk), lambda b, h, i: (b, h, 0, 0, 0))] + cast_specs,
        out_specs=[pl.BlockSpec((1, bq, hpb * D_V), lambda b, h, i: (b, i, h))] + cast_specs,
        out_shape=[jax.ShapeDtypeStruct((B, S, H * D_V), BF16)]
        + [jax.ShapeDtypeStruct(w.shape, BF16) for w in ffn_weights],
        scratch_shapes=[pltpu.VMEM((hpb, 1, bq), F32),
                        pltpu.VMEM((hpb, D_V + ONES_ROWS, bq), F32),
                        pltpu.VMEM((hpb, bk, bq), F32), pltpu.VMEM((hpb, bk, bq), F32),
                        pltpu.VMEM((hpb, 1, bq), F32), pltpu.VMEM((hpb, 1, bq), F32)],
        compiler_params=_params(("arbitrary", "arbitrary", "arbitrary")),
        name="mla_attn",
    )(q_all.reshape(B, S, H * D_QK_PAD), k_all.reshape(B, S, H * D_QK_PAD), vt_all, *ffn_weights)

    ts = min(2048, S)
    assert S % ts == 0 and ts % HG_CHUNK == 0
    seq = lambda: pl.BlockSpec((1, ts, WIDTH), lambda b, i: (b, i, 0))
    n_layers = hg_lb_logits.shape[0]
    y_hg = pl.pallas_call(
        functools.partial(_hgrn_kernel, n_chunks=ts // HG_CHUNK, layer=layer),
        grid=(B, S // ts),
        in_specs=[seq(), seq(), seq(), seq(), _full((n_layers, WIDTH)), _full((1, WIDTH))],
        out_specs=seq(),
        out_shape=jax.ShapeDtypeStruct((B, S, WIDTH), BF16),
        scratch_shapes=[pltpu.VMEM((N_HEADS, HG_D, HG_D), F32),
                        pltpu.VMEM((ts // HG_CHUNK, HG_CHUNK, WIDTH), F32)],
        compiler_params=_params(("arbitrary", "arbitrary")),
        name="hgrn",
    )(hq.reshape(B, S, WIDTH), hf.reshape(B, S, WIDTH), hi.reshape(B, S, WIDTH), hg.reshape(B, S, WIDTH),
      hg_lb_logits.astype(F32), _row(hg_out_norm))

    d_ff = w_gate.shape[1]
    once = lambda shape: pl.BlockSpec(shape, lambda i: (0, 0), pipeline_mode=pl.Buffered(1))
    out = pl.pallas_call(
        _out_ffn_kernel,
        grid=(T // tm,),
        in_specs=[tok(D), tok(WIDTH), tok(WIDTH), tok(WIDTH), _full((1, WIDTH)), once((3 * WIDTH, D)),
                  _full((1, D)), once((D, d_ff)), once((D, d_ff)), once((d_ff, D))],
        out_specs=tok(D),
        out_shape=jax.ShapeDtypeStruct((T, D), x.dtype),
        compiler_params=_params(("arbitrary",)),
        name="out_ffn",
    )(x2, y_mla.reshape(T, WIDTH), y_hg.reshape(T, WIDTH), ymem, _row(mla_out_norm), w_out_b,
      _row(norm_ffn), w_gate_b, w_up_b, w_down_b)
    return out.reshape(B, S, D)


def kernel(x, mem, positions, norm_mix, norm_mem, w_in, q_a_norm, w_uq, kv_a_norm, w_ukv, mla_q_norm, mla_k_norm, hg_lb_logits, hg_out_norm, w_mem_kv, mem_q_norm, mem_k_norm, mla_out_norm, mem_out_norm, w_out, norm_ffn, w_gate, w_up, w_down):
    depth = w_in.shape[0]
    for l in range(depth):
        x = _layer(x, mem, positions, l, norm_mix[l], norm_mem[l], w_in[l], q_a_norm[l], w_uq[l],
                   kv_a_norm[l], w_ukv[l], mla_q_norm[l], mla_k_norm[l], hg_lb_logits, hg_out_norm[l],
                   w_mem_kv[l], mem_q_norm[l], mem_k_norm[l], mla_out_norm[l], mem_out_norm[l],
                   w_out[l], norm_ffn[l], w_gate[l], w_up[l], w_down[l])
    return x
```

```python
import functools

import jax
import jax.numpy as jnp
from jax import lax
from jax.experimental import pallas as pl
from jax.experimental.pallas import tpu as pltpu

F32 = jnp.float32
BF16 = jnp.bfloat16

EPS = 1e-6
N_HEADS = 4
D_NOPE = 128
D_ROPE = 64
D_QK = D_NOPE + D_ROPE
D_QK_PAD = 256
D_V = 128
Q_LORA = 384
KV_LORA = 256
ROPE_THETA = 10000.0
LOG2E = 1.4426950408889634
HG_D = 128
MEM_D = 128
WIDTH = N_HEADS * 128

LANE = 128
HG_CHUNK = 128
HG_SUB = 8
ONES_ROWS = 16

VMEM_LIMIT = 56 * 1024 * 1024


def _dot(a, b):
    return jnp.dot(a, b, preferred_element_type=F32)


def _dot_nt(a, b):
    return lax.dot_general(a, b, (((1,), (1,)), ((), ())), preferred_element_type=F32)


def _rms(x, g, width):
    ss = jnp.sum(x * x, axis=-1, keepdims=True)
    return x * lax.rsqrt(ss * (1.0 / width) + EPS) * g


def _sigmoid(x):
    return 1.0 / (1.0 + jnp.exp(-x))


W_ALL = Q_LORA + 2 * D_ROPE + 5 * WIDTH + KV_LORA


def _w_in_kernel(wt_ref, uq_ref, ukv_ref, wmem_ref, mem_ref, gmem_ref, gk_ref,
                 o_ref, uq_out, uk_out, uv_out, kmem_out, vmem_out):
    lo = Q_LORA + KV_LORA
    half = D_ROPE // 2

    @pl.when(pl.program_id(0) == 0)
    def _():
        uq = uq_ref[...]
        for hd in range(N_HEADS):
            r = hd * D_QK
            uq_out[:, hd * D_QK_PAD:hd * D_QK_PAD + D_NOPE] = uq[:, r:r + D_NOPE].astype(BF16)
            x1 = uq[:, r + D_NOPE:r + D_NOPE + half]
            x2 = uq[:, r + D_NOPE + half:r + D_QK]
            uq_out[:, hd * D_QK_PAD + D_NOPE:(hd + 1) * D_QK_PAD] = (
                jnp.concatenate([x1, x2, -x2, x1], axis=1).astype(BF16))
            c = hd * (D_NOPE + D_V)
            uk_out[:, hd * D_NOPE:(hd + 1) * D_NOPE] = ukv_ref[:, c:c + D_NOPE].astype(BF16)
            uv_out[:, hd * D_V:(hd + 1) * D_V] = ukv_ref[:, c + D_NOPE:c + D_NOPE + D_V].astype(BF16)
        w_mem = wmem_ref[...].astype(BF16)
        for b in range(mem_ref.shape[0]):
            m = mem_ref[b].astype(F32)
            kv = _dot(_rms(m, gmem_ref[...], m.shape[-1]).astype(BF16), w_mem)
            for hd in range(N_HEADS):
                kh = kv[:, hd * MEM_D:(hd + 1) * MEM_D]
                kmem_out[b, :, hd * MEM_D:(hd + 1) * MEM_D] = _rms(kh, gk_ref[...], MEM_D).astype(BF16)
            vmem_out[b] = kv[:, WIDTH:].astype(BF16)

    def put(dst, rows):
        o_ref[:, dst:dst + rows.shape[0]] = rows.T.astype(BF16)

    for c in range(0, Q_LORA, LANE):
        put(c, wt_ref[c:c + LANE, :])
    put(Q_LORA, jnp.concatenate([wt_ref[lo:lo + D_ROPE, :], -wt_ref[lo + half:lo + D_ROPE, :],
                                 wt_ref[lo:lo + half, :]], axis=0))
    for c in range(0, 5 * WIDTH, LANE):
        put(Q_LORA + LANE + c, wt_ref[lo + D_ROPE + c:lo + D_ROPE + c + LANE, :])
    for c in range(0, KV_LORA, LANE):
        put(Q_LORA + LANE + 5 * WIDTH + c, wt_ref[Q_LORA + c:Q_LORA + c + LANE, :])


def _in_proj_kernel(x_ref, pos_ref, pos_next_ref, invf_ref, phase_ref, sign_ref, gmix_ref,
                    w_cqkr_ref, w_ckv_ref, w_hq_ref, w_hf_ref, w_hi_ref, w_hg_ref, w_mq_ref,
                    gqa_ref, w_uq_ref, gkva_ref, w_uk_ref, w_uv_ref,
                    gq_nope_ref, gq_r1_ref, gq_r2_ref, gk_nope_ref, gk_r1_ref, gk_r2_ref,
                    gmq_ref, kmem_ref, vmem_ref, gmo_ref,
                    q_out, k_out, vt_out, hq_out, hf_out, hi_out, hg_out, ymem_out,
                    cs_sc, tc_sc, ts_sc):
    tm = x_ref.shape[0]

    def fill_cs(p_ref):
        posi = p_ref[0]
        first = jnp.broadcast_to(posi[0:1, 0:1], posi.shape)
        step = (lax.broadcasted_iota(jnp.int32, posi.shape, 0) * LANE
                + lax.broadcasted_iota(jnp.int32, posi.shape, 1))
        gap = jnp.max(jnp.abs((posi - first - step).astype(F32)))

        @pl.when(gap == 0.0)
        def _():
            p0 = jnp.broadcast_to(posi[0:1, 0:1], (8, LANE)).astype(F32)
            a = jnp.cos(p0 * invf_ref[...] + phase_ref[...])
            b = pltpu.roll(a, D_ROPE, axis=1) * sign_ref[...]
            cs_sc[...] = a[0:1] * tc_sc[...] + b[0:1] * ts_sc[...]

        @pl.when(gap != 0.0)
        def _():
            posf = posi.astype(F32)
            pos = jnp.concatenate([jnp.broadcast_to(posf[r:r + 1, :], (LANE, LANE)).T
                                   for r in range(posf.shape[0])], axis=0)
            cs_sc[...] = jnp.cos(pos * invf_ref[...] + phase_ref[...])

    @pl.when(pl.program_id(0) == 0)
    def _():
        tf = lax.broadcasted_iota(jnp.int32, (tm, LANE), 0).astype(F32) * invf_ref[...]
        tc_sc[...] = jnp.cos(tf)
        ts_sc[...] = jnp.sin(tf)
        fill_cs(pos_ref)

    x = x_ref[...].astype(F32)
    h = _rms(x, gmix_ref[...], x.shape[-1]).astype(BF16)

    big = _dot(h, w_cqkr_ref[...])
    ckv = _dot(h, w_ckv_ref[...])
    hq = _dot(h, w_hq_ref[...])
    cqn = _rms(big[:, :Q_LORA], gqa_ref[...], Q_LORA).astype(BF16)
    ckvn = _rms(ckv, gkva_ref[...], KV_LORA).astype(BF16)
    qa = _dot(cqn, w_uq_ref[...])
    kn = _dot(ckvn, w_uk_ref[...])
    vt = _dot(ckvn, w_uv_ref[...]).T
    mq = _dot(h, w_mq_ref[...])
    hf_out[...] = _dot(h, w_hf_ref[...])

    cs = cs_sc[...]
    sc = pltpu.roll(cs, D_ROPE, axis=1)

    def rotary(tile, g1_ref, g2_ref):
        return tile * (g1_ref[...] * cs) + pltpu.roll(tile, D_ROPE, axis=1) * (g2_ref[...] * sc)

    q_scale = LOG2E * D_QK ** -0.5
    for hd in range(N_HEADS):
        nope = qa[:, hd * D_QK_PAD:hd * D_QK_PAD + D_NOPE]
        tile = qa[:, hd * D_QK_PAD + D_NOPE:(hd + 1) * D_QK_PAD]
        ss = jnp.sum(nope * nope + 0.5 * (tile * tile), axis=-1, keepdims=True)
        rinv = lax.rsqrt(ss * (1.0 / D_QK) + EPS) * q_scale
        q_out[:, hd * D_QK_PAD:hd * D_QK_PAD + D_NOPE] = (nope * gq_nope_ref[...] * rinv).astype(BF16)
        q_out[:, hd * D_QK_PAD + D_NOPE:(hd + 1) * D_QK_PAD] = (
            rotary(tile, gq_r1_ref, gq_r2_ref) * rinv).astype(BF16)

    vt_out[0, :, 0] = vt.reshape(N_HEADS, D_V, vt.shape[-1]).astype(BF16)
    ktile = big[:, Q_LORA:]
    sq_r = 0.5 * (ktile * ktile)
    rot = rotary(ktile, gk_r1_ref, gk_r2_ref)
    for hd in range(N_HEADS):
        a = kn[:, hd * D_NOPE:(hd + 1) * D_NOPE]
        ss = jnp.sum(a * a + sq_r, axis=-1, keepdims=True)
        rinv = lax.rsqrt(ss * (1.0 / D_QK) + EPS)
        k_out[:, hd * D_QK_PAD:hd * D_QK_PAD + D_NOPE] = (a * gk_nope_ref[...] * rinv).astype(BF16)
        k_out[:, hd * D_QK_PAD + D_NOPE:(hd + 1) * D_QK_PAD] = (rot * rinv).astype(BF16)

    ss_ = []
    for hd in range(N_HEADS):
        sl = slice(hd * MEM_D, (hd + 1) * MEM_D)
        qh = (_rms(mq[:, sl], gmq_ref[...], MEM_D) * (MEM_D ** -0.5)).astype(BF16)
        ss_.append(_dot_nt(qh, kmem_ref[0, :, sl]))
    hi_out[...] = _dot(h, w_hi_ref[...]).astype(BF16)
    hq_out[...] = (hq * _sigmoid(hq) * (HG_D ** -0.5)).astype(BF16)
    ys = []
    for hd in range(N_HEADS):
        sl = slice(hd * MEM_D, (hd + 1) * MEM_D)
        s = ss_[hd]
        p = jnp.exp(s - jnp.max(s, axis=-1, keepdims=True))
        l = jnp.sum(p, axis=-1, keepdims=True)
        ys.append(_dot(p.astype(BF16), vmem_ref[0, :, sl]) / l)
    hg = _dot(h, w_hg_ref[...])
    hg_out[...] = (hg * _sigmoid(hg)).astype(BF16)
    y = jnp.concatenate(ys, axis=-1)
    ymem_out[...] = _rms(y, gmo_ref[...], WIDTH).astype(BF16)

    fill_cs(pos_next_ref)


def _attn_kernel(q_ref, k_ref, vt_ref, w0_ref, w1_ref, w2_ref, w3_ref,
                 o_ref, w0_out, w1_out, w2_out, w3_out,
                 m_sc, acc_sc, s0_sc, s1_sc, cm0_sc, cm1_sc, *, bq, bk, heads):
    qi = pl.program_id(2)
    for w_ref, w_out in ((w0_ref, w0_out), (w1_ref, w1_out), (w2_ref, w2_out), (w3_ref, w3_out)):
        w_out[...] = w_ref[...].astype(BF16)
    s_bufs = (s0_sc, s1_sc)
    cm_bufs = (cm0_sc, cm1_sc)
    m_sc[...] = jnp.full(m_sc.shape, -jnp.inf, F32)
    acc_sc[...] = jnp.zeros(acc_sc.shape, F32)

    def scores(t, slot, hd, q0=0):
        r0 = t * bk if isinstance(t, int) else pl.multiple_of(t * bk, bk)
        q = q_ref[0, q0:, hd * D_QK_PAD:(hd + 1) * D_QK_PAD]
        k = k_ref[0, pl.ds(r0, bk), hd * D_QK_PAD:(hd + 1) * D_QK_PAD]
        s = _dot_nt(k, q)
        s_bufs[slot][hd, :, q0:] = s
        if q0 == 0:
            cm_bufs[slot][hd] = jnp.max(s, axis=0, keepdims=True)

    def accumulate(t, slot, hd, key_offset=None, q0=0):
        s = s_bufs[slot][hd, :, q0:]
        if key_offset is not None:
            kv = lax.broadcasted_iota(jnp.int32, s.shape, 0) + (key_offset - q0)
            qq = lax.broadcasted_iota(jnp.int32, s.shape, 1)
            s = jnp.where(kv <= qq, s, -jnp.inf)
            cm = jnp.max(s, axis=0, keepdims=True)
        else:
            cm = cm_bufs[slot][hd]
        m_prev = m_sc[hd, :, q0:]
        m_new = jnp.maximum(m_prev, cm)
        p = jnp.exp2(s - m_new)
        alpha = jnp.exp2(m_prev - m_new)
        vt1 = jnp.concatenate([vt_ref[0, hd, t], jnp.ones((ONES_ROWS, bk), BF16)], axis=0)
        acc_sc[hd, :, q0:] = alpha * acc_sc[hd, :, q0:] + _dot(vt1, p.astype(BF16))
        m_sc[hd, :, q0:] = m_new

    for hd in range(heads):
        scores(0, 0, hd)

    def advance(t, slot):
        for hd in range(heads):
            scores(t + 1, 1 - slot, hd)
            accumulate(t, slot, hd)

    def pair(jj):
        advance(2 * jj, 0)
        advance(2 * jj + 1, 1)

    def body(j4, carry):
        pair(2 * j4)
        pair(2 * j4 + 1)
        return carry

    lax.fori_loop(0, jnp.right_shift(qi, 1), body, 0)

    @pl.when((qi & 1) == 1)
    def _():
        pair(qi - 1)
    for hd in range(heads):
        scores(2 * qi + 1, 1, hd, q0=bk)
        accumulate(2 * qi, 0, hd, key_offset=0)
    for hd in range(heads):
        accumulate(2 * qi + 1, 1, hd, key_offset=bk, q0=bk)

    for hd in range(heads):
        o = (acc_sc[hd, :D_V] / acc_sc[hd, D_V:D_V + 1]).T
        o_ref[0, :, hd * D_V:(hd + 1) * D_V] = o.astype(o_ref.dtype)


def _pair_reference(b, m):
    c = b.shape[0]
    n2 = c // (2 * m)
    br = b.reshape(n2, 2 * m, b.shape[1])
    last = br[:, m - 1:m, :]
    return jnp.broadcast_to(last, br.shape).reshape(b.shape)


def _hgrn_kernel(hq_ref, hf_ref, hi_ref, hg_ref, lbl_ref, gain_ref, o_ref, st_ref, b_sc, *, n_chunks, layer):
    C = HG_CHUNK

    @pl.when(pl.program_id(1) == 0)
    def _():
        st_ref[...] = jnp.zeros(st_ref.shape, F32)

    lg = lbl_ref[...].astype(F32)
    e = jnp.exp(lg - jnp.max(lg, axis=0, keepdims=True))
    lb = jnp.sum(e[:layer + 1], axis=0, keepdims=True) / jnp.sum(e, axis=0, keepdims=True)

    row = lax.broadcasted_iota(jnp.int32, (C, C), 0)
    col = lax.broadcasted_iota(jnp.int32, (C, C), 1)
    tri = (col <= row).astype(BF16)
    diag_mask = (((row ^ col) & ~(HG_SUB - 1)) | jnp.where(col <= row, 0, 1)) == 0
    sub_keep = [jnp.where((row & (HG_SUB - 1)) == s_off, 1.0, 0.0).astype(BF16) for s_off in range(HG_SUB)]
    levels = []
    m = HG_SUB
    while m < C:
        bad = ((row ^ col) & ~(2 * m - 1)) | ((row & m) ^ m) | (col & m)
        levels.append((m, bad == 0))
        m *= 2

    def chunk(c):
        r0 = c * C
        fr = hf_ref[0, pl.ds(r0, C), :]
        f = lb + (1.0 - lb) * _sigmoid(fr)
        logf = jnp.log(f) * LOG2E
        kk_all = 1.0 - f
        t0 = logf.astype(BF16)
        r1 = logf - t0.astype(F32)
        t1 = r1.astype(BF16)
        t2 = (r1 - t1.astype(F32)).astype(BF16)
        b_all = _dot(tri, t0) + _dot(tri, t1) + _dot(tri, t2)
        q_all = hq_ref[0, pl.ds(r0, C), :].astype(F32)
        v_all = hi_ref[0, pl.ds(r0, C), :]
        g_all = hg_ref[0, pl.ds(r0, C), :].astype(F32)
        b_sc[c] = b_all
        zero = jnp.zeros((C, HG_D), BF16)

        def pair_nt(x0, x1):
            return jnp.concatenate([jnp.concatenate([x0, zero], axis=1),
                                    jnp.concatenate([zero, x1], axis=1)], axis=0)

        for h0 in range(0, N_HEADS, 2):
            heads = (h0, h0 + 1)
            sls = [slice(hd * HG_D, (hd + 1) * HG_D) for hd in heads]
            bs_ = [b_all[:, sl] for sl in sls]
            qs = [q_all[:, sl] for sl in sls]
            kks = [kk_all[:, sl] for sl in sls]
            vs = [v_all[:, sl] for sl in sls]

            a_ = []
            for b, q, kk, sl in zip(bs_, qs, kks, sls):
                kb = kk.astype(BF16)
                ms = []
                ks = []
                for s_off in range(HG_SUB):
                    bs = jnp.concatenate(
                        [jnp.broadcast_to(b_sc[c, i * HG_SUB + s_off:i * HG_SUB + s_off + 1, sl],
                                          (HG_SUB, HG_D)) for i in range(C // HG_SUB)], axis=0)
                    ms.append((q * jnp.exp2(jnp.minimum(b - bs, 0.0))).astype(BF16))
                    ks.append(kb * sub_keep[s_off])
                ad = _dot_nt(jnp.concatenate(ms, axis=1), jnp.concatenate(ks, axis=1))
                a_.append(jnp.where(diag_mask, ad, 0.0))
            for m_blk, mask in levels:
                qes = []
                kes = []
                for b, q, kk in zip(bs_, qs, kks):
                    d = b - _pair_reference(b, m_blk)
                    qes.append((q * jnp.exp2(d)).astype(BF16))
                    kes.append((kk * jnp.exp2(-d)).astype(BF16))
                al = _dot_nt(jnp.concatenate(qes, axis=1), pair_nt(*kes))
                a_ = [jnp.where(mask, al[:, j * C:(j + 1) * C], a_[j]) for j in range(2)]

            sts = [st_ref[hd] for hd in heads]
            qd = jnp.concatenate([(q * jnp.exp2(b)).astype(BF16) for b, q in zip(bs_, qs)], axis=1)
            a2 = jnp.concatenate([a.astype(BF16) for a in a_], axis=1)
            v2 = jnp.concatenate([jnp.concatenate([vs[0], zero], axis=1),
                                  jnp.concatenate([zero, vs[1]], axis=1)], axis=0)
            o2 = _dot(a2, v2) + _dot_nt(qd, pair_nt(*[st.astype(BF16) for st in sts]))

            for j, hd in enumerate(heads):
                b, kk, v, sl = bs_[j], kks[j], vs[j], sls[j]
                b_last = b[C - 1:C, :]
                kd = (kk * jnp.exp2(b_last - b)).astype(BF16)
                vt = v.astype(F32).T.astype(BF16)
                st_ref[hd] = sts[j] * jnp.exp2(b_last) + _dot(vt, kd)

                on = _rms(o2[:, j * HG_D:(j + 1) * HG_D], gain_ref[:, sl], HG_D)
                o_ref[0, pl.ds(r0, C), sl] = (on * g_all[:, sl]).astype(o_ref.dtype)

    for c in range(n_chunks):
        chunk(c)


def _out_ffn_kernel(x_ref, ymla_ref, yhg_ref, ymem_ref, gmla_ref, w_out_ref, gffn_ref,
                    w_gate_ref, w_up_ref, w_down_ref, o_ref):
    x = x_ref[...].astype(F32)
    ymla = _rms(ymla_ref[...].astype(F32), gmla_ref[...], WIDTH).astype(BF16)
    mix = (_dot(ymla, w_out_ref[0:WIDTH, :])
           + _dot(yhg_ref[...], w_out_ref[WIDTH:2 * WIDTH, :])
           + _dot(ymem_ref[...], w_out_ref[2 * WIDTH:3 * WIDTH, :]))
    x1 = x + mix
    h2 = _rms(x1, gffn_ref[...], x1.shape[-1]).astype(BF16)
    g = _dot(h2, w_gate_ref[...])
    u = _dot(h2, w_up_ref[...])
    act = (g * _sigmoid(g) * u).astype(BF16)
    o_ref[...] = (x1 + _dot(act, w_down_ref[...])).astype(o_ref.dtype)


def _full(shape):
    nd = len(shape)
    return pl.BlockSpec(shape, lambda *_: (0,) * nd)


def _params(sem):
    return pltpu.CompilerParams(dimension_semantics=sem, vmem_limit_bytes=VMEM_LIMIT)


def _row(v):
    return v.reshape(1, -1).astype(F32)


def _layer(x, mem, positions, layer, norm_mix, norm_mem, w_in, q_a_norm, w_uq, kv_a_norm, w_ukv,
           mla_q_norm, mla_k_norm, hg_lb_logits, hg_out_norm, w_mem_kv, mem_q_norm, mem_k_norm,
           mla_out_norm, mem_out_norm, w_out, norm_ffn, w_gate, w_up, w_down):
    B, S, D = x.shape
    M = mem.shape[1]
    T = B * S
    half = D_ROPE // 2
    H = N_HEADS

    assert w_in.shape == (D, W_ALL - D_ROPE)
    wrows = 256
    assert w_uq.shape == (Q_LORA, H * D_QK) and w_ukv.shape == (KV_LORA, H * (D_NOPE + D_V))
    small = [(Q_LORA, H * D_QK_PAD), (KV_LORA, H * D_NOPE), (KV_LORA, H * D_V), (B, M, WIDTH), (B, M, WIDTH)]
    w_all, uq2, w_uk, w_uv, kmem, vmem = pl.pallas_call(
        _w_in_kernel,
        grid=(D // wrows,),
        in_specs=[pl.BlockSpec((w_in.shape[1], wrows), lambda i: (0, i)),
                  _full(w_uq.shape), _full(w_ukv.shape), _full(w_mem_kv.shape), _full(mem.shape),
                  _full((1, D)), _full((1, MEM_D))],
        out_specs=[pl.BlockSpec((wrows, W_ALL), lambda i: (i, 0))] + [_full(s) for s in small],
        out_shape=[jax.ShapeDtypeStruct((D, W_ALL), BF16)] + [jax.ShapeDtypeStruct(s, BF16) for s in small],
        compiler_params=_params(("arbitrary",)),
        name="w_in_layout",
    )(w_in.T, w_uq, w_ukv, w_mem_kv, mem, _row(norm_mem), _row(mem_k_norm))

    pad = jnp.zeros((LANE - D_ROPE,), F32)

    def rotary_gains(g):
        g = g.astype(F32)
        g1 = jnp.concatenate([g[D_NOPE:], pad]).reshape(1, LANE)
        g2 = jnp.concatenate([g[D_NOPE + half:], g[D_NOPE:D_NOPE + half], pad]).reshape(1, LANE)
        return g[:D_NOPE].reshape(1, D_NOPE), g1, g2

    gq_nope, gq_r1, gq_r2 = rotary_gains(mla_q_norm)
    gk_nope, gk_r1, gk_r2 = rotary_gains(mla_k_norm)
    inv_freq = jnp.power(ROPE_THETA, -jnp.arange(half, dtype=F32) / half)
    invf = jnp.tile(inv_freq, LANE // half).reshape(1, LANE)
    phase = jnp.concatenate([jnp.zeros((D_ROPE,), F32), jnp.full((D_ROPE,), -jnp.pi / 2, F32)]).reshape(1, LANE)

    tm = min(512, S)
    assert S % tm == 0
    steps_per_batch = S // tm
    x2 = x.reshape(T, D)
    assert tm % LANE == 0
    pos2 = positions.reshape(T // tm, tm // LANE, LANE).astype(jnp.int32)
    pos_spec = pl.BlockSpec((1, tm // LANE, LANE), lambda i: (i, 0, 0))
    n_tok_steps = T // tm
    pos_next_spec = pl.BlockSpec((1, tm // LANE, LANE), lambda i: (jnp.minimum(i + 1, n_tok_steps - 1), 0, 0))
    sign = jnp.concatenate([jnp.full((D_ROPE,), -1.0, F32), jnp.ones((D_ROPE,), F32)]).reshape(1, LANE)
    tok = lambda w: pl.BlockSpec((tm, w), lambda i: (i, 0))
    col = lambda width, start: pl.BlockSpec((D, width), lambda i: (0, start // width))
    weight_specs = [col(WIDTH, 0), col(KV_LORA, 6 * WIDTH), col(WIDTH, WIDTH), col(WIDTH, 2 * WIDTH),
                    col(WIDTH, 3 * WIDTH), col(WIDTH, 4 * WIDTH), col(WIDTH, 5 * WIDTH)]
    weights = [w_all] * len(weight_specs)
    rest = [_row(q_a_norm), uq2, _row(kv_a_norm), w_uk, w_uv,
            gq_nope, gq_r1, gq_r2, gk_nope, gk_r1, gk_r2, _row(mem_q_norm)]
    mem_spec = pl.BlockSpec((1, M, WIDTH), lambda i: (i // steps_per_batch, 0, 0))
    vt_spec = pl.BlockSpec((1, H, 1, D_V, tm),
                           lambda i: (i // steps_per_batch, 0, i % steps_per_batch, 0, 0))
    q_all, k_all, vt_all, hq, hf, hi, hg, ymem = pl.pallas_call(
        _in_proj_kernel,
        grid=(T // tm,),
        in_specs=([tok(D), pos_spec, pos_next_spec, _full((1, LANE)), _full((1, LANE)), _full((1, LANE)),
                   _full((1, D))]
                  + weight_specs + [_full(r.shape) for r in rest]
                  + [mem_spec, mem_spec, _full((1, WIDTH))]),
        out_specs=[tok(H * D_QK_PAD), tok(H * D_QK_PAD), vt_spec] + [tok(WIDTH)] * 5,
        out_shape=[jax.ShapeDtypeStruct((T, H * D_QK_PAD), BF16)] * 2
        + [jax.ShapeDtypeStruct((B, H, steps_per_batch, D_V, tm), BF16)]
        + [jax.ShapeDtypeStruct((T, WIDTH), dt) for dt in (BF16, F32, BF16, BF16, BF16)],
        scratch_shapes=[pltpu.VMEM((tm, LANE), F32)] * 3,
        compiler_params=_params(("arbitrary",)),
        name="in_proj",
    )(x2, pos2, pos2, invf, phase, sign, _row(norm_mix), *weights, *rest, kmem, vmem, _row(mem_out_norm))

    bk = tm
    bq = 2 * bk
    assert S % bq == 0
    hpb = 2
    att_grid = (B, H // hpb, S // bq)
    n_att = att_grid[0] * att_grid[1] * att_grid[2]

    def cast_spec(w):
        rows = w.shape[0]
        rb = next(r for r in range(16 * -(-rows // (16 * n_att)), rows + 1, 16) if rows % r == 0)
        last = rows // rb - 1
        return pl.BlockSpec((rb, w.shape[1]),
                            lambda b, h, i: (jnp.minimum((b * att_grid[1] + h) * att_grid[2] + i, last), 0))

    ffn_weights = [w_out, w_gate, w_up, w_down]
    cast_specs = [cast_spec(w) for w in ffn_weights]
    y_mla, w_out_b, w_gate_b, w_up_b, w_down_b = pl.pallas_call(
        functools.partial(_attn_kernel, bq=bq, bk=bk, heads=hpb),
        grid=att_grid,
        in_specs=[pl.BlockSpec((1, bq, hpb * D_QK_PAD), lambda b, h, i: (b, i, h)),
                  pl.BlockSpec((1, S, hpb * D_QK_PAD), lambda b, h, i: (b, 0, h)),
                  pl.BlockSpec((1, hpb, S // bk, D_V, bk), lambda b, h, i: (b, h, 0, 0, 0))] + cast_specs,
        out_specs=[pl.BlockSpec((1, bq, hpb * D_V), lambda b, h, i: (b, i, h))] + cast_specs,
        out_shape=[jax.ShapeDtypeStruct((B, S, H * D_V), BF16)]
        + [jax.ShapeDtypeStruct(w.shape, BF16) for w in ffn_weights],
        scratch_shapes=[pltpu.VMEM((hpb, 1, bq), F32),
                        pltpu.VMEM((hpb, D_V + ONES_ROWS, bq), F32),
                        pltpu.VMEM((hpb, bk, bq), F32), pltpu.VMEM((hpb, bk, bq), F32),
                        pltpu.VMEM((hpb, 1, bq), F32), pltpu.VMEM((hpb, 1, bq), F32)],
        compiler_params=_params(("arbitrary", "arbitrary", "arbitrary")),
        name="mla_attn",
    )(q_all.reshape(B, S, H * D_QK_PAD), k_all.reshape(B, S, H * D_QK_PAD), vt_all, *ffn_weights)

    ts = min(1024, S)
    assert S % ts == 0 and ts % HG_CHUNK == 0
    seq = lambda: pl.BlockSpec((1, ts, WIDTH), lambda b, i: (b, i, 0))
    n_layers = hg_lb_logits.shape[0]
    y_hg = pl.pallas_call(
        functools.partial(_hgrn_kernel, n_chunks=ts // HG_CHUNK, layer=layer),
        grid=(B, S // ts),
        in_specs=[seq(), seq(), seq(), seq(), _full((n_layers, WIDTH)), _full((1, WIDTH))],
        out_specs=seq(),
        out_shape=jax.ShapeDtypeStruct((B, S, WIDTH), BF16),
        scratch_shapes=[pltpu.VMEM((N_HEADS, HG_D, HG_D), F32),
                        pltpu.VMEM((ts // HG_CHUNK, HG_CHUNK, WIDTH), F32)],
        compiler_params=_params(("arbitrary", "arbitrary")),
        name="hgrn",
    )(hq.reshape(B, S, WIDTH), hf.reshape(B, S, WIDTH), hi.reshape(B, S, WIDTH), hg.reshape(B, S, WIDTH),
      hg_lb_logits.astype(F32), _row(hg_out_norm))

    d_ff = w_gate.shape[1]
    once = lambda shape: pl.BlockSpec(shape, lambda i: (0, 0), pipeline_mode=pl.Buffered(1))
    out = pl.pallas_call(
        _out_ffn_kernel,
        grid=(T // tm,),
        in_specs=[tok(D), tok(WIDTH), tok(WIDTH), tok(WIDTH), _full((1, WIDTH)), once((3 * WIDTH, D)),
                  _full((1, D)), once((D, d_ff)), once((D, d_ff)), once((d_ff, D))],
        out_specs=tok(D),
        out_shape=jax.ShapeDtypeStruct((T, D), x.dtype),
        compiler_params=_params(("arbitrary",)),
        name="out_ffn",
    )(x2, y_mla.reshape(T, WIDTH), y_hg.reshape(T, WIDTH), ymem, _row(mla_out_norm), w_out_b,
      _row(norm_ffn), w_gate_b, w_up_b, w_down_b)
    return out.reshape(B, S, D)


def kernel(x, mem, positions, norm_mix, norm_mem, w_in, q_a_norm, w_uq, kv_a_norm, w_ukv, mla_q_norm, mla_k_norm, hg_lb_logits, hg_out_norm, w_mem_kv, mem_q_norm, mem_k_norm, mla_out_norm, mem_out_norm, w_out, norm_ffn, w_gate, w_up, w_down):
    depth = w_in.shape[0]
    for l in range(depth):
        x = _layer(x, mem, positions, l, norm_mix[l], norm_mem[l], w_in[l], q_a_norm[l], w_uq[l],
                   kv_a_norm[l], w_ukv[l], mla_q_norm[l], mla_k_norm[l], hg_lb_logits, hg_out_norm[l],
                   w_mem_kv[l], mem_q_norm[l], mem_k_norm[l], mla_out_norm[l], mem_out_norm[l],
                   w_out[l], norm_ffn[l], w_gate[l], w_up[l], w_down[l])
    return x
```

```python
import functools

import jax
import jax.numpy as jnp
from jax import lax
from jax.experimental import pallas as pl
from jax.experimental.pallas import tpu as pltpu

F32 = jnp.float32
BF16 = jnp.bfloat16

EPS = 1e-6
N_HEADS = 4
D_NOPE = 128
D_ROPE = 64
D_QK = D_NOPE + D_ROPE
D_QK_PAD = 256
D_V = 128
Q_LORA = 384
KV_LORA = 256
ROPE_THETA = 10000.0
LOG2E = 1.4426950408889634
HG_D = 128
MEM_D = 128
WIDTH = N_HEADS * 128

LANE = 128
HG_CHUNK = 128
HG_SUB = 8
ONES_ROWS = 16

VMEM_LIMIT = 56 * 1024 * 1024


def _dot(a, b):
    return jnp.dot(a, b, preferred_element_type=F32)


def _dot_nt(a, b):
    return lax.dot_general(a, b, (((1,), (1,)), ((), ())), preferred_element_type=F32)


def _rms(x, g, width):
    ss = jnp.sum(x * x, axis=-1, keepdims=True)
    return x * lax.rsqrt(ss * (1.0 / width) + EPS) * g


def _sigmoid(x):
    return 1.0 / (1.0 + jnp.exp(-x))


def _mem_kv_kernel(mem_ref, g_ref, w_ref, kn_ref, k_out, v_out):
    m = mem_ref[0].astype(F32)
    mh = _rms(m, g_ref[...], m.shape[-1]).astype(BF16)
    kv = _dot(mh, w_ref[...])
    for h in range(N_HEADS):
        kh = kv[:, h * MEM_D:(h + 1) * MEM_D]
        k_out[0, :, h * MEM_D:(h + 1) * MEM_D] = _rms(kh, kn_ref[...], MEM_D).astype(BF16)
    v_out[0] = kv[:, WIDTH:].astype(BF16)


W_ALL = Q_LORA + 2 * D_ROPE + 5 * WIDTH + KV_LORA


def _w_in_kernel(wt_ref, uqt_ref, ukv_ref, wmem_ref, o_ref, uq_out, uk_out, uv_out, wmem_out):
    lo = Q_LORA + KV_LORA
    half = D_ROPE // 2

    @pl.when(pl.program_id(0) == 0)
    def _():
        for hd in range(N_HEADS):
            r = hd * D_QK
            uq_out[:, hd * D_QK_PAD:hd * D_QK_PAD + D_NOPE] = uqt_ref[r:r + D_NOPE, :].T.astype(BF16)
            x1 = uqt_ref[r + D_NOPE:r + D_NOPE + half, :]
            x2 = uqt_ref[r + D_NOPE + half:r + D_QK, :]
            uq_out[:, hd * D_QK_PAD + D_NOPE:(hd + 1) * D_QK_PAD] = (
                jnp.concatenate([x1, x2, -x2, x1], axis=0).T.astype(BF16))
            c = hd * (D_NOPE + D_V)
            uk_out[:, hd * D_NOPE:(hd + 1) * D_NOPE] = ukv_ref[:, c:c + D_NOPE].astype(BF16)
            uv_out[:, hd * D_V:(hd + 1) * D_V] = ukv_ref[:, c + D_NOPE:c + D_NOPE + D_V].astype(BF16)
        wmem_out[...] = wmem_ref[...].astype(BF16)

    def put(dst, rows):
        o_ref[:, dst:dst + rows.shape[0]] = rows.T.astype(BF16)

    for c in range(0, Q_LORA, LANE):
        put(c, wt_ref[c:c + LANE, :])
    put(Q_LORA, jnp.concatenate([wt_ref[lo:lo + D_ROPE, :], -wt_ref[lo + half:lo + D_ROPE, :],
                                 wt_ref[lo:lo + half, :]], axis=0))
    for c in range(0, 5 * WIDTH, LANE):
        put(Q_LORA + LANE + c, wt_ref[lo + D_ROPE + c:lo + D_ROPE + c + LANE, :])
    for c in range(0, KV_LORA, LANE):
        put(Q_LORA + LANE + 5 * WIDTH + c, wt_ref[Q_LORA + c:Q_LORA + c + LANE, :])


def _in_proj_kernel(x_ref, pos_ref, pos_next_ref, invf_ref, phase_ref, sign_ref, gmix_ref,
                    w_cqkr_ref, w_ckv_ref, w_hq_ref, w_hf_ref, w_hi_ref, w_hg_ref, w_mq_ref,
                    gqa_ref, w_uq_ref, gkva_ref, w_uk_ref, w_uv_ref,
                    gq_nope_ref, gq_r1_ref, gq_r2_ref, gk_nope_ref, gk_r1_ref, gk_r2_ref,
                    gmq_ref, kmem_ref, vmem_ref, gmo_ref,
                    q_out, k_out, vt_out, hq_out, hf_out, hi_out, hg_out, ymem_out,
                    cs_sc, tc_sc, ts_sc):
    tm = x_ref.shape[0]

    def fill_cs(p_ref):
        posi = p_ref[0]
        first = jnp.broadcast_to(posi[0:1, 0:1], posi.shape)
        step = (lax.broadcasted_iota(jnp.int32, posi.shape, 0) * LANE
                + lax.broadcasted_iota(jnp.int32, posi.shape, 1))
        gap = jnp.max(jnp.abs((posi - first - step).astype(F32)))

        @pl.when(gap == 0.0)
        def _():
            p0 = jnp.broadcast_to(posi[0:1, 0:1], (8, LANE)).astype(F32)
            a = jnp.cos(p0 * invf_ref[...] + phase_ref[...])
            b = pltpu.roll(a, D_ROPE, axis=1) * sign_ref[...]
            cs_sc[...] = a[0:1] * tc_sc[...] + b[0:1] * ts_sc[...]

        @pl.when(gap != 0.0)
        def _():
            posf = posi.astype(F32)
            pos = jnp.concatenate([jnp.broadcast_to(posf[r:r + 1, :], (LANE, LANE)).T
                                   for r in range(posf.shape[0])], axis=0)
            cs_sc[...] = jnp.cos(pos * invf_ref[...] + phase_ref[...])

    @pl.when(pl.program_id(0) == 0)
    def _():
        tf = lax.broadcasted_iota(jnp.int32, (tm, LANE), 0).astype(F32) * invf_ref[...]
        tc_sc[...] = jnp.cos(tf)
        ts_sc[...] = jnp.sin(tf)
        fill_cs(pos_ref)

    x = x_ref[...].astype(F32)
    h = _rms(x, gmix_ref[...], x.shape[-1]).astype(BF16)

    big = _dot(h, w_cqkr_ref[...])
    ckv = _dot(h, w_ckv_ref[...])
    hq = _dot(h, w_hq_ref[...])
    cqn = _rms(big[:, :Q_LORA], gqa_ref[...], Q_LORA).astype(BF16)
    ckvn = _rms(ckv, gkva_ref[...], KV_LORA).astype(BF16)
    qa = _dot(cqn, w_uq_ref[...])
    kn = _dot(ckvn, w_uk_ref[...])
    vt = _dot(ckvn, w_uv_ref[...]).T
    mq = _dot(h, w_mq_ref[...])
    hf_out[...] = _dot(h, w_hf_ref[...])

    cs = cs_sc[...]
    sc = pltpu.roll(cs, D_ROPE, axis=1)

    def rotary(tile, g1_ref, g2_ref):
        return tile * (g1_ref[...] * cs) + pltpu.roll(tile, D_ROPE, axis=1) * (g2_ref[...] * sc)

    q_scale = LOG2E * D_QK ** -0.5
    for hd in range(N_HEADS):
        nope = qa[:, hd * D_QK_PAD:hd * D_QK_PAD + D_NOPE]
        tile = qa[:, hd * D_QK_PAD + D_NOPE:(hd + 1) * D_QK_PAD]
        ss = jnp.sum(nope * nope + 0.5 * (tile * tile), axis=-1, keepdims=True)
        rinv = lax.rsqrt(ss * (1.0 / D_QK) + EPS) * q_scale
        q_out[:, hd * D_QK_PAD:hd * D_QK_PAD + D_NOPE] = (nope * gq_nope_ref[...] * rinv).astype(BF16)
        q_out[:, hd * D_QK_PAD + D_NOPE:(hd + 1) * D_QK_PAD] = (
            rotary(tile, gq_r1_ref, gq_r2_ref) * rinv).astype(BF16)

    vt_out[0, :, 0] = vt.reshape(N_HEADS, D_V, vt.shape[-1]).astype(BF16)
    ktile = big[:, Q_LORA:]
    sq_r = 0.5 * (ktile * ktile)
    rot = rotary(ktile, gk_r1_ref, gk_r2_ref)
    for hd in range(N_HEADS):
        a = kn[:, hd * D_NOPE:(hd + 1) * D_NOPE]
        ss = jnp.sum(a * a + sq_r, axis=-1, keepdims=True)
        rinv = lax.rsqrt(ss * (1.0 / D_QK) + EPS)
        k_out[:, hd * D_QK_PAD:hd * D_QK_PAD + D_NOPE] = (a * gk_nope_ref[...] * rinv).astype(BF16)
        k_out[:, hd * D_QK_PAD + D_NOPE:(hd + 1) * D_QK_PAD] = (rot * rinv).astype(BF16)

    ss_ = []
    for hd in range(N_HEADS):
        sl = slice(hd * MEM_D, (hd + 1) * MEM_D)
        qh = (_rms(mq[:, sl], gmq_ref[...], MEM_D) * (MEM_D ** -0.5)).astype(BF16)
        ss_.append(_dot_nt(qh, kmem_ref[0, :, sl]))
    hi_out[...] = _dot(h, w_hi_ref[...]).astype(BF16)
    hq_out[...] = (hq * _sigmoid(hq) * (HG_D ** -0.5)).astype(BF16)
    ys = []
    for hd in range(N_HEADS):
        sl = slice(hd * MEM_D, (hd + 1) * MEM_D)
        s = ss_[hd]
        p = jnp.exp(s - jnp.max(s, axis=-1, keepdims=True))
        l = jnp.sum(p, axis=-1, keepdims=True)
        ys.append(_dot(p.astype(BF16), vmem_ref[0, :, sl]) / l)
    hg = _dot(h, w_hg_ref[...])
    hg_out[...] = (hg * _sigmoid(hg)).astype(BF16)
    y = jnp.concatenate(ys, axis=-1)
    ymem_out[...] = _rms(y, gmo_ref[...], WIDTH).astype(BF16)

    fill_cs(pos_next_ref)


def _attn_kernel(q_ref, k_ref, vt_ref, w0_ref, w1_ref, w2_ref, w3_ref,
                 o_ref, w0_out, w1_out, w2_out, w3_out,
                 m_sc, acc_sc, s0_sc, s1_sc, cm0_sc, cm1_sc, *, bq, bk, heads):
    qi = pl.program_id(2)
    for w_ref, w_out in ((w0_ref, w0_out), (w1_ref, w1_out), (w2_ref, w2_out), (w3_ref, w3_out)):
        w_out[...] = w_ref[...].astype(BF16)
    s_bufs = (s0_sc, s1_sc)
    cm_bufs = (cm0_sc, cm1_sc)
    m_sc[...] = jnp.full(m_sc.shape, -jnp.inf, F32)
    acc_sc[...] = jnp.zeros(acc_sc.shape, F32)

    def scores(t, slot, hd, q0=0):
        r0 = t * bk if isinstance(t, int) else pl.multiple_of(t * bk, bk)
        q = q_ref[0, q0:, hd * D_QK_PAD:(hd + 1) * D_QK_PAD]
        k = k_ref[0, pl.ds(r0, bk), hd * D_QK_PAD:(hd + 1) * D_QK_PAD]
        s = _dot_nt(k, q)
        s_bufs[slot][hd, :, q0:] = s
        if q0 == 0:
            cm_bufs[slot][hd] = jnp.max(s, axis=0, keepdims=True)

    def accumulate(t, slot, hd, key_offset=None, q0=0):
        s = s_bufs[slot][hd, :, q0:]
        if key_offset is not None:
            kv = lax.broadcasted_iota(jnp.int32, s.shape, 0) + (key_offset - q0)
            qq = lax.broadcasted_iota(jnp.int32, s.shape, 1)
            s = jnp.where(kv <= qq, s, -jnp.inf)
            cm = jnp.max(s, axis=0, keepdims=True)
        else:
            cm = cm_bufs[slot][hd]
        m_prev = m_sc[hd, :, q0:]
        m_new = jnp.maximum(m_prev, cm)
        p = jnp.exp2(s - m_new)
        alpha = jnp.exp2(m_prev - m_new)
        vt1 = jnp.concatenate([vt_ref[0, hd, t], jnp.ones((ONES_ROWS, bk), BF16)], axis=0)
        acc_sc[hd, :, q0:] = alpha * acc_sc[hd, :, q0:] + _dot(vt1, p.astype(BF16))
        m_sc[hd, :, q0:] = m_new

    for hd in range(heads):
        scores(0, 0, hd)

    def advance(t, slot):
        for hd in range(heads):
            scores(t + 1, 1 - slot, hd)
            accumulate(t, slot, hd)

    def pair(jj):
        advance(2 * jj, 0)
        advance(2 * jj + 1, 1)

    def body(j4, carry):
        pair(2 * j4)
        pair(2 * j4 + 1)
        return carry

    lax.fori_loop(0, jnp.right_shift(qi, 1), body, 0)

    @pl.when((qi & 1) == 1)
    def _():
        pair(qi - 1)
    for hd in range(heads):
        scores(2 * qi + 1, 1, hd, q0=bk)
        accumulate(2 * qi, 0, hd, key_offset=0)
    for hd in range(heads):
        accumulate(2 * qi + 1, 1, hd, key_offset=bk, q0=bk)

    for hd in range(heads):
        o = (acc_sc[hd, :D_V] / acc_sc[hd, D_V:D_V + 1]).T
        o_ref[0, :, hd * D_V:(hd + 1) * D_V] = o.astype(o_ref.dtype)


def _pair_reference(b, m):
    c = b.shape[0]
    n2 = c // (2 * m)
    br = b.reshape(n2, 2 * m, b.shape[1])
    last = br[:, m - 1:m, :]
    return jnp.broadcast_to(last, br.shape).reshape(b.shape)


def _hgrn_kernel(hq_ref, hf_ref, hi_ref, hg_ref, lbl_ref, gain_ref, o_ref, st_ref, b_sc, *, n_chunks, layer):
    C = HG_CHUNK

    @pl.when(pl.program_id(1) == 0)
    def _():
        st_ref[...] = jnp.zeros(st_ref.shape, F32)

    lg = lbl_ref[...].astype(F32)
    e = jnp.exp(lg - jnp.max(lg, axis=0, keepdims=True))
    lb = jnp.sum(e[:layer + 1], axis=0, keepdims=True) / jnp.sum(e, axis=0, keepdims=True)

    row = lax.broadcasted_iota(jnp.int32, (C, C), 0)
    col = lax.broadcasted_iota(jnp.int32, (C, C), 1)
    tri = (col <= row).astype(BF16)
    diag_mask = (((row ^ col) & ~(HG_SUB - 1)) | jnp.where(col <= row, 0, 1)) == 0
    sub_keep = [jnp.where((row & (HG_SUB - 1)) == s_off, 1.0, 0.0).astype(BF16) for s_off in range(HG_SUB)]
    levels = []
    m = HG_SUB
    while m < C:
        bad = ((row ^ col) & ~(2 * m - 1)) | ((row & m) ^ m) | (col & m)
        levels.append((m, bad == 0))
        m *= 2

    def chunk(c):
        r0 = c * C
        fr = hf_ref[0, pl.ds(r0, C), :]
        f = lb + (1.0 - lb) * _sigmoid(fr)
        logf = jnp.log(f) * LOG2E
        kk_all = 1.0 - f
        t0 = logf.astype(BF16)
        r1 = logf - t0.astype(F32)
        t1 = r1.astype(BF16)
        t2 = (r1 - t1.astype(F32)).astype(BF16)
        b_all = _dot(tri, t0) + _dot(tri, t1) + _dot(tri, t2)
        q_all = hq_ref[0, pl.ds(r0, C), :].astype(F32)
        v_all = hi_ref[0, pl.ds(r0, C), :]
        g_all = hg_ref[0, pl.ds(r0, C), :].astype(F32)
        b_sc[c] = b_all
        zero = jnp.zeros((C, HG_D), BF16)

        def pair_nt(x0, x1):
            return jnp.concatenate([jnp.concatenate([x0, zero], axis=1),
                                    jnp.concatenate([zero, x1], axis=1)], axis=0)

        for h0 in range(0, N_HEADS, 2):
            heads = (h0, h0 + 1)
            sls = [slice(hd * HG_D, (hd + 1) * HG_D) for hd in heads]
            bs_ = [b_all[:, sl] for sl in sls]
            qs = [q_all[:, sl] for sl in sls]
            kks = [kk_all[:, sl] for sl in sls]
            vs = [v_all[:, sl] for sl in sls]

            a_ = []
            for b, q, kk, sl in zip(bs_, qs, kks, sls):
                kb = kk.astype(BF16)
                ms = []
                ks = []
                for s_off in range(HG_SUB):
                    bs = jnp.concatenate(
                        [jnp.broadcast_to(b_sc[c, i * HG_SUB + s_off:i * HG_SUB + s_off + 1, sl],
                                          (HG_SUB, HG_D)) for i in range(C // HG_SUB)], axis=0)
                    ms.append((q * jnp.exp2(jnp.minimum(b - bs, 0.0))).astype(BF16))
                    ks.append(kb * sub_keep[s_off])
                ad = _dot_nt(jnp.concatenate(ms, axis=1), jnp.concatenate(ks, axis=1))
                a_.append(jnp.where(diag_mask, ad, 0.0))
            for m_blk, mask in levels:
                qes = []
                kes = []
                for b, q, kk in zip(bs_, qs, kks):
                    d = b - _pair_reference(b, m_blk)
                    qes.append((q * jnp.exp2(d)).astype(BF16))
                    kes.append((kk * jnp.exp2(-d)).astype(BF16))
                al = _dot_nt(jnp.concatenate(qes, axis=1), pair_nt(*kes))
                a_ = [jnp.where(mask, al[:, j * C:(j + 1) * C], a_[j]) for j in range(2)]

            sts = [st_ref[hd] for hd in heads]
            qd = jnp.concatenate([(q * jnp.exp2(b)).astype(BF16) for b, q in zip(bs_, qs)], axis=1)
            a2 = jnp.concatenate([a.astype(BF16) for a in a_], axis=1)
            v2 = jnp.concatenate([jnp.concatenate([vs[0], zero], axis=1),
                                  jnp.concatenate([zero, vs[1]], axis=1)], axis=0)
            o2 = _dot(a2, v2) + _dot_nt(qd, pair_nt(*[st.astype(BF16) for st in sts]))

            for j, hd in enumerate(heads):
                b, kk, v, sl = bs_[j], kks[j], vs[j], sls[j]
                b_last = b[C - 1:C, :]
                kd = (kk * jnp.exp2(b_last - b)).astype(BF16)
                vt = v.astype(F32).T.astype(BF16)
                st_ref[hd] = sts[j] * jnp.exp2(b_last) + _dot(vt, kd)

                on = _rms(o2[:, j * HG_D:(j + 1) * HG_D], gain_ref[:, sl], HG_D)
                o_ref[0, pl.ds(r0, C), sl] = (on * g_all[:, sl]).astype(o_ref.dtype)

    for c in range(n_chunks):
        chunk(c)


def _out_ffn_kernel(x_ref, ymla_ref, yhg_ref, ymem_ref, gmla_ref, w_out_ref, gffn_ref,
                    w_gate_ref, w_up_ref, w_down_ref, o_ref):
    n = x_ref.shape[0] // 2
    halves = [pl.ds(0, n), pl.ds(n, n)]
    x1s = []
    for rows in halves:
        ymla = _rms(ymla_ref[rows, :].astype(F32), gmla_ref[...], WIDTH).astype(BF16)
        mix = (_dot(ymla, w_out_ref[0:WIDTH, :])
               + _dot(yhg_ref[rows, :], w_out_ref[WIDTH:2 * WIDTH, :])
               + _dot(ymem_ref[rows, :], w_out_ref[2 * WIDTH:3 * WIDTH, :]))
        x1s.append(x_ref[rows, :].astype(F32) + mix)
    acts = []
    for x1 in x1s:
        h2 = _rms(x1, gffn_ref[...], x1.shape[-1]).astype(BF16)
        g = _dot(h2, w_gate_ref[...])
        u = _dot(h2, w_up_ref[...])
        acts.append((g * _sigmoid(g) * u).astype(BF16))
    for rows, x1, act in zip(halves, x1s, acts):
        o_ref[rows, :] = (x1 + _dot(act, w_down_ref[...])).astype(o_ref.dtype)


def _full(shape):
    nd = len(shape)
    return pl.BlockSpec(shape, lambda *_: (0,) * nd)


def _params(sem):
    return pltpu.CompilerParams(dimension_semantics=sem, vmem_limit_bytes=VMEM_LIMIT)


def _row(v):
    return v.reshape(1, -1).astype(F32)


def _layer(x, mem, positions, layer, norm_mix, norm_mem, w_in, q_a_norm, w_uq, kv_a_norm, w_ukv,
           mla_q_norm, mla_k_norm, hg_lb_logits, hg_out_norm, w_mem_kv, mem_q_norm, mem_k_norm,
           mla_out_norm, mem_out_norm, w_out, norm_ffn, w_gate, w_up, w_down):
    B, S, D = x.shape
    M = mem.shape[1]
    T = B * S
    half = D_ROPE // 2
    H = N_HEADS

    assert w_in.shape == (D, W_ALL - D_ROPE)
    wrows = 256
    assert w_uq.shape == (Q_LORA, H * D_QK) and w_ukv.shape == (KV_LORA, H * (D_NOPE + D_V))
    small = [(Q_LORA, H * D_QK_PAD), (KV_LORA, H * D_NOPE), (KV_LORA, H * D_V), w_mem_kv.shape]
    w_all, uq2, w_uk, w_uv, w_mem_b = pl.pallas_call(
        _w_in_kernel,
        grid=(D // wrows,),
        in_specs=[pl.BlockSpec((w_in.shape[1], wrows), lambda i: (0, i)),
                  _full((H * D_QK, Q_LORA)), _full(w_ukv.shape), _full(w_mem_kv.shape)],
        out_specs=[pl.BlockSpec((wrows, W_ALL), lambda i: (i, 0))] + [_full(s) for s in small],
        out_shape=[jax.ShapeDtypeStruct((D, W_ALL), BF16)] + [jax.ShapeDtypeStruct(s, BF16) for s in small],
        compiler_params=_params(("arbitrary",)),
        name="w_in_layout",
    )(w_in.T, w_uq.T, w_ukv, w_mem_kv)

    pad = jnp.zeros((LANE - D_ROPE,), F32)

    def rotary_gains(g):
        g = g.astype(F32)
        g1 = jnp.concatenate([g[D_NOPE:], pad]).reshape(1, LANE)
        g2 = jnp.concatenate([g[D_NOPE + half:], g[D_NOPE:D_NOPE + half], pad]).reshape(1, LANE)
        return g[:D_NOPE].reshape(1, D_NOPE), g1, g2

    gq_nope, gq_r1, gq_r2 = rotary_gains(mla_q_norm)
    gk_nope, gk_r1, gk_r2 = rotary_gains(mla_k_norm)
    inv_freq = jnp.power(ROPE_THETA, -jnp.arange(half, dtype=F32) / half)
    invf = jnp.tile(inv_freq, LANE // half).reshape(1, LANE)
    phase = jnp.concatenate([jnp.zeros((D_ROPE,), F32), jnp.full((D_ROPE,), -jnp.pi / 2, F32)]).reshape(1, LANE)

    kmem, vmem = pl.pallas_call(
        _mem_kv_kernel,
        grid=(B,),
        in_specs=[pl.BlockSpec((1, M, D), lambda b: (b, 0, 0)),
                  _full((1, D)), _full((D, 2 * WIDTH)), _full((1, MEM_D))],
        out_specs=[pl.BlockSpec((1, M, WIDTH), lambda b: (b, 0, 0))] * 2,
        out_shape=[jax.ShapeDtypeStruct((B, M, WIDTH), BF16)] * 2,
        compiler_params=_params(("arbitrary",)),
        name="mem_kv",
    )(mem, _row(norm_mem), w_mem_b, _row(mem_k_norm))

    tm = min(512, S)
    assert S % tm == 0
    steps_per_batch = S // tm
    x2 = x.reshape(T, D)
    assert tm % LANE == 0
    pos2 = positions.reshape(T // tm, tm // LANE, LANE).astype(jnp.int32)
    pos_spec = pl.BlockSpec((1, tm // LANE, LANE), lambda i: (i, 0, 0))
    n_tok_steps = T // tm
    pos_next_spec = pl.BlockSpec((1, tm // LANE, LANE), lambda i: (jnp.minimum(i + 1, n_tok_steps - 1), 0, 0))
    sign = jnp.concatenate([jnp.full((D_ROPE,), -1.0, F32), jnp.ones((D_ROPE,), F32)]).reshape(1, LANE)
    tok = lambda w: pl.BlockSpec((tm, w), lambda i: (i, 0))
    col = lambda width, start: pl.BlockSpec((D, width), lambda i: (0, start // width))
    weight_specs = [col(WIDTH, 0), col(KV_LORA, 6 * WIDTH), col(WIDTH, WIDTH), col(WIDTH, 2 * WIDTH),
                    col(WIDTH, 3 * WIDTH), col(WIDTH, 4 * WIDTH), col(WIDTH, 5 * WIDTH)]
    weights = [w_all] * len(weight_specs)
    rest = [_row(q_a_norm), uq2, _row(kv_a_norm), w_uk, w_uv,
            gq_nope, gq_r1, gq_r2, gk_nope, gk_r1, gk_r2, _row(mem_q_norm)]
    mem_spec = pl.BlockSpec((1, M, WIDTH), lambda i: (i // steps_per_batch, 0, 0))
    vt_spec = pl.BlockSpec((1, H, 1, D_V, tm),
                           lambda i: (i // steps_per_batch, 0, i % steps_per_batch, 0, 0))
    q_all, k_all, vt_all, hq, hf, hi, hg, ymem = pl.pallas_call(
        _in_proj_kernel,
        grid=(T // tm,),
        in_specs=([tok(D), pos_spec, pos_next_spec, _full((1, LANE)), _full((1, LANE)), _full((1, LANE)),
                   _full((1, D))]
                  + weight_specs + [_full(r.shape) for r in rest]
                  + [mem_spec, mem_spec, _full((1, WIDTH))]),
        out_specs=[tok(H * D_QK_PAD), tok(H * D_QK_PAD), vt_spec] + [tok(WIDTH)] * 5,
        out_shape=[jax.ShapeDtypeStruct((T, H * D_QK_PAD), BF16)] * 2
        + [jax.ShapeDtypeStruct((B, H, steps_per_batch, D_V, tm), BF16)]
        + [jax.ShapeDtypeStruct((T, WIDTH), dt) for dt in (BF16, F32, BF16, BF16, BF16)],
        scratch_shapes=[pltpu.VMEM((tm, LANE), F32)] * 3,
        compiler_params=_params(("arbitrary",)),
        name="in_proj",
    )(x2, pos2, pos2, invf, phase, sign, _row(norm_mix), *weights, *rest, kmem, vmem, _row(mem_out_norm))

    bk = tm
    bq = 2 * bk
    assert S % bq == 0
    hpb = 2
    att_grid = (B, H // hpb, S // bq)
    n_att = att_grid[0] * att_grid[1] * att_grid[2]

    def cast_spec(w):
        rows = w.shape[0]
        rb = next(r for r in range(16 * -(-rows // (16 * n_att)), rows + 1, 16) if rows % r == 0)
        last = rows // rb - 1
        return pl.BlockSpec((rb, w.shape[1]),
                            lambda b, h, i: (jnp.minimum((b * att_grid[1] + h) * att_grid[2] + i, last), 0))

    ffn_weights = [w_out, w_gate, w_up, w_down]
    cast_specs = [cast_spec(w) for w in ffn_weights]
    y_mla, w_out_b, w_gate_b, w_up_b, w_down_b = pl.pallas_call(
        functools.partial(_attn_kernel, bq=bq, bk=bk, heads=hpb),
        grid=att_grid,
        in_specs=[pl.BlockSpec((1, bq, hpb * D_QK_PAD), lambda b, h, i: (b, i, h)),
                  pl.BlockSpec((1, S, hpb * D_QK_PAD), lambda b, h, i: (b, 0, h)),
                  pl.BlockSpec((1, hpb, S // bk, D_V, bk), lambda b, h, i: (b, h, 0, 0, 0))] + cast_specs,
        out_specs=[pl.BlockSpec((1, bq, hpb * D_V), lambda b, h, i: (b, i, h))] + cast_specs,
        out_shape=[jax.ShapeDtypeStruct((B, S, H * D_V), BF16)]
        + [jax.ShapeDtypeStruct(w.shape, BF16) for w in ffn_weights],
        scratch_shapes=[pltpu.VMEM((hpb, 1, bq), F32),
                        pltpu.VMEM((hpb, D_V + ONES_ROWS, bq), F32),
                        pltpu.VMEM((hpb, bk, bq), F32), pltpu.VMEM((hpb, bk, bq), F32),
                        pltpu.VMEM((hpb, 1, bq), F32), pltpu.VMEM((hpb, 1, bq), F32)],
        compiler_params=_params(("arbitrary", "arbitrary", "arbitrary")),
        name="mla_attn",
    )(q_all.reshape(B, S, H * D_QK_PAD), k_all.reshape(B, S, H * D_QK_PAD), vt_all, *ffn_weights)

    ts = min(1024, S)
    assert S % ts == 0 and ts % HG_CHUNK == 0
    seq = lambda: pl.BlockSpec((1, ts, WIDTH), lambda b, i: (b, i, 0))
    n_layers = hg_lb_logits.shape[0]
    y_hg = pl.pallas_call(
        functools.partial(_hgrn_kernel, n_chunks=ts // HG_CHUNK, layer=layer),
        grid=(B, S // ts),
        in_specs=[seq(), seq(), seq(), seq(), _full((n_layers, WIDTH)), _full((1, WIDTH))],
        out_specs=seq(),
        out_shape=jax.ShapeDtypeStruct((B, S, WIDTH), BF16),
        scratch_shapes=[pltpu.VMEM((N_HEADS, HG_D, HG_D), F32),
                        pltpu.VMEM((ts // HG_CHUNK, HG_CHUNK, WIDTH), F32)],
        compiler_params=_params(("arbitrary", "arbitrary")),
        name="hgrn",
    )(hq.reshape(B, S, WIDTH), hf.reshape(B, S, WIDTH), hi.reshape(B, S, WIDTH), hg.reshape(B, S, WIDTH),
      hg_lb_logits.astype(F32), _row(hg_out_norm))

    d_ff = w_gate.shape[1]
    once = lambda shape: pl.BlockSpec(shape, lambda i: (0, 0), pipeline_mode=pl.Buffered(1))
    out = pl.pallas_call(
        _out_ffn_kernel,
        grid=(T // tm,),
        in_specs=[tok(D), tok(WIDTH), tok(WIDTH), tok(WIDTH), _full((1, WIDTH)), once((3 * WIDTH, D)),
                  _full((1, D)), once((D, d_ff)), once((D, d_ff)), once((d_ff, D))],
        out_specs=tok(D),
        out_shape=jax.ShapeDtypeStruct((T, D), x.dtype),
        compiler_params=_params(("arbitrary",)),
        name="out_ffn",
    )(x2, y_mla.reshape(T, WIDTH), y_hg.reshape(T, WIDTH), ymem, _row(mla_out_norm), w_out_b,
      _row(norm_ffn), w_gate_b, w_up_b, w_down_b)
    return out.reshape(B, S, D)


def kernel(x, mem, positions, norm_mix, norm_mem, w_in, q_a_norm, w_uq, kv_a_norm, w_ukv, mla_q_norm, mla_k_norm, hg_lb_logits, hg_out_norm, w_mem_kv, mem_q_norm, mem_k_norm, mla_out_norm, mem_out_norm, w_out, norm_ffn, w_gate, w_up, w_down):
    depth = w_in.shape[0]
    for l in range(depth):
        x = _layer(x, mem, positions, l, norm_mix[l], norm_mem[l], w_in[l], q_a_norm[l], w_uq[l],
                   kv_a_norm[l], w_ukv[l], mla_q_norm[l], mla_k_norm[l], hg_lb_logits, hg_out_norm[l],
                   w_mem_kv[l], mem_q_norm[l], mem_k_norm[l], mla_out_norm[l], mem_out_norm[l],
                   w_out[l], norm_ffn[l], w_gate[l], w_up[l], w_down[l])
    return x
```

```python
import functools

import jax
import jax.numpy as jnp
from jax import lax
from jax.experimental import pallas as pl
from jax.experimental.pallas import tpu as pltpu

F32 = jnp.float32
BF16 = jnp.bfloat16

EPS = 1e-6
N_HEADS = 4
D_NOPE = 128
D_ROPE = 64
D_QK = D_NOPE + D_ROPE
D_QK_PAD = 256
D_V = 128
Q_LORA = 384
KV_LORA = 256
ROPE_THETA = 10000.0
LOG2E = 1.4426950408889634
HG_D = 128
MEM_D = 128
WIDTH = N_HEADS * 128

LANE = 128
HG_CHUNK = 128
HG_SUB = 8
ONES_ROWS = 16

VMEM_LIMIT = 56 * 1024 * 1024


def _dot(a, b):
    return jnp.dot(a, b, preferred_element_type=F32)


def _dot_nt(a, b):
    return lax.dot_general(a, b, (((1,), (1,)), ((), ())), preferred_element_type=F32)


def _rms(x, g, width):
    ss = jnp.sum(x * x, axis=-1, keepdims=True)
    return x * lax.rsqrt(ss * (1.0 / width) + EPS) * g


def _sigmoid(x):
    return 1.0 / (1.0 + jnp.exp(-x))


def _mem_kv_kernel(mem_ref, g_ref, w_ref, kn_ref, k_out, v_out):
    m = mem_ref[0].astype(F32)
    mh = _rms(m, g_ref[...], m.shape[-1]).astype(BF16)
    kv = _dot(mh, w_ref[...])
    for h in range(N_HEADS):
        kh = kv[:, h * MEM_D:(h + 1) * MEM_D]
        k_out[0, :, h * MEM_D:(h + 1) * MEM_D] = _rms(kh, kn_ref[...], MEM_D).astype(BF16)
    v_out[0] = kv[:, WIDTH:].astype(BF16)


W_ALL = Q_LORA + 2 * D_ROPE + 5 * WIDTH + KV_LORA


def _w_in_kernel(wt_ref, uqt_ref, ukv_ref, wmem_ref, o_ref, uq_out, uk_out, uv_out, wmem_out):
    lo = Q_LORA + KV_LORA
    half = D_ROPE // 2

    @pl.when(pl.program_id(0) == 0)
    def _():
        for hd in range(N_HEADS):
            r = hd * D_QK
            uq_out[:, hd * D_QK_PAD:hd * D_QK_PAD + D_NOPE] = uqt_ref[r:r + D_NOPE, :].T.astype(BF16)
            x1 = uqt_ref[r + D_NOPE:r + D_NOPE + half, :]
            x2 = uqt_ref[r + D_NOPE + half:r + D_QK, :]
            uq_out[:, hd * D_QK_PAD + D_NOPE:(hd + 1) * D_QK_PAD] = (
                jnp.concatenate([x1, x2, -x2, x1], axis=0).T.astype(BF16))
            c = hd * (D_NOPE + D_V)
            uk_out[:, hd * D_NOPE:(hd + 1) * D_NOPE] = ukv_ref[:, c:c + D_NOPE].astype(BF16)
            uv_out[:, hd * D_V:(hd + 1) * D_V] = ukv_ref[:, c + D_NOPE:c + D_NOPE + D_V].astype(BF16)
        wmem_out[...] = wmem_ref[...].astype(BF16)

    def put(dst, rows):
        o_ref[:, dst:dst + rows.shape[0]] = rows.T.astype(BF16)

    for c in range(0, Q_LORA, LANE):
        put(c, wt_ref[c:c + LANE, :])
    put(Q_LORA, jnp.concatenate([wt_ref[lo:lo + D_ROPE, :], -wt_ref[lo + half:lo + D_ROPE, :],
                                 wt_ref[lo:lo + half, :]], axis=0))
    for c in range(0, 5 * WIDTH, LANE):
        put(Q_LORA + LANE + c, wt_ref[lo + D_ROPE + c:lo + D_ROPE + c + LANE, :])
    for c in range(0, KV_LORA, LANE):
        put(Q_LORA + LANE + 5 * WIDTH + c, wt_ref[Q_LORA + c:Q_LORA + c + LANE, :])


def _in_proj_kernel(x_ref, pos_ref, pos_next_ref, invf_ref, phase_ref, sign_ref, gmix_ref,
                    w_cqkr_ref, w_ckv_ref, w_hq_ref, w_hf_ref, w_hi_ref, w_hg_ref, w_mq_ref,
                    gqa_ref, w_uq_ref, gkva_ref, w_uk_ref, w_uv_ref,
                    gq_nope_ref, gq_r1_ref, gq_r2_ref, gk_nope_ref, gk_r1_ref, gk_r2_ref,
                    gmq_ref, kmem_ref, vmem_ref, gmo_ref,
                    q_out, k_out, vt_out, hq_out, hf_out, hi_out, hg_out, ymem_out,
                    cs_sc, tc_sc, ts_sc):
    tm = x_ref.shape[0]

    def fill_cs(p_ref):
        posi = p_ref[0]
        first = jnp.broadcast_to(posi[0:1, 0:1], posi.shape)
        step = (lax.broadcasted_iota(jnp.int32, posi.shape, 0) * LANE
                + lax.broadcasted_iota(jnp.int32, posi.shape, 1))
        gap = jnp.max(jnp.abs((posi - first - step).astype(F32)))

        @pl.when(gap == 0.0)
        def _():
            p0 = jnp.broadcast_to(posi[0:1, 0:1], (8, LANE)).astype(F32)
            a = jnp.cos(p0 * invf_ref[...] + phase_ref[...])
            b = pltpu.roll(a, D_ROPE, axis=1) * sign_ref[...]
            cs_sc[...] = a[0:1] * tc_sc[...] + b[0:1] * ts_sc[...]

        @pl.when(gap != 0.0)
        def _():
            posf = posi.astype(F32)
            pos = jnp.concatenate([jnp.broadcast_to(posf[r:r + 1, :], (LANE, LANE)).T
                                   for r in range(posf.shape[0])], axis=0)
            cs_sc[...] = jnp.cos(pos * invf_ref[...] + phase_ref[...])

    @pl.when(pl.program_id(0) == 0)
    def _():
        tf = lax.broadcasted_iota(jnp.int32, (tm, LANE), 0).astype(F32) * invf_ref[...]
        tc_sc[...] = jnp.cos(tf)
        ts_sc[...] = jnp.sin(tf)
        fill_cs(pos_ref)

    def half(rows, cols):
        x = x_ref[rows, :].astype(F32)
        h = _rms(x, gmix_ref[...], x.shape[-1]).astype(BF16)

        big = _dot(h, w_cqkr_ref[...])
        ckv = _dot(h, w_ckv_ref[...])
        hq = _dot(h, w_hq_ref[...])
        cqn = _rms(big[:, :Q_LORA], gqa_ref[...], Q_LORA).astype(BF16)
        ckvn = _rms(ckv, gkva_ref[...], KV_LORA).astype(BF16)
        qa = _dot(cqn, w_uq_ref[...])
        kn = _dot(ckvn, w_uk_ref[...])
        vt = _dot(ckvn, w_uv_ref[...]).T
        mq = _dot(h, w_mq_ref[...])
        hf_out[rows, :] = _dot(h, w_hf_ref[...])

        cs = cs_sc[rows, :]
        sc = pltpu.roll(cs, D_ROPE, axis=1)

        def rotary(tile, g1_ref, g2_ref):
            return tile * (g1_ref[...] * cs) + pltpu.roll(tile, D_ROPE, axis=1) * (g2_ref[...] * sc)

        q_scale = LOG2E * D_QK ** -0.5
        for hd in range(N_HEADS):
            nope = qa[:, hd * D_QK_PAD:hd * D_QK_PAD + D_NOPE]
            tile = qa[:, hd * D_QK_PAD + D_NOPE:(hd + 1) * D_QK_PAD]
            ss = (jnp.sum(nope * nope, axis=-1, keepdims=True)
                  + 0.5 * jnp.sum(tile * tile, axis=-1, keepdims=True))
            rinv = lax.rsqrt(ss * (1.0 / D_QK) + EPS) * q_scale
            q_out[rows, hd * D_QK_PAD:hd * D_QK_PAD + D_NOPE] = (nope * gq_nope_ref[...] * rinv).astype(BF16)
            q_out[rows, hd * D_QK_PAD + D_NOPE:(hd + 1) * D_QK_PAD] = (
                rotary(tile, gq_r1_ref, gq_r2_ref) * rinv).astype(BF16)

        vt_out[0, :, 0, :, cols] = vt.reshape(N_HEADS, D_V, vt.shape[-1]).astype(BF16)
        ktile = big[:, Q_LORA:]
        ss_r = 0.5 * jnp.sum(ktile * ktile, axis=-1, keepdims=True)
        rot = rotary(ktile, gk_r1_ref, gk_r2_ref)
        for hd in range(N_HEADS):
            a = kn[:, hd * D_NOPE:(hd + 1) * D_NOPE]
            ss = jnp.sum(a * a, axis=-1, keepdims=True) + ss_r
            rinv = lax.rsqrt(ss * (1.0 / D_QK) + EPS)
            k_out[rows, hd * D_QK_PAD:hd * D_QK_PAD + D_NOPE] = (a * gk_nope_ref[...] * rinv).astype(BF16)
            k_out[rows, hd * D_QK_PAD + D_NOPE:(hd + 1) * D_QK_PAD] = (rot * rinv).astype(BF16)

        ss_ = []
        for hd in range(N_HEADS):
            sl = slice(hd * MEM_D, (hd + 1) * MEM_D)
            qh = (_rms(mq[:, sl], gmq_ref[...], MEM_D) * (MEM_D ** -0.5)).astype(BF16)
            ss_.append(_dot_nt(qh, kmem_ref[0, :, sl]))
        hi_out[rows, :] = _dot(h, w_hi_ref[...]).astype(BF16)
        hq_out[rows, :] = (hq * _sigmoid(hq) * (HG_D ** -0.5)).astype(BF16)
        ys = []
        for hd in range(N_HEADS):
            sl = slice(hd * MEM_D, (hd + 1) * MEM_D)
            s = ss_[hd]
            p = jnp.exp(s - jnp.max(s, axis=-1, keepdims=True))
            l = jnp.sum(p, axis=-1, keepdims=True)
            ys.append(_dot(p.astype(BF16), vmem_ref[0, :, sl]) / l)
        hg = _dot(h, w_hg_ref[...])
        hg_out[rows, :] = (hg * _sigmoid(hg)).astype(BF16)
        y = jnp.concatenate(ys, axis=-1)
        ymem_out[rows, :] = _rms(y, gmo_ref[...], WIDTH).astype(BF16)

    n = tm // 2
    half(pl.ds(0, n), slice(0, n))
    half(pl.ds(n, n), slice(n, tm))

    fill_cs(pos_next_ref)


def _attn_kernel(q_ref, k_ref, vt_ref, w0_ref, w1_ref, w2_ref, w3_ref,
                 o_ref, w0_out, w1_out, w2_out, w3_out,
                 m_sc, acc_sc, s0_sc, s1_sc, cm0_sc, cm1_sc, *, bq, bk, heads):
    qi = pl.program_id(2)
    for w_ref, w_out in ((w0_ref, w0_out), (w1_ref, w1_out), (w2_ref, w2_out), (w3_ref, w3_out)):
        w_out[...] = w_ref[...].astype(BF16)
    s_bufs = (s0_sc, s1_sc)
    cm_bufs = (cm0_sc, cm1_sc)
    m_sc[...] = jnp.full(m_sc.shape, -jnp.inf, F32)
    acc_sc[...] = jnp.zeros(acc_sc.shape, F32)

    def scores(t, slot, hd, q0=0):
        r0 = t * bk if isinstance(t, int) else pl.multiple_of(t * bk, bk)
        q = q_ref[0, q0:, hd * D_QK_PAD:(hd + 1) * D_QK_PAD]
        k = k_ref[0, pl.ds(r0, bk), hd * D_QK_PAD:(hd + 1) * D_QK_PAD]
        s = _dot_nt(k, q)
        s_bufs[slot][hd, :, q0:] = s
        if q0 == 0:
            cm_bufs[slot][hd] = jnp.max(s, axis=0, keepdims=True)

    def accumulate(t, slot, hd, key_offset=None, q0=0):
        s = s_bufs[slot][hd, :, q0:]
        if key_offset is not None:
            kv = lax.broadcasted_iota(jnp.int32, s.shape, 0) + (key_offset - q0)
            qq = lax.broadcasted_iota(jnp.int32, s.shape, 1)
            s = jnp.where(kv <= qq, s, -jnp.inf)
            cm = jnp.max(s, axis=0, keepdims=True)
        else:
            cm = cm_bufs[slot][hd]
        m_prev = m_sc[hd, :, q0:]
        m_new = jnp.maximum(m_prev, cm)
        p = jnp.exp2(s - m_new)
        alpha = jnp.exp2(m_prev - m_new)
        vt1 = jnp.concatenate([vt_ref[0, hd, t], jnp.ones((ONES_ROWS, bk), BF16)], axis=0)
        acc_sc[hd, :, q0:] = alpha * acc_sc[hd, :, q0:] + _dot(vt1, p.astype(BF16))
        m_sc[hd, :, q0:] = m_new

    for hd in range(heads):
        scores(0, 0, hd)

    def advance(t, slot):
        for hd in range(heads):
            scores(t + 1, 1 - slot, hd)
            accumulate(t, slot, hd)

    def pair(jj):
        advance(2 * jj, 0)
        advance(2 * jj + 1, 1)

    def body(j4, carry):
        pair(2 * j4)
        pair(2 * j4 + 1)
        return carry

    lax.fori_loop(0, jnp.right_shift(qi, 1), body, 0)

    @pl.when((qi & 1) == 1)
    def _():
        pair(qi - 1)
    for hd in range(heads):
        scores(2 * qi + 1, 1, hd, q0=bk)
        accumulate(2 * qi, 0, hd, key_offset=0)
    for hd in range(heads):
        accumulate(2 * qi + 1, 1, hd, key_offset=bk, q0=bk)

    for hd in range(heads):
        o = (acc_sc[hd, :D_V] / acc_sc[hd, D_V:D_V + 1]).T
        o_ref[0, :, hd * D_V:(hd + 1) * D_V] = o.astype(o_ref.dtype)


def _pair_reference(b, m):
    c = b.shape[0]
    n2 = c // (2 * m)
    br = b.reshape(n2, 2 * m, b.shape[1])
    last = br[:, m - 1:m, :]
    return jnp.broadcast_to(last, br.shape).reshape(b.shape)


def _hgrn_kernel(hq_ref, hf_ref, hi_ref, hg_ref, lbl_ref, gain_ref, o_ref, st_ref, b_sc, *, n_chunks, layer):
    C = HG_CHUNK

    @pl.when(pl.program_id(1) == 0)
    def _():
        st_ref[...] = jnp.zeros(st_ref.shape, F32)

    lg = lbl_ref[...].astype(F32)
    e = jnp.exp(lg - jnp.max(lg, axis=0, keepdims=True))
    lb = jnp.sum(e[:layer + 1], axis=0, keepdims=True) / jnp.sum(e, axis=0, keepdims=True)

    row = lax.broadcasted_iota(jnp.int32, (C, C), 0)
    col = lax.broadcasted_iota(jnp.int32, (C, C), 1)
    tri = (col <= row).astype(BF16)
    diag_mask = (((row ^ col) & ~(HG_SUB - 1)) | jnp.where(col <= row, 0, 1)) == 0
    sub_keep = [jnp.where((row & (HG_SUB - 1)) == s_off, 1.0, 0.0).astype(BF16) for s_off in range(HG_SUB)]
    levels = []
    m = HG_SUB
    while m < C:
        bad = ((row ^ col) & ~(2 * m - 1)) | ((row & m) ^ m) | (col & m)
        levels.append((m, bad == 0))
        m *= 2

    def chunk(c):
        r0 = c * C
        fr = hf_ref[0, pl.ds(r0, C), :]
        f = lb + (1.0 - lb) * _sigmoid(fr)
        logf = jnp.log(f) * LOG2E
        kk_all = 1.0 - f
        t0 = logf.astype(BF16)
        r1 = logf - t0.astype(F32)
        t1 = r1.astype(BF16)
        t2 = (r1 - t1.astype(F32)).astype(BF16)
        b_all = _dot(tri, t0) + _dot(tri, t1) + _dot(tri, t2)
        q_all = hq_ref[0, pl.ds(r0, C), :].astype(F32)
        v_all = hi_ref[0, pl.ds(r0, C), :]
        g_all = hg_ref[0, pl.ds(r0, C), :].astype(F32)
        b_sc[c] = b_all
        zero = jnp.zeros((C, HG_D), BF16)

        def pair_nt(x0, x1):
            return jnp.concatenate([jnp.concatenate([x0, zero], axis=1),
                                    jnp.concatenate([zero, x1], axis=1)], axis=0)

        for h0 in range(0, N_HEADS, 2):
            heads = (h0, h0 + 1)
            sls = [slice(hd * HG_D, (hd + 1) * HG_D) for hd in heads]
            bs_ = [b_all[:, sl] for sl in sls]
            qs = [q_all[:, sl] for sl in sls]
            kks = [kk_all[:, sl] for sl in sls]
            vs = [v_all[:, sl] for sl in sls]

            a_ = []
            for b, q, kk, sl in zip(bs_, qs, kks, sls):
                kb = kk.astype(BF16)
                ms = []
                ks = []
                for s_off in range(HG_SUB):
                    bs = jnp.concatenate(
                        [jnp.broadcast_to(b_sc[c, i * HG_SUB + s_off:i * HG_SUB + s_off + 1, sl],
                                          (HG_SUB, HG_D)) for i in range(C // HG_SUB)], axis=0)
                    ms.append((q * jnp.exp2(jnp.minimum(b - bs, 0.0))).astype(BF16))
                    ks.append(kb * sub_keep[s_off])
                ad = _dot_nt(jnp.concatenate(ms, axis=1), jnp.concatenate(ks, axis=1))
                a_.append(jnp.where(diag_mask, ad, 0.0))
            for m_blk, mask in levels:
                qes = []
                kes = []
                for b, q, kk in zip(bs_, qs, kks):
                    d = b - _pair_reference(b, m_blk)
                    qes.append((q * jnp.exp2(d)).astype(BF16))
                    kes.append((kk * jnp.exp2(-d)).astype(BF16))
                al = _dot_nt(jnp.concatenate(qes, axis=1), pair_nt(*kes))
                a_ = [jnp.where(mask, al[:, j * C:(j + 1) * C], a_[j]) for j in range(2)]

            sts = [st_ref[hd] for hd in heads]
            qd = jnp.concatenate([(q * jnp.exp2(b)).astype(BF16) for b, q in zip(bs_, qs)], axis=1)
            a2 = jnp.concatenate([a.astype(BF16) for a in a_], axis=1)
            v2 = jnp.concatenate([jnp.concatenate([vs[0], zero], axis=1),
                                  jnp.concatenate([zero, vs[1]], axis=1)], axis=0)
            o2 = _dot(a2, v2) + _dot_nt(qd, pair_nt(*[st.astype(BF16) for st in sts]))

            for j, hd in enumerate(heads):
                b, kk, v, sl = bs_[j], kks[j], vs[j], sls[j]
                b_last = b[C - 1:C, :]
                kd = (kk * jnp.exp2(b_last - b)).astype(BF16)
                vt = v.astype(F32).T.astype(BF16)
                st_ref[hd] = sts[j] * jnp.exp2(b_last) + _dot(vt, kd)

                on = _rms(o2[:, j * HG_D:(j + 1) * HG_D], gain_ref[:, sl], HG_D)
                o_ref[0, pl.ds(r0, C), sl] = (on * g_all[:, sl]).astype(o_ref.dtype)

    for c in range(n_chunks):
        chunk(c)


def _out_ffn_kernel(x_ref, ymla_ref, yhg_ref, ymem_ref, gmla_ref, w_out_ref, gffn_ref,
                    w_gate_ref, w_up_ref, w_down_ref, o_ref):
    n = x_ref.shape[0] // 2
    halves = [pl.ds(0, n), pl.ds(n, n)]
    x1s = []
    for rows in halves:
        ymla = _rms(ymla_ref[rows, :].astype(F32), gmla_ref[...], WIDTH).astype(BF16)
        mix = (_dot(ymla, w_out_ref[0:WIDTH, :])
               + _dot(yhg_ref[rows, :], w_out_ref[WIDTH:2 * WIDTH, :])
               + _dot(ymem_ref[rows, :], w_out_ref[2 * WIDTH:3 * WIDTH, :]))
        x1s.append(x_ref[rows, :].astype(F32) + mix)
    acts = []
    for x1 in x1s:
        h2 = _rms(x1, gffn_ref[...], x1.shape[-1]).astype(BF16)
        g = _dot(h2, w_gate_ref[...])
        u = _dot(h2, w_up_ref[...])
        acts.append((g * _sigmoid(g) * u).astype(BF16))
    for rows, x1, act in zip(halves, x1s, acts):
        o_ref[rows, :] = (x1 + _dot(act, w_down_ref[...])).astype(o_ref.dtype)


def _full(shape):
    nd = len(shape)
    return pl.BlockSpec(shape, lambda *_: (0,) * nd)


def _params(sem):
    return pltpu.CompilerParams(dimension_semantics=sem, vmem_limit_bytes=VMEM_LIMIT)


def _row(v):
    return v.reshape(1, -1).astype(F32)


def _layer(x, mem, positions, layer, norm_mix, norm_mem, w_in, q_a_norm, w_uq, kv_a_norm, w_ukv,
           mla_q_norm, mla_k_norm, hg_lb_logits, hg_out_norm, w_mem_kv, mem_q_norm, mem_k_norm,
           mla_out_norm, mem_out_norm, w_out, norm_ffn, w_gate, w_up, w_down):
    B, S, D = x.shape
    M = mem.shape[1]
    T = B * S
    half = D_ROPE // 2
    H = N_HEADS

    assert w_in.shape == (D, W_ALL - D_ROPE)
    wrows = 256
    assert w_uq.shape == (Q_LORA, H * D_QK) and w_ukv.shape == (KV_LORA, H * (D_NOPE + D_V))
    small = [(Q_LORA, H * D_QK_PAD), (KV_LORA, H * D_NOPE), (KV_LORA, H * D_V), w_mem_kv.shape]
    w_all, uq2, w_uk, w_uv, w_mem_b = pl.pallas_call(
        _w_in_kernel,
        grid=(D // wrows,),
        in_specs=[pl.BlockSpec((w_in.shape[1], wrows), lambda i: (0, i)),
                  _full((H * D_QK, Q_LORA)), _full(w_ukv.shape), _full(w_mem_kv.shape)],
        out_specs=[pl.BlockSpec((wrows, W_ALL), lambda i: (i, 0))] + [_full(s) for s in small],
        out_shape=[jax.ShapeDtypeStruct((D, W_ALL), BF16)] + [jax.ShapeDtypeStruct(s, BF16) for s in small],
        compiler_params=_params(("arbitrary",)),
        name="w_in_layout",
    )(w_in.T, w_uq.T, w_ukv, w_mem_kv)

    pad = jnp.zeros((LANE - D_ROPE,), F32)

    def rotary_gains(g):
        g = g.astype(F32)
        g1 = jnp.concatenate([g[D_NOPE:], pad]).reshape(1, LANE)
        g2 = jnp.concatenate([g[D_NOPE + half:], g[D_NOPE:D_NOPE + half], pad]).reshape(1, LANE)
        return g[:D_NOPE].reshape(1, D_NOPE), g1, g2

    gq_nope, gq_r1, gq_r2 = rotary_gains(mla_q_norm)
    gk_nope, gk_r1, gk_r2 = rotary_gains(mla_k_norm)
    inv_freq = jnp.power(ROPE_THETA, -jnp.arange(half, dtype=F32) / half)
    invf = jnp.tile(inv_freq, LANE // half).reshape(1, LANE)
    phase = jnp.concatenate([jnp.zeros((D_ROPE,), F32), jnp.full((D_ROPE,), -jnp.pi / 2, F32)]).reshape(1, LANE)

    kmem, vmem = pl.pallas_call(
        _mem_kv_kernel,
        grid=(B,),
        in_specs=[pl.BlockSpec((1, M, D), lambda b: (b, 0, 0)),
                  _full((1, D)), _full((D, 2 * WIDTH)), _full((1, MEM_D))],
        out_specs=[pl.BlockSpec((1, M, WIDTH), lambda b: (b, 0, 0))] * 2,
        out_shape=[jax.ShapeDtypeStruct((B, M, WIDTH), BF16)] * 2,
        compiler_params=_params(("arbitrary",)),
        name="mem_kv",
    )(mem, _row(norm_mem), w_mem_b, _row(mem_k_norm))

    tm = min(512, S)
    assert S % tm == 0
    steps_per_batch = S // tm
    x2 = x.reshape(T, D)
    assert tm % LANE == 0
    pos2 = positions.reshape(T // tm, tm // LANE, LANE).astype(jnp.int32)
    pos_spec = pl.BlockSpec((1, tm // LANE, LANE), lambda i: (i, 0, 0))
    n_tok_steps = T // tm
    pos_next_spec = pl.BlockSpec((1, tm // LANE, LANE), lambda i: (jnp.minimum(i + 1, n_tok_steps - 1), 0, 0))
    sign = jnp.concatenate([jnp.full((D_ROPE,), -1.0, F32), jnp.ones((D_ROPE,), F32)]).reshape(1, LANE)
    tok = lambda w: pl.BlockSpec((tm, w), lambda i: (i, 0))
    col = lambda width, start: pl.BlockSpec((D, width), lambda i: (0, start // width))
    weight_specs = [col(WIDTH, 0), col(KV_LORA, 6 * WIDTH), col(WIDTH, WIDTH), col(WIDTH, 2 * WIDTH),
                    col(WIDTH, 3 * WIDTH), col(WIDTH, 4 * WIDTH), col(WIDTH, 5 * WIDTH)]
    weights = [w_all] * len(weight_specs)
    rest = [_row(q_a_norm), uq2, _row(kv_a_norm), w_uk, w_uv,
            gq_nope, gq_r1, gq_r2, gk_nope, gk_r1, gk_r2, _row(mem_q_norm)]
    mem_spec = pl.BlockSpec((1, M, WIDTH), lambda i: (i // steps_per_batch, 0, 0))
    vt_spec = pl.BlockSpec((1, H, 1, D_V, tm),
                           lambda i: (i // steps_per_batch, 0, i % steps_per_batch, 0, 0))
    q_all, k_all, vt_all, hq, hf, hi, hg, ymem = pl.pallas_call(
        _in_proj_kernel,
        grid=(T // tm,),
        in_specs=([tok(D), pos_spec, pos_next_spec, _full((1, LANE)), _full((1, LANE)), _full((1, LANE)),
                   _full((1, D))]
                  + weight_specs + [_full(r.shape) for r in rest]
                  + [mem_spec, mem_spec, _full((1, WIDTH))]),
        out_specs=[tok(H * D_QK_PAD), tok(H * D_QK_PAD), vt_spec] + [tok(WIDTH)] * 5,
        out_shape=[jax.ShapeDtypeStruct((T, H * D_QK_PAD), BF16)] * 2
        + [jax.ShapeDtypeStruct((B, H, steps_per_batch, D_V, tm), BF16)]
        + [jax.ShapeDtypeStruct((T, WIDTH), dt) for dt in (BF16, F32, BF16, BF16, BF16)],
        scratch_shapes=[pltpu.VMEM((tm, LANE), F32)] * 3,
        compiler_params=_params(("arbitrary",)),
        name="in_proj",
    )(x2, pos2, pos2, invf, phase, sign, _row(norm_mix), *weights, *rest, kmem, vmem, _row(mem_out_norm))

    bk = tm
    bq = 2 * bk
    assert S % bq == 0
    hpb = 2
    att_grid = (B, H // hpb, S // bq)
    n_att = att_grid[0] * att_grid[1] * att_grid[2]

    def cast_spec(w):
        rows = w.shape[0]
        rb = next(r for r in range(16 * -(-rows // (16 * n_att)), rows + 1, 16) if rows % r == 0)
        last = rows // rb - 1
        return pl.BlockSpec((rb, w.shape[1]),
                            lambda b, h, i: (jnp.minimum((b * att_grid[1] + h) * att_grid[2] + i, last), 0))

    ffn_weights = [w_out, w_gate, w_up, w_down]
    cast_specs = [cast_spec(w) for w in ffn_weights]
    y_mla, w_out_b, w_gate_b, w_up_b, w_down_b = pl.pallas_call(
        functools.partial(_attn_kernel, bq=bq, bk=bk, heads=hpb),
        grid=att_grid,
        in_specs=[pl.BlockSpec((1, bq, hpb * D_QK_PAD), lambda b, h, i: (b, i, h)),
                  pl.BlockSpec((1, S, hpb * D_QK_PAD), lambda b, h, i: (b, 0, h)),
                  pl.BlockSpec((1, hpb, S // bk, D_V, bk), lambda b, h, i: (b, h, 0, 0, 0))] + cast_specs,
        out_specs=[pl.BlockSpec((1, bq, hpb * D_V), lambda b, h, i: (b, i, h))] + cast_specs,
        out_shape=[jax.ShapeDtypeStruct((B, S, H * D_V), BF16)]
        + [jax.ShapeDtypeStruct(w.shape, BF16) for w in ffn_weights],
        scratch_shapes=[pltpu.VMEM((hpb, 1, bq), F32),
                        pltpu.VMEM((hpb, D_V + ONES_ROWS, bq), F32),
                        pltpu.VMEM((hpb, bk, bq), F32), pltpu.VMEM((hpb, bk, bq), F32),
                        pltpu.VMEM((hpb, 1, bq), F32), pltpu.VMEM((hpb, 1, bq), F32)],
        compiler_params=_params(("arbitrary", "arbitrary", "arbitrary")),
        name="mla_attn",
    )(q_all.reshape(B, S, H * D_QK_PAD), k_all.reshape(B, S, H * D_QK_PAD), vt_all, *ffn_weights)

    ts = min(1024, S)
    assert S % ts == 0 and ts % HG_CHUNK == 0
    seq = lambda: pl.BlockSpec((1, ts, WIDTH), lambda b, i: (b, i, 0))
    n_layers = hg_lb_logits.shape[0]
    y_hg = pl.pallas_call(
        functools.partial(_hgrn_kernel, n_chunks=ts // HG_CHUNK, layer=layer),
        grid=(B, S // ts),
        in_specs=[seq(), seq(), seq(), seq(), _full((n_layers, WIDTH)), _full((1, WIDTH))],
        out_specs=seq(),
        out_shape=jax.ShapeDtypeStruct((B, S, WIDTH), BF16),
        scratch_shapes=[pltpu.VMEM((N_HEADS, HG_D, HG_D), F32),
                        pltpu.VMEM((ts // HG_CHUNK, HG_CHUNK, WIDTH), F32)],
        compiler_params=_params(("arbitrary", "arbitrary")),
        name="hgrn",
    )(hq.reshape(B, S, WIDTH), hf.reshape(B, S, WIDTH), hi.reshape(B, S, WIDTH), hg.reshape(B, S, WIDTH),
      hg_lb_logits.astype(F32), _row(hg_out_norm))

    d_ff = w_gate.shape[1]
    once = lambda shape: pl.BlockSpec(shape, lambda i: (0, 0), pipeline_mode=pl.Buffered(1))
    out = pl.pallas_call(
        _out_ffn_kernel,
        grid=(T // tm,),
        in_specs=[tok(D), tok(WIDTH), tok(WIDTH), tok(WIDTH), _full((1, WIDTH)), once((3 * WIDTH, D)),
                  _full((1, D)), once((D, d_ff)), once((D, d_ff)), once((d_ff, D))],
        out_specs=tok(D),
        out_shape=jax.ShapeDtypeStruct((T, D), x.dtype),
        compiler_params=_params(("arbitrary",)),
        name="out_ffn",
    )(x2, y_mla.reshape(T, WIDTH), y_hg.reshape(T, WIDTH), ymem, _row(mla_out_norm), w_out_b,
      _row(norm_ffn), w_gate_b, w_up_b, w_down_b)
    return out.reshape(B, S, D)


def kernel(x, mem, positions, norm_mix, norm_mem, w_in, q_a_norm, w_uq, kv_a_norm, w_ukv, mla_q_norm, mla_k_norm, hg_lb_logits, hg_out_norm, w_mem_kv, mem_q_norm, mem_k_norm, mla_out_norm, mem_out_norm, w_out, norm_ffn, w_gate, w_up, w_down):
    depth = w_in.shape[0]
    for l in range(depth):
        x = _layer(x, mem, positions, l, norm_mix[l], norm_mem[l], w_in[l], q_a_norm[l], w_uq[l],
                   kv_a_norm[l], w_ukv[l], mla_q_norm[l], mla_k_norm[l], hg_lb_logits, hg_out_norm[l],
                   w_mem_kv[l], mem_q_norm[l], mem_k_norm[l], mla_out_norm[l], mem_out_norm[l],
                   w_out[l], norm_ffn[l], w_gate[l], w_up[l], w_down[l])
    return x
```
